```python
import jax, jax.numpy as jnp
from jax import lax
import numpy as np

D_MODEL = 1024
BATCH = 8
SEQ = 4096
DEPTH = 4

N_MIXERS = 3
N_A = (DEPTH + 2) // 3
N_B = (DEPTH + 1) // 3
N_C = DEPTH // 3

ROPE_THETA = 10000.0
RMS_EPS = 1e-6
NEG_INF = -1e30
ADA_INIT_STD = 0.02

MLA_HEADS = 8
MLA_NOPE = 128
MLA_ROPE = 64
MLA_V = 128
MLA_Q_LORA = 256
MLA_KV_LORA = 128
MLA_BLOCK_Q = 128
MLA_WIDTH = MLA_HEADS * MLA_V
MLA_IN = MLA_Q_LORA + MLA_KV_LORA + MLA_ROPE + MLA_WIDTH

SWA_HEADS = 16
SWA_KV_HEADS = 2
SWA_HEAD_DIM = 64
SWA_WINDOW = 128
SWA_BLOCK_Q = 128
SWA_WIDTH = SWA_HEADS * SWA_HEAD_DIM
SWA_KV_W = SWA_KV_HEADS * SWA_HEAD_DIM
SWA_IN = SWA_WIDTH + 2 * SWA_KV_W + SWA_WIDTH

NSA_HEADS = 16
NSA_KV_HEADS = 4
NSA_HEAD_DIM = 64
NSA_CMP_BLOCK = 32
NSA_CMP_STRIDE = 16
NSA_CMP_HIDDEN = 128
NSA_SEL_BLOCK = 64
NSA_N_SELECT = 16
NSA_WINDOW = 512
NSA_BLOCK_Q = 64
NSA_FORCE_BONUS = 1e4
NSA_WIDTH = NSA_HEADS * NSA_HEAD_DIM
NSA_KV_W = NSA_KV_HEADS * NSA_HEAD_DIM
NSA_IN = NSA_WIDTH + 6 * NSA_KV_W + 3 * NSA_HEADS + NSA_WIDTH

kernel_name = 'hybrid_mla_swa_nsa_interleaved'


def rmsnorm(x, g):
    xf = x.astype(jnp.float32)
    y = xf * lax.rsqrt(jnp.mean(xf * xf, axis=-1, keepdims=True) + RMS_EPS)
    return (y * g.astype(jnp.float32)).astype(x.dtype)


def rope(x, positions):
    half = x.shape[-1] // 2
    inv_freq = ROPE_THETA ** (-jnp.arange(half, dtype=jnp.float32) / half)
    ang = positions.astype(jnp.float32)[..., None] * inv_freq
    cos = jnp.cos(ang)[:, :, None, :]
    sin = jnp.sin(ang)[:, :, None, :]
    xf = x.astype(jnp.float32)
    x1, x2 = xf[..., :half], xf[..., half:]
    return jnp.concatenate([x1 * cos - x2 * sin, x2 * cos + x1 * sin], axis=-1).astype(x.dtype)


def dense_causal_attention(q, k, v, block_q):
    B, S, H, dqk = q.shape
    nb = S // block_q
    scale = dqk ** -0.5
    q_blocks = jnp.moveaxis(q.reshape(B, nb, block_q, H, dqk), 1, 0)
    starts = jnp.arange(nb, dtype=jnp.int32) * block_q
    key_pos = jnp.arange(S, dtype=jnp.int32)

    def one_block(args):
        qb, start = args
        s = jnp.einsum('bqhd,bkhd->bhqk', qb, k, preferred_element_type=jnp.float32) * scale
        q_pos = start + jnp.arange(block_q, dtype=jnp.int32)
        s = jnp.where(q_pos[:, None] >= key_pos[None, :], s, NEG_INF)
        p = jax.nn.softmax(s, axis=-1).astype(v.dtype)
        return jnp.einsum('bhqk,bkhd->bqhd', p, v)

    out = lax.map(one_block, (q_blocks, starts))
    return jnp.moveaxis(out, 0, 1).reshape(B, S, H, v.shape[-1])


def banded_attention(q, k, v, window, block_q, sinks=None):
    B, S, KV, G, d = q.shape
    nb = S // block_q
    span = window + block_q
    scale = d ** -0.5
    pad = ((0, 0), (window, 0), (0, 0), (0, 0))
    k_pad = jnp.pad(k, pad)
    v_pad = jnp.pad(v, pad)
    q_blocks = jnp.moveaxis(q.reshape(B, nb, block_q, KV, G, d), 1, 0)
    starts = jnp.arange(nb, dtype=jnp.int32) * block_q

    def one_block(args):
        qb, start = args
        kb = lax.dynamic_slice_in_dim(k_pad, start, span, axis=1)
        vb = lax.dynamic_slice_in_dim(v_pad, start, span, axis=1)
        s = jnp.einsum('bqkgd,bskd->bkgqs', qb, kb, preferred_element_type=jnp.float32) * scale
        q_pos = start + jnp.arange(block_q, dtype=jnp.int32)
        k_pos = start - window + jnp.arange(span, dtype=jnp.int32)
        diff = q_pos[:, None] - k_pos[None, :]
        mask = (diff >= 0) & (diff < window) & (k_pos[None, :] >= 0)
        s = jnp.where(mask, s, NEG_INF)
        if sinks is not None:
            sink = jnp.broadcast_to(sinks.astype(jnp.float32).reshape(KV, G)[None, :, :, None, None],
                                    s.shape[:-1] + (1,))
            p = jax.nn.softmax(jnp.concatenate([s, sink], axis=-1), axis=-1)[..., :-1]
        else:
            p = jax.nn.softmax(s, axis=-1)
        return jnp.einsum('bkgqs,bskd->bqkgd', p.astype(v.dtype), vb)

    out = lax.map(one_block, (q_blocks, starts))
    return jnp.moveaxis(out, 0, 1).reshape(B, S, KV, G, d)


def mla_mixer(h, positions, w_in, q_norm_g, kv_norm_g, w_q_b, w_kv_b, w_out):
    B, S, _ = h.shape
    o1 = MLA_Q_LORA
    o2 = o1 + MLA_KV_LORA
    o3 = o2 + MLA_ROPE
    q_a, kv_a, k_pe, gate = jnp.split(h @ w_in, [o1, o2, o3], axis=-1)
    q = (rmsnorm(q_a, q_norm_g) @ w_q_b).reshape(B, S, MLA_HEADS, MLA_NOPE + MLA_ROPE)
    kv = (rmsnorm(kv_a, kv_norm_g) @ w_kv_b).reshape(B, S, MLA_HEADS, MLA_NOPE + MLA_V)
    q = jnp.concatenate([q[..., :MLA_NOPE], rope(q[..., MLA_NOPE:], positions)], axis=-1)
    k_pe = rope(k_pe[:, :, None, :], positions)
    k = jnp.concatenate([kv[..., :MLA_NOPE],
                         jnp.broadcast_to(k_pe, (B, S, MLA_HEADS, MLA_ROPE))], axis=-1)
    v = kv[..., MLA_NOPE:]
    o = dense_causal_attention(q, k, v, MLA_BLOCK_Q).reshape(B, S, MLA_WIDTH)
    return (o * jax.nn.silu(gate)) @ w_out


def swa_mixer(h, positions, w_in, sinks, w_out):
    B, S, _ = h.shape
    G = SWA_HEADS // SWA_KV_HEADS
    q, k, v, gate = jnp.split(h @ w_in, [SWA_WIDTH, SWA_WIDTH + SWA_KV_W, SWA_WIDTH + 2 * SWA_KV_W], axis=-1)
    q = rope(q.reshape(B, S, SWA_HEADS, SWA_HEAD_DIM), positions).reshape(B, S, SWA_KV_HEADS, G, SWA_HEAD_DIM)
    k = rope(k.reshape(B, S, SWA_KV_HEADS, SWA_HEAD_DIM), positions)
    v = v.reshape(B, S, SWA_KV_HEADS, SWA_HEAD_DIM)
    o = banded_attention(q, k, v, SWA_WINDOW, SWA_BLOCK_Q, sinks).reshape(B, S, SWA_WIDTH)
    return (o * jax.nn.silu(gate)) @ w_out


def compress_blocks(x, pos_emb, w1, w2):
    B, S, KV, d = x.shape
    r = NSA_CMP_BLOCK // NSA_CMP_STRIDE
    n_chunks = S // NSA_CMP_STRIDE
    nc = n_chunks - r + 1
    chunks = x.reshape(B, n_chunks, NSA_CMP_STRIDE, KV, d)
    blocks = jnp.concatenate([chunks[:, i:i + nc] for i in range(r)], axis=2)
    blocks = blocks + pos_emb[None, None, :, None, :]
    flat = jnp.moveaxis(blocks, 3, 2).reshape(B, nc, KV, NSA_CMP_BLOCK * d)
    return jax.nn.silu(flat @ w1) @ w2


def nsa_mixer(h, positions, w_in, cmp_pos, w_cmp_k1, w_cmp_k2, w_cmp_v1, w_cmp_v2, w_out):
    B, S, _ = h.shape
    KV, G, d = NSA_KV_HEADS, NSA_HEADS // NSA_KV_HEADS, NSA_HEAD_DIM
    sizes = [NSA_WIDTH] + [NSA_KV_W] * 6 + [3 * NSA_HEADS]
    splits = [int(s) for s in np.cumsum(sizes)]
    q, k_cmp, v_cmp, k_slc, v_slc, k_win, v_win, g_branch, gate = jnp.split(h @ w_in, splits, axis=-1)
    kv_shape = (B, S, KV, d)
    q = rope(q.reshape(B, S, NSA_HEADS, d), positions).reshape(B, S, KV, G, d)

    k_c = compress_blocks(k_cmp.reshape(kv_shape), cmp_pos, w_cmp_k1, w_cmp_k2)
    v_c = compress_blocks(v_cmp.reshape(kv_shape), cmp_pos, w_cmp_v1, w_cmp_v2)
    nc = k_c.shape[1]
    ns = S // NSA_SEL_BLOCK
    n_top = min(NSA_N_SELECT, ns)
    k_sel = jnp.moveaxis(rope(k_slc.reshape(kv_shape), positions).reshape(B, ns, NSA_SEL_BLOCK, KV, d), 3, 1)
    v_sel = jnp.moveaxis(v_slc.reshape(B, ns, NSA_SEL_BLOCK, KV, d), 3, 1)

    cmp_start = jnp.arange(nc, dtype=jnp.int32) * NSA_CMP_STRIDE
    sel_start = jnp.arange(ns, dtype=jnp.int32) * NSA_SEL_BLOCK
    overlap = jnp.clip(jnp.minimum(cmp_start[:, None] + NSA_CMP_BLOCK, sel_start[None, :] + NSA_SEL_BLOCK)
                       - jnp.maximum(cmp_start[:, None], sel_start[None, :]), 0, None
                       ).astype(jnp.float32) / NSA_CMP_BLOCK
    cmp_end = cmp_start + NSA_CMP_BLOCK - 1
    scale = d ** -0.5
    nb = S // NSA_BLOCK_Q
    q_blocks = jnp.moveaxis(q.reshape(B, nb, NSA_BLOCK_Q, KV, G, d), 1, 0)
    starts = jnp.arange(nb, dtype=jnp.int32) * NSA_BLOCK_Q
    b_idx = jnp.arange(B)[:, None, None, None]
    h_idx = jnp.arange(KV)[None, :, None, None]
    sel_offsets = jnp.arange(NSA_SEL_BLOCK, dtype=jnp.int32)
    blk = jnp.arange(ns, dtype=jnp.int32)

    def one_block(args):
        qb, start = args
        q_pos = start + jnp.arange(NSA_BLOCK_Q, dtype=jnp.int32)
        s_c = jnp.einsum('bqkgd,bnkd->bkgqn', qb, k_c, preferred_element_type=jnp.float32) * scale
        valid_c = cmp_end[None, :] <= q_pos[:, None]
        p_c = jax.nn.softmax(jnp.where(valid_c, s_c, NEG_INF), axis=-1) * valid_c
        o_c = jnp.einsum('bkgqn,bnkd->bqkgd', p_c.astype(v_c.dtype), v_c)
        imp = jnp.einsum('bkgqn,ns->bkqs', p_c, overlap)
        q_blk = q_pos // NSA_SEL_BLOCK
        causal = blk[None, :] <= q_blk[:, None]
        forced = (blk[None, :] == 0) | (blk[None, :] == q_blk[:, None]) | (blk[None, :] == q_blk[:, None] - 1)
        imp = jnp.where(causal, imp + jnp.where(forced, NSA_FORCE_BONUS, 0.0), -1.0)
        _, top_idx = lax.top_k(imp, n_top)
        m = n_top * NSA_SEL_BLOCK
        k_g = k_sel[b_idx, h_idx, top_idx].reshape(B, KV, NSA_BLOCK_Q, m, d)
        v_g = v_sel[b_idx, h_idx, top_idx].reshape(B, KV, NSA_BLOCK_Q, m, d)
        tok = (top_idx[..., None] * NSA_SEL_BLOCK + sel_offsets).reshape(B, KV, NSA_BLOCK_Q, m)
        valid_s = tok <= q_pos[None, None, :, None]
        s_s = jnp.einsum('bqkgd,bkqmd->bkgqm', qb, k_g, preferred_element_type=jnp.float32) * scale
        p_s = jax.nn.softmax(jnp.where(valid_s[:, :, None], s_s, NEG_INF), axis=-1)
        o_s = jnp.einsum('bkgqm,bkqmd->bqkgd', p_s.astype(v_g.dtype), v_g)
        return o_c, o_s

    o_c, o_s = lax.map(one_block, (q_blocks, starts))
    o_c = jnp.moveaxis(o_c, 0, 1).reshape(B, S, KV, G, d)
    o_s = jnp.moveaxis(o_s, 0, 1).reshape(B, S, KV, G, d)
    o_w = banded_attention(q, rope(k_win.reshape(kv_shape), positions), v_win.reshape(kv_shape),
                           NSA_WINDOW, NSA_BLOCK_Q)
    g = jax.nn.sigmoid(g_branch.astype(jnp.float32)).reshape(B, S, KV, G, 3).astype(h.dtype)
    o = g[..., 0:1] * o_c + g[..., 1:2] * o_s + g[..., 2:3] * o_w
    return (o.reshape(B, S, NSA_WIDTH) * jax.nn.silu(gate)) @ w_out


def setup_inputs(seed: int = 0) -> dict:
    key = jax.random.key(seed)
    ks = iter(jax.random.split(key, 32))

    def dense(shape, fan_in):
        return jax.random.normal(next(ks), shape, jnp.float32) * fan_in ** -0.5

    def gain(shape):
        return 1.0 + 0.02 * jax.random.normal(next(ks), shape, jnp.float32)

    x = jax.random.normal(next(ks), (BATCH, SEQ, D_MODEL), jnp.float32)
    c = jax.random.normal(next(ks), (BATCH, D_MODEL), jnp.float32)
    offsets = jax.random.randint(next(ks), (BATCH, 1), 0, 512, dtype=jnp.int32)
    positions = offsets + jnp.arange(SEQ, dtype=jnp.int32)[None, :]
    return {
        'x': x,
        'c': c,
        'positions': positions,
        'norm_g': gain((DEPTH, D_MODEL)),
        'ada_w': ADA_INIT_STD * jax.random.normal(next(ks), (DEPTH, D_MODEL, 3 * D_MODEL), jnp.float32),
        'ada_b': 0.02 * jax.random.normal(next(ks), (DEPTH, 3 * D_MODEL), jnp.float32),
        'mla_w_in': dense((N_A, D_MODEL, MLA_IN), D_MODEL),
        'mla_q_norm_g': gain((N_A, MLA_Q_LORA)),
        'mla_kv_norm_g': gain((N_A, MLA_KV_LORA)),
        'mla_w_q_b': dense((N_A, MLA_Q_LORA, MLA_HEADS * (MLA_NOPE + MLA_ROPE)), MLA_Q_LORA),
        'mla_w_kv_b': dense((N_A, MLA_KV_LORA, MLA_HEADS * (MLA_NOPE + MLA_V)), MLA_KV_LORA),
        'mla_w_out': dense((N_A, MLA_WIDTH, D_MODEL), MLA_WIDTH),
        'swa_w_in': dense((N_B, D_MODEL, SWA_IN), D_MODEL),
        'swa_sinks': jax.random.normal(next(ks), (N_B, SWA_HEADS), jnp.float32),
        'swa_w_out': dense((N_B, SWA_WIDTH, D_MODEL), SWA_WIDTH),
        'nsa_w_in': dense((N_C, D_MODEL, NSA_IN), D_MODEL),
        'nsa_cmp_pos': 0.1 * jax.random.normal(next(ks), (N_C, NSA_CMP_BLOCK, NSA_HEAD_DIM), jnp.float32),
        'nsa_w_cmp_k1': dense((N_C, NSA_CMP_BLOCK * NSA_HEAD_DIM, NSA_CMP_HIDDEN), NSA_CMP_BLOCK * NSA_HEAD_DIM),
        'nsa_w_cmp_k2': dense((N_C, NSA_CMP_HIDDEN, NSA_HEAD_DIM), NSA_CMP_HIDDEN),
        'nsa_w_cmp_v1': dense((N_C, NSA_CMP_BLOCK * NSA_HEAD_DIM, NSA_CMP_HIDDEN), NSA_CMP_BLOCK * NSA_HEAD_DIM),
        'nsa_w_cmp_v2': dense((N_C, NSA_CMP_HIDDEN, NSA_HEAD_DIM), NSA_CMP_HIDDEN),
        'nsa_w_out': dense((N_C, NSA_WIDTH, D_MODEL), NSA_WIDTH),
        'final_norm_g': gain((D_MODEL,)),
    }


def reference(x, c, positions, norm_g, ada_w, ada_b,
              mla_w_in, mla_q_norm_g, mla_kv_norm_g, mla_w_q_b, mla_w_kv_b, mla_w_out,
              swa_w_in, swa_sinks, swa_w_out,
              nsa_w_in, nsa_cmp_pos, nsa_w_cmp_k1, nsa_w_cmp_k2, nsa_w_cmp_v1, nsa_w_cmp_v2, nsa_w_out,
              final_norm_g):
    cond = jax.nn.silu(c)
    for i in range(DEPTH):
        shift, scale, gate = jnp.split(cond @ ada_w[i] + ada_b[i], 3, axis=-1)
        h = rmsnorm(x, norm_g[i]) * (1 + scale[:, None, :]) + shift[:, None, :]
        kind, j = i % N_MIXERS, i // N_MIXERS
        if kind == 0:
            y = mla_mixer(h, positions, mla_w_in[j], mla_q_norm_g[j], mla_kv_norm_g[j],
                          mla_w_q_b[j], mla_w_kv_b[j], mla_w_out[j])
        elif kind == 1:
            y = swa_mixer(h, positions, swa_w_in[j], swa_sinks[j], swa_w_out[j])
        else:
            y = nsa_mixer(h, positions, nsa_w_in[j], nsa_cmp_pos[j], nsa_w_cmp_k1[j], nsa_w_cmp_k2[j],
                          nsa_w_cmp_v1[j], nsa_w_cmp_v2[j], nsa_w_out[j])
        x = x + gate[:, None, :] * y
    return rmsnorm(x, final_norm_g)
```

```python
import functools

import jax
import jax.numpy as jnp
import numpy as np
from jax import lax
from jax.experimental import pallas as pl
from jax.experimental.pallas import tpu as pltpu

F32 = jnp.float32
BF16 = jnp.bfloat16

ROPE_THETA = 10000.0
RMS_EPS = 1e-6
NEG_INF = -1e30

MLA_HEADS = 8
MLA_NOPE = 128
MLA_ROPE = 64
MLA_V = 128
MLA_Q_LORA = 256
MLA_KV_LORA = 128

SWA_HEADS = 16
SWA_KV_HEADS = 2
SWA_HEAD_DIM = 64
SWA_WINDOW = 128

NSA_HEADS = 16
NSA_KV_HEADS = 4
NSA_HEAD_DIM = 64
NSA_CMP_BLOCK = 32
NSA_CMP_STRIDE = 16
NSA_CMP_HIDDEN = 128
NSA_SEL_BLOCK = 64
NSA_N_SELECT = 16
NSA_WINDOW = 512
NSA_FORCE_BONUS = 1e4

LANES = 128
ROPE_HALF = 32
VMEM_LIMIT = 56 * 1024 * 1024

PROJ_ROWS = 256
OUT_ROWS = 512
MLA_TQ = 256
SWA_TQ = 128
NSA_TQ = 128
NSA_SEL_TK = 512


def _dot(a, b):
    return jnp.dot(a, b, preferred_element_type=F32)


def _dot_nt(a, b):
    return lax.dot_general(a, b, (((1,), (1,)), ((), ())), preferred_element_type=F32)


def _silu(x):
    return x / (1.0 + jnp.exp(-x))


def _sigmoid(x):
    return 1.0 / (1.0 + jnp.exp(-x))


def _rms(x, g):
    return x * lax.rsqrt(jnp.mean(x * x, axis=-1, keepdims=True) + RMS_EPS) * g


def _params(n_axes):
    return pltpu.CompilerParams(dimension_semantics=("arbitrary",) * n_axes,
                                vmem_limit_bytes=VMEM_LIMIT)


def _rope128(x, cos, sin_signed):
    lane = lax.broadcasted_iota(jnp.int32, x.shape, 1)
    lower_half = (lane & (2 * ROPE_HALF - 1)) < ROPE_HALF
    partner = jnp.where(lower_half, pltpu.roll(x, LANES - ROPE_HALF, axis=1),
                        pltpu.roll(x, ROPE_HALF, axis=1))
    return x * cos + partner * sin_signed


def _rope_table_kernel(pos_ref, freq_ref, sign_ref, cos_ref, sin_ref):
    ang = pos_ref[0] * freq_ref[...]
    cos_ref[0] = jnp.cos(ang)
    sin_ref[0] = jnp.sin(ang) * sign_ref[...]


def _rope_tables(positions):
    B, S = positions.shape
    inv_freq = ROPE_THETA ** (-jnp.arange(ROPE_HALF, dtype=F32) / ROPE_HALF)
    freq = jnp.tile(inv_freq, LANES // ROPE_HALF)[None, :]
    sign = jnp.tile(jnp.concatenate([-jnp.ones(ROPE_HALF, F32), jnp.ones(ROPE_HALF, F32)]),
                    LANES // (2 * ROPE_HALF))[None, :]
    pos = positions.astype(F32)[..., None]
    ts = min(S, 512)
    spec = pl.BlockSpec((1, ts, LANES), lambda b, s: (b, s, 0))
    return pl.pallas_call(
        _rope_table_kernel,
        out_shape=(jax.ShapeDtypeStruct((B, S, LANES), F32),) * 2,
        grid=(B, S // ts),
        in_specs=[pl.BlockSpec((1, ts, 1), lambda b, s: (b, s, 0)),
                  pl.BlockSpec((1, LANES), lambda b, s: (0, 0)),
                  pl.BlockSpec((1, LANES), lambda b, s: (0, 0))],
        out_specs=(spec, spec),
        compiler_params=_params(2),
        name="rope_tables",
    )(pos, freq, sign)


def _ada_kernel(c_ref, w_ref, b_ref, o_ref):
    cond = _silu(c_ref[...]).astype(BF16)
    o_ref[0] = _dot(cond, w_ref[0]) + b_ref[0]


def _ada_modulation(c, ada_w, ada_b):
    depth, D, D3 = ada_w.shape
    B = c.shape[0]
    Bp = -(-B // 16) * 16
    c_pad = jnp.pad(c, ((0, Bp - B), (0, 0)))
    tn = 1024
    out = pl.pallas_call(
        _ada_kernel,
        out_shape=jax.ShapeDtypeStruct((depth, Bp, D3), F32),
        grid=(depth, D3 // tn),
        in_specs=[pl.BlockSpec((Bp, D), lambda i, n: (0, 0)),
                  pl.BlockSpec((1, D, tn), lambda i, n: (i, 0, n)),
                  pl.BlockSpec((1, 1, tn), lambda i, n: (i, 0, n))],
        out_specs=pl.BlockSpec((1, Bp, tn), lambda i, n: (i, 0, n)),
        compiler_params=_params(2),
        name="ada_modulation",
    )(c_pad, ada_w.astype(BF16), ada_b[:, None, :])
    return out[:, :B]


def _out_proj_kernel(og_ref, w_ref, x_ref, gate_ref, fg_ref, o_ref, *, final):
    y = _dot(og_ref[...], w_ref[...])
    xn = x_ref[...] + gate_ref[0] * y
    if final:
        xn = _rms(xn, fg_ref[...])
    o_ref[...] = xn


def _out_proj(og, w_out, x2, gate_c, final_g, S, final):
    T, D = x2.shape
    W = og.shape[1]
    tm = OUT_ROWS
    tpb = S // tm
    return pl.pallas_call(
        functools.partial(_out_proj_kernel, final=final),
        out_shape=jax.ShapeDtypeStruct((T, D), F32),
        grid=(T // tm,),
        in_specs=[pl.BlockSpec((tm, W), lambda i: (i, 0)),
                  pl.BlockSpec((W, D), lambda i: (0, 0)),
                  pl.BlockSpec((tm, D), lambda i: (i, 0)),
                  pl.BlockSpec((1, 1, D), lambda i: (i // tpb, 0, 0)),
                  pl.BlockSpec((1, D), lambda i: (0, 0))],
        out_specs=pl.BlockSpec((tm, D), lambda i: (i, 0)),
        compiler_params=_params(1),
        name="out_proj",
    )(og, w_out.astype(BF16), x2, gate_c[:, None, :], final_g[None, :])


def _modulated_norm(x_ref, g_ref, scale_ref, shift_ref):
    y = _rms(x_ref[...], g_ref[...])
    return (y * (1.0 + scale_ref[0]) + shift_ref[0]).astype(BF16)


def _store_heads(dst_ref, val, n_heads, width, dtype):
    for h in range(n_heads):
        dst_ref[0, h] = val[:, h * width:(h + 1) * width].astype(dtype)


def _store_slabs(dst_ref, val, n_heads, d):
    for h in range(n_heads):
        for c in range(val.shape[1] // LANES):
            dst_ref[0, h, c] = val[h * d:(h + 1) * d, c * LANES:(c + 1) * LANES].astype(BF16)


def _rope_wide(val, cos, sin_signed):
    return jnp.concatenate(
        [_rope128(val[:, c * LANES:(c + 1) * LANES], cos, sin_signed)
         for c in range(val.shape[1] // LANES)], axis=1)


def _mla_proj_kernel(x_ref, scale_ref, shift_ref, g_ref, cos_ref, sin_ref,
                     wqa_ref, wkva_ref, wkpe_ref, wgate_ref, qg_ref, kvg_ref,
                     wqb_ref, wkb_ref, wvbT_ref,
                     q_ref, k_ref, vT_ref, sg_ref):
    h = _modulated_norm(x_ref, g_ref, scale_ref, shift_ref)
    cos, sin = cos_ref[0], sin_ref[0]
    head_w = MLA_NOPE + LANES
    q_scale = (MLA_NOPE + MLA_ROPE) ** -0.5

    qn = _rms(_dot(h, wqa_ref[...]), qg_ref[...]).astype(BF16)
    q = _dot(qn, wqb_ref[...]) * q_scale
    for hd in range(MLA_HEADS):
        lo = hd * head_w
        q_ref[:, lo:lo + MLA_NOPE] = q[:, lo:lo + MLA_NOPE].astype(BF16)
        q_ref[:, lo + MLA_NOPE:lo + head_w] = _rope128(
            q[:, lo + MLA_NOPE:lo + head_w], cos, sin).astype(BF16)

    kvn = _rms(_dot(h, wkva_ref[...]), kvg_ref[...]).astype(BF16)
    kn = _dot(kvn, wkb_ref[...])
    kpe = _rope128(_dot(h, wkpe_ref[...]), cos, sin).astype(BF16)
    for hd in range(MLA_HEADS):
        lo = hd * head_w
        k_ref[:, lo:lo + MLA_NOPE] = kn[:, hd * MLA_NOPE:(hd + 1) * MLA_NOPE].astype(BF16)
        k_ref[:, lo + MLA_NOPE:lo + head_w] = kpe
    _store_slabs(vT_ref, _dot_nt(wvbT_ref[...], kvn), MLA_HEADS, MLA_V)

    sg_ref[...] = _silu(_dot(h, wgate_ref[...]))


def _mla_attn_kernel(q_ref, k_ref, vT_ref, sg_ref, o_ref):
    qi = pl.program_id(2)
    tq = tk = MLA_TQ
    slabs = tk // LANES
    q = q_ref[0]

    def step(kb, carry, masked):
        m, l, acc = carry
        k = k_ref[0, pl.ds(pl.multiple_of(kb * tk, tk), tk), :]
        s = _dot_nt(k, q)
        if masked:
            kpos = lax.broadcasted_iota(jnp.int32, s.shape, 0)
            qpos = lax.broadcasted_iota(jnp.int32, s.shape, 1)
            s = jnp.where(kpos <= qpos, s, NEG_INF)
        m_new = jnp.maximum(m, jnp.max(s, axis=0, keepdims=True))
        alpha = jnp.exp(m - m_new)
        p = jnp.exp(s - m_new)
        l = alpha * l + jnp.sum(p, axis=0, keepdims=True)
        vt = jnp.concatenate([vT_ref[0, 0, kb * slabs + c] for c in range(slabs)], axis=1)
        acc = alpha * acc + _dot(vt, p.astype(BF16))
        return m_new, l, acc

    init = (jnp.full((1, tq), NEG_INF, F32), jnp.zeros((1, tq), F32), jnp.zeros((MLA_V, tq), F32))
    carry = lax.fori_loop(0, qi, lambda kb, c: step(kb, c, False), init)
    m, l, acc = step(qi, carry, True)
    o = (acc / l).T
    o_ref[0] = (o * sg_ref[0]).astype(BF16)


def _mla_layer(x2, B, S, norm_g, scale, shift, cos, sin, w_in, q_norm_g, kv_norm_g, w_q_b, w_kv_b):
    T, D = x2.shape
    H = MLA_HEADS
    head_w = MLA_NOPE + LANES
    o1 = MLA_Q_LORA
    o2 = o1 + MLA_KV_LORA
    o3 = o2 + MLA_ROPE
    w_qa, w_kva, w_kpe, w_gate = (w_in[:, :o1], w_in[:, o1:o2], w_in[:, o2:o3], w_in[:, o3:])
    w_kpe = jnp.pad(w_kpe, ((0, 0), (0, LANES - MLA_ROPE)))
    wq = w_q_b.reshape(MLA_Q_LORA, H, MLA_NOPE + MLA_ROPE)
    wq = jnp.pad(wq, ((0, 0), (0, 0), (0, head_w - MLA_NOPE - MLA_ROPE))).reshape(MLA_Q_LORA, H * head_w)
    wkv = w_kv_b.reshape(MLA_KV_LORA, H, MLA_NOPE + MLA_V)
    w_kb = wkv[:, :, :MLA_NOPE].reshape(MLA_KV_LORA, H * MLA_NOPE)
    w_vbT = wkv[:, :, MLA_NOPE:].reshape(MLA_KV_LORA, H * MLA_V).T
    W = H * MLA_V

    tm = PROJ_ROWS
    tpb = S // tm
    full = lambda shape: pl.BlockSpec(shape, lambda i: (0,) * len(shape))
    rows = lambda w: pl.BlockSpec((tm, w), lambda i: (i, 0))
    per_b = pl.BlockSpec((1, 1, D), lambda i: (i // tpb, 0, 0))
    tab = pl.BlockSpec((1, tm, LANES), lambda i: (i // tpb, i % tpb, 0))
    q, k, vT, sg = pl.pallas_call(
        _mla_proj_kernel,
        out_shape=(jax.ShapeDtypeStruct((T, H * head_w), BF16),
                   jax.ShapeDtypeStruct((T, H * head_w), BF16),
                   jax.ShapeDtypeStruct((B, H, S // LANES, MLA_V, LANES), BF16),
                   jax.ShapeDtypeStruct((T, W), F32)),
        grid=(T // tm,),
        in_specs=[rows(D), per_b, per_b, full((1, D)), tab, tab,
                  full((D, o1)), full((D, MLA_KV_LORA)), full((D, LANES)), full((D, W)),
                  full((1, o1)), full((1, MLA_KV_LORA)),
                  full((o1, H * head_w)), full((MLA_KV_LORA, H * MLA_NOPE)), full((W, MLA_KV_LORA))],
        out_specs=(rows(H * head_w), rows(H * head_w),
                   pl.BlockSpec((1, H, tm // LANES, MLA_V, LANES),
                                lambda i: (i // tpb, 0, i % tpb, 0, 0)),
                   rows(W)),
        compiler_params=_params(1),
        name="mla_proj",
    )(x2, scale[:, None, :], shift[:, None, :], norm_g[None, :], cos, sin,
      w_qa.astype(BF16), w_kva.astype(BF16), w_kpe.astype(BF16), w_gate.astype(BF16),
      q_norm_g[None, :], kv_norm_g[None, :],
      wq.astype(BF16), w_kb.astype(BF16), w_vbT.astype(BF16))

    tq = MLA_TQ
    q3 = q.reshape(B, S, H * head_w)
    k3 = k.reshape(B, S, H * head_w)
    sg3 = sg.reshape(B, S, W)
    og = pl.pallas_call(
        _mla_attn_kernel,
        out_shape=jax.ShapeDtypeStruct((B, S, W), BF16),
        grid=(B, H, S // tq),
        in_specs=[pl.BlockSpec((1, tq, head_w), lambda b, h, i: (b, i, h)),
                  pl.BlockSpec((1, S, head_w), lambda b, h, i: (b, 0, h)),
                  pl.BlockSpec((1, 1, S // LANES, MLA_V, LANES), lambda b, h, i: (b, h, 0, 0, 0)),
                  pl.BlockSpec((1, tq, MLA_V), lambda b, h, i: (b, i, h))],
        out_specs=pl.BlockSpec((1, tq, MLA_V), lambda b, h, i: (b, i, h)),
        compiler_params=_params(3),
        name="mla_attn",
    )(q3, k3, vT, sg3)
    return og.reshape(T, W)


def _swa_proj_kernel(x_ref, scale_ref, shift_ref, g_ref, cos_ref, sin_ref,
                     wq_ref, wk_ref, wvT_ref, wgate_ref,
                     q_ref, k_ref, vT_ref, sg_ref):
    h = _modulated_norm(x_ref, g_ref, scale_ref, shift_ref)
    cos, sin = cos_ref[0], sin_ref[0]
    q = _rope_wide(_dot(h, wq_ref[...]) * (SWA_HEAD_DIM ** -0.5), cos, sin)
    _store_heads(q_ref, q, SWA_HEADS, SWA_HEAD_DIM, BF16)
    k = _rope_wide(_dot(h, wk_ref[...]), cos, sin)
    _store_heads(k_ref, k, SWA_KV_HEADS, SWA_HEAD_DIM, BF16)
    _store_slabs(vT_ref, _dot_nt(wvT_ref[...], h), SWA_KV_HEADS, SWA_HEAD_DIM)
    sg_ref[...] = _silu(_dot(h, wgate_ref[...]))


def _swa_attn_kernel(q_ref, k_ref, vT_ref, sink_ref, sg_ref, o_ref):
    qi = pl.program_id(2)
    tq = SWA_TQ
    G = SWA_HEADS // SWA_KV_HEADS
    n = G * tq
    span = SWA_WINDOW + tq
    q = q_ref[0].reshape(n, SWA_HEAD_DIM)
    blk0 = jnp.maximum(qi - SWA_WINDOW // tq, 0)
    base = pl.multiple_of(blk0 * LANES, LANES)
    k = k_ref[0, 0, pl.ds(base, span), :]
    s = _dot_nt(k, q)
    kpos = base + lax.broadcasted_iota(jnp.int32, s.shape, 0)
    qpos = qi * tq + (lax.broadcasted_iota(jnp.int32, s.shape, 1) & (tq - 1))
    diff = qpos - kpos
    s = jnp.where((diff >= 0) & (diff < SWA_WINDOW), s, NEG_INF)
    sink = sink_ref[0]
    m = jnp.maximum(jnp.max(s, axis=0, keepdims=True), sink)
    e = jnp.exp(s - m)
    denom = jnp.sum(e, axis=0, keepdims=True) + jnp.exp(sink - m)
    p = (e / denom).astype(BF16)
    vt = jnp.concatenate([vT_ref[0, 0, blk0 + c] for c in range(span // LANES)], axis=1)
    oT = _dot(vt, p)
    o = jnp.concatenate([oT[:, g * tq:(g + 1) * tq] for g in range(G)], axis=0).T
    o_ref[0] = (o * sg_ref[0]).astype(BF16)


def _swa_layer(x2, B, S, norm_g, scale, shift, cos, sin, w_in, sinks):
    T, D = x2.shape
    H, KV, d = SWA_HEADS, SWA_KV_HEADS, SWA_HEAD_DIM
    G = H // KV
    W = H * d
    KW = KV * d
    w_q, w_k, w_v, w_gate = (w_in[:, :W], w_in[:, W:W + KW], w_in[:, W + KW:W + 2 * KW],
                             w_in[:, W + 2 * KW:])
    tm = PROJ_ROWS
    tpb = S // tm
    full = lambda shape: pl.BlockSpec(shape, lambda i: (0,) * len(shape))
    rows = lambda w: pl.BlockSpec((tm, w), lambda i: (i, 0))
    per_b = pl.BlockSpec((1, 1, D), lambda i: (i // tpb, 0, 0))
    tab = pl.BlockSpec((1, tm, LANES), lambda i: (i // tpb, i % tpb, 0))
    heads = lambda nh: pl.BlockSpec((1, nh, tm, d), lambda i: (i // tpb, 0, i % tpb, 0))
    q, k, vT, sg = pl.pallas_call(
        _swa_proj_kernel,
        out_shape=(jax.ShapeDtypeStruct((B, H, S, d), BF16),
                   jax.ShapeDtypeStruct((B, KV, S, d), BF16),
                   jax.ShapeDtypeStruct((B, KV, S // LANES, d, LANES), BF16),
                   jax.ShapeDtypeStruct((T, W), F32)),
        grid=(T // tm,),
        in_specs=[rows(D), per_b, per_b, full((1, D)), tab, tab,
                  full((D, W)), full((D, KW)), full((KW, D)), full((D, W))],
        out_specs=(heads(H), heads(KV),
                   pl.BlockSpec((1, KV, tm // LANES, d, LANES), lambda i: (i // tpb, 0, i % tpb, 0, 0)),
                   rows(W)),
        compiler_params=_params(1),
        name="swa_proj",
    )(x2, scale[:, None, :], shift[:, None, :], norm_g[None, :], cos, sin,
      w_q.astype(BF16), w_k.astype(BF16), w_v.T.astype(BF16), w_gate.astype(BF16))

    tq = SWA_TQ
    n = G * tq
    sink_rows = jnp.repeat(sinks.astype(F32).reshape(KV, G), tq, axis=1)[:, None, :]
    og = pl.pallas_call(
        _swa_attn_kernel,
        out_shape=jax.ShapeDtypeStruct((B, S, W), BF16),
        grid=(B, KV, S // tq),
        in_specs=[pl.BlockSpec((1, G, tq, d), lambda b, kv, i: (b, kv, i, 0)),
                  pl.BlockSpec((1, 1, S, d), lambda b, kv, i: (b, kv, 0, 0)),
                  pl.BlockSpec((1, 1, S // LANES, d, LANES), lambda b, kv, i: (b, kv, 0, 0, 0)),
                  pl.BlockSpec((1, 1, n), lambda b, kv, i: (kv, 0, 0)),
                  pl.BlockSpec((1, tq, G * d), lambda b, kv, i: (b, i, kv))],
        out_specs=pl.BlockSpec((1, tq, G * d), lambda b, kv, i: (b, i, kv)),
        compiler_params=_params(3),
        name="swa_attn",
    )(q, k, vT, sink_rows, sg.reshape(B, S, W))
    return og.reshape(T, W)


def _nsa_proj_kernel(x_ref, scale_ref, shift_ref, g_ref, cos_ref, sin_ref,
                     wq_ref, wkc_ref, wvc_ref, wks_ref, wvsT_ref, wkw_ref, wvwT_ref, wgT_ref, wgate_ref,
                     q_ref, kc_ref, vc_ref, ks_ref, vsT_ref, kw_ref, vwT_ref, gT_ref, sg_ref):
    h = _modulated_norm(x_ref, g_ref, scale_ref, shift_ref)
    cos, sin = cos_ref[0], sin_ref[0]
    KV, d = NSA_KV_HEADS, NSA_HEAD_DIM
    q = _rope_wide(_dot(h, wq_ref[...]) * (d ** -0.5), cos, sin)
    _store_heads(q_ref, q, NSA_HEADS, d, BF16)
    _store_heads(kc_ref, _dot(h, wkc_ref[...]), KV, d, F32)
    _store_heads(vc_ref, _dot(h, wvc_ref[...]), KV, d, F32)
    _store_heads(ks_ref, _rope_wide(_dot(h, wks_ref[...]), cos, sin), KV, d, BF16)
    _store_heads(kw_ref, _rope_wide(_dot(h, wkw_ref[...]), cos, sin), KV, d, BF16)
    _store_slabs(vsT_ref, _dot_nt(wvsT_ref[...], h), KV, d)
    _store_slabs(vwT_ref, _dot_nt(wvwT_ref[...], h), KV, d)
    gT = _sigmoid(_dot_nt(wgT_ref[...], h))
    for kv in range(KV):
        gT_ref[0, kv] = gT[kv * 16:(kv + 1) * 16, :]
    sg_ref[...] = _silu(_dot(h, wgate_ref[...]))


def _nsa_compress_kernel(kc_ref, vc_ref, pe_ref, wk1_ref, wk2_ref, wv1_ref, wv2T_ref, kout_ref, vT_ref):
    pe_top, pe_bot = pe_ref[0], pe_ref[1]

    def hidden(chunks, w1_ref):
        top = _dot((chunks + pe_top).astype(BF16), w1_ref[0])
        bot = _dot((chunks + pe_bot).astype(BF16), w1_ref[1])
        n = bot.shape[0]
        return _silu(top + pltpu.roll(bot, n - 1, axis=0)).astype(BF16)

    hk = hidden(kc_ref[0, 0], wk1_ref)
    kc = _dot(hk, wk2_ref[...])
    row = lax.broadcasted_iota(jnp.int32, kc.shape, 0)
    kout_ref[0, 0] = jnp.where(row < kc.shape[0] - 1, kc, 0.0).astype(BF16)
    hv = hidden(vc_ref[0, 0], wv1_ref)
    vT = _dot_nt(wv2T_ref[...], hv)
    col = lax.broadcasted_iota(jnp.int32, vT.shape, 1)
    vT_ref[0, 0] = jnp.where(col < vT.shape[1] - 1, vT, 0.0).astype(BF16)


def _nsa_attn_kernel(q_ref, kc_ref, vcT_ref, ovT_ref, ks_ref, vsT_ref, kw_ref, vwT_ref, gT_ref, sg_ref,
                     o_ref, sel_ref, *, n_top):
    qi = pl.program_id(2)
    tq = NSA_TQ
    G = NSA_HEADS // NSA_KV_HEADS
    d = NSA_HEAD_DIM
    n = G * tq
    q = q_ref[0].reshape(n, d)
    q0 = qi * tq

    def col_qpos(shape):
        return q0 + (lax.broadcasted_iota(jnp.int32, shape, 1) & (tq - 1))

    s = _dot_nt(kc_ref[0, 0], q)
    cmp_end = lax.broadcasted_iota(jnp.int32, s.shape, 0) * NSA_CMP_STRIDE + (NSA_CMP_BLOCK - 1)
    valid = cmp_end <= col_qpos(s.shape)
    s = jnp.where(valid, s, NEG_INF)
    m = jnp.max(s, axis=0, keepdims=True)
    e = jnp.where(valid, jnp.exp(s - m), 0.0)
    l = jnp.sum(e, axis=0, keepdims=True)
    p = (e * jnp.where(l > 0.0, 1.0 / l, 0.0)).astype(BF16)
    o_c = _dot(vcT_ref[0, 0], p)

    imp_all = _dot(ovT_ref[...], p)
    imp = imp_all[:, 0:tq]
    for g in range(1, G):
        imp = imp + imp_all[:, g * tq:(g + 1) * tq]
    ns = imp.shape[0]
    blk = lax.broadcasted_iota(jnp.int32, imp.shape, 0)
    sel_shift = NSA_SEL_BLOCK.bit_length() - 1
    q_blk = lax.shift_right_logical(q0 + lax.broadcasted_iota(jnp.int32, imp.shape, 1), sel_shift)
    causal = blk <= q_blk
    forced = (blk == 0) | (blk == q_blk) | (blk == q_blk - 1)
    val = jnp.where(causal, imp + jnp.where(forced, NSA_FORCE_BONUS, 0.0), -1.0)
    rank = jnp.zeros(imp.shape, F32)
    for i in range(ns):
        row = val[i:i + 1, :]
        rank = rank + jnp.where(blk > i, jnp.where(row >= val, 1.0, 0.0), jnp.where(row > val, 1.0, 0.0))
    sel = jnp.where(causal & (rank < n_top), 1.0, 0.0)
    per_step = NSA_SEL_TK // NSA_SEL_BLOCK
    for c in range(ns // per_step):
        sel_ref[c] = sel[c * per_step:(c + 1) * per_step, :]

    tk = NSA_SEL_TK
    slabs = tk // LANES

    def sel_step(kb, carry):
        m, l, acc = carry
        k = ks_ref[0, 0, pl.ds(pl.multiple_of(kb * tk, tk), tk), :]
        s = _dot_nt(k, q)
        chosen = sel_ref[kb]
        keep = jnp.concatenate(
            [jnp.broadcast_to(chosen[c:c + 1, :], (NSA_SEL_BLOCK, tq)) for c in range(per_step)], axis=0)
        keep = jnp.concatenate([keep] * G, axis=1)
        kpos = kb * tk + lax.broadcasted_iota(jnp.int32, s.shape, 0)
        s = jnp.where((keep > 0.5) & (kpos <= col_qpos(s.shape)), s, NEG_INF)
        m_new = jnp.maximum(m, jnp.max(s, axis=0, keepdims=True))
        alpha = jnp.exp(m - m_new)
        p = jnp.exp(s - m_new)
        l = alpha * l + jnp.sum(p, axis=0, keepdims=True)
        vt = jnp.concatenate([vsT_ref[0, 0, kb * slabs + c] for c in range(slabs)], axis=1)
        acc = alpha * acc + _dot(vt, p.astype(BF16))
        return m_new, l, acc

    init = (jnp.full((1, n), NEG_INF, F32), jnp.zeros((1, n), F32), jnp.zeros((d, n), F32))
    n_steps = lax.shift_right_logical(q0 + tq - 1, tk.bit_length() - 1) + 1
    _, l_s, acc_s = lax.fori_loop(0, n_steps, sel_step, init)
    o_s = acc_s / l_s

    span = NSA_WINDOW + tq
    blk0 = jnp.maximum(qi - NSA_WINDOW // tq, 0)
    base = pl.multiple_of(blk0 * LANES, LANES)
    s = _dot_nt(kw_ref[0, 0, pl.ds(base, span), :], q)
    diff = col_qpos(s.shape) - (base + lax.broadcasted_iota(jnp.int32, s.shape, 0))
    s = jnp.where((diff >= 0) & (diff < NSA_WINDOW), s, NEG_INF)
    e = jnp.exp(s - jnp.max(s, axis=0, keepdims=True))
    p = (e / jnp.sum(e, axis=0, keepdims=True)).astype(BF16)
    vt = jnp.concatenate([vwT_ref[0, 0, blk0 + c] for c in range(span // LANES)], axis=1)
    o_w = _dot(vt, p)

    gates = gT_ref[0, 0]
    outs = []
    for g in range(G):
        cols = slice(g * tq, (g + 1) * tq)
        outs.append(gates[3 * g:3 * g + 1, :] * o_c[:, cols]
                    + gates[3 * g + 1:3 * g + 2, :] * o_s[:, cols]
                    + gates[3 * g + 2:3 * g + 3, :] * o_w[:, cols])
    o = jnp.concatenate(outs, axis=0).T
    o_ref[0] = (o * sg_ref[0]).astype(BF16)


def _nsa_overlap_T(nc_pad, ns):
    nc = nc_pad - (NSA_CMP_BLOCK // NSA_CMP_STRIDE - 1)
    cs = np.arange(nc_pad)[None, :] * NSA_CMP_STRIDE
    ss = np.arange(ns)[:, None] * NSA_SEL_BLOCK
    ov = np.clip(np.minimum(cs + NSA_CMP_BLOCK, ss + NSA_SEL_BLOCK) - np.maximum(cs, ss), 0, None)
    ov = np.where(np.arange(nc_pad)[None, :] < nc, ov, 0)
    return jnp.asarray(ov / NSA_CMP_BLOCK, dtype=BF16)


def _nsa_layer(x2, B, S, norm_g, scale, shift, cos, sin, w_in, cmp_pos, w_k1, w_k2, w_v1, w_v2):
    T, D = x2.shape
    H, KV, d = NSA_HEADS, NSA_KV_HEADS, NSA_HEAD_DIM
    G = H // KV
    W = H * d
    KW = KV * d
    offs = np.cumsum([0, W] + [KW] * 6 + [3 * H, W])
    w_q, w_kc, w_vc, w_ks, w_vs, w_kw, w_vw, w_g, w_gate = (
        w_in[:, offs[i]:offs[i + 1]] for i in range(9))
    w_gT = jnp.pad(w_g.T.reshape(KV, 3 * G, D), ((0, 0), (0, 16 - 3 * G), (0, 0))).reshape(KV * 16, D)

    tm = PROJ_ROWS
    tpb = S // tm
    full = lambda shape: pl.BlockSpec(shape, lambda i: (0,) * len(shape))
    rows = lambda w: pl.BlockSpec((tm, w), lambda i: (i, 0))
    per_b = pl.BlockSpec((1, 1, D), lambda i: (i // tpb, 0, 0))
    tab = pl.BlockSpec((1, tm, LANES), lambda i: (i // tpb, i % tpb, 0))
    heads = lambda nh: pl.BlockSpec((1, nh, tm, d), lambda i: (i // tpb, 0, i % tpb, 0))
    slab = pl.BlockSpec((1, KV, tm // LANES, d, LANES), lambda i: (i // tpb, 0, i % tpb, 0, 0))
    kv_f32 = jax.ShapeDtypeStruct((B, KV, S, d), F32)
    kv_bf16 = jax.ShapeDtypeStruct((B, KV, S, d), BF16)
    kv_slab = jax.ShapeDtypeStruct((B, KV, S // LANES, d, LANES), BF16)
    q, kc, vc, ks, vsT, kw, vwT, gT, sg = pl.pallas_call(
        _nsa_proj_kernel,
        out_shape=(jax.ShapeDtypeStruct((B, H, S, d), BF16), kv_f32, kv_f32,
                   kv_bf16, kv_slab, kv_bf16, kv_slab,
                   jax.ShapeDtypeStruct((B, KV, 16, S), F32),
                   jax.ShapeDtypeStruct((T, W), F32)),
        grid=(T // tm,),
        in_specs=[rows(D), per_b, per_b, full((1, D)), tab, tab,
                  full((D, W)), full((D, KW)), full((D, KW)), full((D, KW)), full((KW, D)),
                  full((D, KW)), full((KW, D)), full((KV * 16, D)), full((D, W))],
        out_specs=(heads(H), heads(KV), heads(KV), heads(KV), slab, heads(KV), slab,
                   pl.BlockSpec((1, KV, 16, tm), lambda i: (i // tpb, 0, 0, i % tpb)),
                   rows(W)),
        compiler_params=_params(1),
        name="nsa_proj",
    )(x2, scale[:, None, :], shift[:, None, :], norm_g[None, :], cos, sin,
      w_q.astype(BF16), w_kc.astype(BF16), w_vc.astype(BF16), w_ks.astype(BF16), w_vs.T.astype(BF16),
      w_kw.astype(BF16), w_vw.T.astype(BF16), w_gT.astype(BF16), w_gate.astype(BF16))

    n_chunks = S // NSA_CMP_STRIDE
    cw = NSA_CMP_STRIDE * d
    pe = cmp_pos.reshape(2, 1, cw)
    chunk_spec = pl.BlockSpec((1, 1, n_chunks, cw), lambda b, kv: (b, kv, 0, 0))
    full2 = lambda shape: pl.BlockSpec(shape, lambda b, kv: (0,) * len(shape))
    k_c, v_cT = pl.pallas_call(
        _nsa_compress_kernel,
        out_shape=(jax.ShapeDtypeStruct((B, KV, n_chunks, d), BF16),
                   jax.ShapeDtypeStruct((B, KV, d, n_chunks), BF16)),
        grid=(B, KV),
        in_specs=[chunk_spec, chunk_spec, full2((2, 1, cw)),
                  full2((2, cw, NSA_CMP_HIDDEN)), full2((NSA_CMP_HIDDEN, d)),
                  full2((2, cw, NSA_CMP_HIDDEN)), full2((d, NSA_CMP_HIDDEN))],
        out_specs=(pl.BlockSpec((1, 1, n_chunks, d), lambda b, kv: (b, kv, 0, 0)),
                   pl.BlockSpec((1, 1, d, n_chunks), lambda b, kv: (b, kv, 0, 0))),
        compiler_params=_params(2),
        name="nsa_compress",
    )(kc.reshape(B, KV, n_chunks, cw), vc.reshape(B, KV, n_chunks, cw), pe,
      w_k1.reshape(2, cw, NSA_CMP_HIDDEN).astype(BF16), w_k2.astype(BF16),
      w_v1.reshape(2, cw, NSA_CMP_HIDDEN).astype(BF16), w_v2.T.astype(BF16))

    tq = NSA_TQ
    ns = S // NSA_SEL_BLOCK
    per_step = NSA_SEL_TK // NSA_SEL_BLOCK
    ovT = _nsa_overlap_T(n_chunks, ns)
    whole = lambda shape: pl.BlockSpec((1, 1) + shape, lambda b, kv, i: (b, kv) + (0,) * len(shape))
    og = pl.pallas_call(
        functools.partial(_nsa_attn_kernel, n_top=min(NSA_N_SELECT, ns)),
        out_shape=jax.ShapeDtypeStruct((B, S, W), BF16),
        grid=(B, KV, S // tq),
        in_specs=[pl.BlockSpec((1, G, tq, d), lambda b, kv, i: (b, kv, i, 0)),
                  whole((n_chunks, d)), whole((d, n_chunks)),
                  pl.BlockSpec((ns, n_chunks), lambda b, kv, i: (0, 0)),
                  whole((S, d)), whole((S // LANES, d, LANES)),
                  whole((S, d)), whole((S // LANES, d, LANES)),
                  pl.BlockSpec((1, 1, 16, tq), lambda b, kv, i: (b, kv, 0, i)),
                  pl.BlockSpec((1, tq, G * d), lambda b, kv, i: (b, i, kv))],
        out_specs=pl.BlockSpec((1, tq, G * d), lambda b, kv, i: (b, i, kv)),
        scratch_shapes=[pltpu.VMEM((ns // per_step, per_step, tq), F32)],
        compiler_params=_params(3),
        name="nsa_attn",
    )(q, k_c, v_cT, ovT, ks, vsT, kw, vwT, gT, sg.reshape(B, S, W))
    return og.reshape(T, W)


def kernel(x, c, positions, norm_g, ada_w, ada_b, mla_w_in, mla_q_norm_g, mla_kv_norm_g, mla_w_q_b, mla_w_kv_b, mla_w_out, swa_w_in, swa_sinks, swa_w_out, nsa_w_in, nsa_cmp_pos, nsa_w_cmp_k1, nsa_w_cmp_k2, nsa_w_cmp_v1, nsa_w_cmp_v2, nsa_w_out, final_norm_g):
    B, S, D = x.shape
    depth = norm_g.shape[0]
    cos, sin = _rope_tables(positions)
    mod = _ada_modulation(c, ada_w, ada_b)
    x2 = x.reshape(B * S, D)
    for i in range(depth):
        shift, scale, gate = mod[i, :, :D], mod[i, :, D:2 * D], mod[i, :, 2 * D:]
        kind, j = i % 3, i // 3
        if kind == 0:
            og = _mla_layer(x2, B, S, norm_g[i], scale, shift, cos, sin, mla_w_in[j], mla_q_norm_g[j],
                            mla_kv_norm_g[j], mla_w_q_b[j], mla_w_kv_b[j])
            w_out = mla_w_out[j]
        elif kind == 1:
            og = _swa_layer(x2, B, S, norm_g[i], scale, shift, cos, sin, swa_w_in[j], swa_sinks[j])
            w_out = swa_w_out[j]
        else:
            og = _nsa_layer(x2, B, S, norm_g[i], scale, shift, cos, sin, nsa_w_in[j], nsa_cmp_pos[j],
                            nsa_w_cmp_k1[j], nsa_w_cmp_k2[j], nsa_w_cmp_v1[j], nsa_w_cmp_v2[j])
            w_out = nsa_w_out[j]
        x2 = _out_proj(og, w_out, x2, gate, final_norm_g, S, final=(i == depth - 1))
    return x2.reshape(B, S, D)
```

```python
import functools

import jax
import jax.numpy as jnp
import numpy as np
from jax import lax
from jax.experimental import pallas as pl
from jax.experimental.pallas import tpu as pltpu

F32 = jnp.float32
BF16 = jnp.bfloat16

ROPE_THETA = 10000.0
RMS_EPS = 1e-6
NEG_INF = -1e30
LOG2_E = 1.4426950408889634

MLA_HEADS = 8
MLA_NOPE = 128
MLA_ROPE = 64
MLA_V = 128
MLA_Q_LORA = 256
MLA_KV_LORA = 128

SWA_HEADS = 16
SWA_KV_HEADS = 2
SWA_HEAD_DIM = 64
SWA_WINDOW = 128

NSA_HEADS = 16
NSA_KV_HEADS = 4
NSA_HEAD_DIM = 64
NSA_CMP_BLOCK = 32
NSA_CMP_STRIDE = 16
NSA_CMP_HIDDEN = 128
NSA_SEL_BLOCK = 64
NSA_N_SELECT = 16
NSA_WINDOW = 512
NSA_FORCE_BONUS = 1e4

LANES = 128
ROPE_HALF = 32
VMEM_LIMIT = 56 * 1024 * 1024

PROJ_ROWS = 256
OUT_ROWS = 512
MLA_TQ = 512
MLA_HEADS_PER_STEP = 4
SWA_TQ = 128
NSA_TQ = 128
NSA_SEL_TK = 512


def _dot(a, b):
    return jnp.dot(a, b, preferred_element_type=F32)


def _dot_nt(a, b):
    return lax.dot_general(a, b, (((1,), (1,)), ((), ())), preferred_element_type=F32)


def _silu(x):
    return x / (1.0 + jnp.exp(-x))


def _sigmoid(x):
    return 1.0 / (1.0 + jnp.exp(-x))


def _rms(x, g):
    return x * lax.rsqrt(jnp.mean(x * x, axis=-1, keepdims=True) + RMS_EPS) * g


def _params(n_axes):
    return pltpu.CompilerParams(dimension_semantics=("arbitrary",) * n_axes,
                                vmem_limit_bytes=VMEM_LIMIT)


def _rope128(x, cos, sin_signed):
    lane = lax.broadcasted_iota(jnp.int32, x.shape, 1)
    lower_half = (lane & (2 * ROPE_HALF - 1)) < ROPE_HALF
    partner = jnp.where(lower_half, pltpu.roll(x, LANES - ROPE_HALF, axis=1),
                        pltpu.roll(x, ROPE_HALF, axis=1))
    return x * cos + partner * sin_signed


def _rope_table_kernel(pos_ref, freq_ref, sign_ref, cos_ref, sin_ref):
    ang = pos_ref[0] * freq_ref[...]
    cos_ref[0] = jnp.cos(ang)
    sin_ref[0] = jnp.sin(ang) * sign_ref[...]


def _rope_tables(positions):
    B, S = positions.shape
    inv_freq = ROPE_THETA ** (-jnp.arange(ROPE_HALF, dtype=F32) / ROPE_HALF)
    freq = jnp.tile(inv_freq, LANES // ROPE_HALF)[None, :]
    sign = jnp.tile(jnp.concatenate([-jnp.ones(ROPE_HALF, F32), jnp.ones(ROPE_HALF, F32)]),
                    LANES // (2 * ROPE_HALF))[None, :]
    pos = positions.astype(F32)[..., None]
    ts = min(S, 512)
    spec = pl.BlockSpec((1, ts, LANES), lambda b, s: (b, s, 0))
    return pl.pallas_call(
        _rope_table_kernel,
        out_shape=(jax.ShapeDtypeStruct((B, S, LANES), F32),) * 2,
        grid=(B, S // ts),
        in_specs=[pl.BlockSpec((1, ts, 1), lambda b, s: (b, s, 0)),
                  pl.BlockSpec((1, LANES), lambda b, s: (0, 0)),
                  pl.BlockSpec((1, LANES), lambda b, s: (0, 0))],
        out_specs=(spec, spec),
        compiler_params=_params(2),
        name="rope_tables",
    )(pos, freq, sign)


def _ada_kernel(c_ref, w_ref, b_ref, o_ref):
    cond = _silu(c_ref[...]).astype(BF16)
    o_ref[0] = _dot(cond, w_ref[0]) + b_ref[0]


def _ada_modulation(c, ada_w, ada_b):
    depth, D, D3 = ada_w.shape
    B = c.shape[0]
    Bp = -(-B // 16) * 16
    c_pad = jnp.pad(c, ((0, Bp - B), (0, 0)))
    tn = 1024
    out = pl.pallas_call(
        _ada_kernel,
        out_shape=jax.ShapeDtypeStruct((depth, Bp, D3), F32),
        grid=(depth, D3 // tn),
        in_specs=[pl.BlockSpec((Bp, D), lambda i, n: (0, 0)),
                  pl.BlockSpec((1, D, tn), lambda i, n: (i, 0, n)),
                  pl.BlockSpec((1, 1, tn), lambda i, n: (i, 0, n))],
        out_specs=pl.BlockSpec((1, Bp, tn), lambda i, n: (i, 0, n)),
        compiler_params=_params(2),
        name="ada_modulation",
    )(c_pad, ada_w.astype(BF16), ada_b[:, None, :])
    return out[:, :B]


def _out_proj_kernel(og_ref, w_ref, x_ref, gate_ref, fg_ref, o_ref, *, final):
    y = _dot(og_ref[...], w_ref[...])
    xn = x_ref[...] + gate_ref[0] * y
    if final:
        xn = _rms(xn, fg_ref[...])
    o_ref[...] = xn


def _out_proj(og, w_out, x2, gate_c, final_g, S, final):
    T, D = x2.shape
    W = og.shape[1]
    tm = OUT_ROWS
    tpb = S // tm
    return pl.pallas_call(
        functools.partial(_out_proj_kernel, final=final),
        out_shape=jax.ShapeDtypeStruct((T, D), F32),
        grid=(T // tm,),
        in_specs=[pl.BlockSpec((tm, W), lambda i: (i, 0)),
                  pl.BlockSpec((W, D), lambda i: (0, 0)),
                  pl.BlockSpec((tm, D), lambda i: (i, 0)),
                  pl.BlockSpec((1, 1, D), lambda i: (i // tpb, 0, 0)),
                  pl.BlockSpec((1, D), lambda i: (0, 0))],
        out_specs=pl.BlockSpec((tm, D), lambda i: (i, 0)),
        compiler_params=_params(1),
        name="out_proj",
    )(og, w_out.astype(BF16), x2, gate_c[:, None, :], final_g[None, :])


def _modulated_norm(x_ref, g_ref, scale_ref, shift_ref):
    y = _rms(x_ref[...], g_ref[...])
    return (y * (1.0 + scale_ref[0]) + shift_ref[0]).astype(BF16)


def _store_heads(dst_ref, val, n_heads, width, dtype):
    for h in range(n_heads):
        dst_ref[0, h] = val[:, h * width:(h + 1) * width].astype(dtype)


def _store_heads_padded(dst_ref, val, n_heads, upper=None):
    half = LANES // 2
    lane = lax.broadcasted_iota(jnp.int32, (val.shape[0], LANES), 1)
    fill = 0.0 if upper is None else upper
    for h in range(n_heads):
        chunk = val[:, (h // 2) * LANES:(h // 2 + 1) * LANES]
        if h % 2:
            chunk = pltpu.roll(chunk, half, axis=1)
        dst_ref[0, h] = jnp.where(lane < half, chunk, fill).astype(BF16)


def _store_slabs(dst_ref, val, n_heads, d):
    for h in range(n_heads):
        for c in range(val.shape[1] // LANES):
            dst_ref[0, h, c] = val[h * d:(h + 1) * d, c * LANES:(c + 1) * LANES].astype(BF16)


def _rope_wide(val, cos, sin_signed):
    return jnp.concatenate(
        [_rope128(val[:, c * LANES:(c + 1) * LANES], cos, sin_signed)
         for c in range(val.shape[1] // LANES)], axis=1)


def _mla_proj_kernel(x_ref, scale_ref, shift_ref, g_ref, cos_ref, sin_ref,
                     wqa_ref, wkva_ref, wkpe_ref, wgate_ref, qg_ref, kvg_ref,
                     wqb_ref, wkb_ref, wvbT_ref,
                     q_ref, k_ref, vT_ref, sg_ref):
    h = _modulated_norm(x_ref, g_ref, scale_ref, shift_ref)
    cos, sin = cos_ref[0], sin_ref[0]
    head_w = MLA_NOPE + LANES
    q_scale = (MLA_NOPE + MLA_ROPE) ** -0.5 * LOG2_E

    qn = _rms(_dot(h, wqa_ref[...]), qg_ref[...]).astype(BF16)
    q = _dot(qn, wqb_ref[...]) * q_scale
    for hd in range(MLA_HEADS):
        lo = hd * head_w
        q_ref[:, lo:lo + MLA_NOPE] = q[:, lo:lo + MLA_NOPE].astype(BF16)
        q_ref[:, lo + MLA_NOPE:lo + head_w] = _rope128(
            q[:, lo + MLA_NOPE:lo + head_w], cos, sin).astype(BF16)

    kvn = _rms(_dot(h, wkva_ref[...]), kvg_ref[...]).astype(BF16)
    kn = _dot(kvn, wkb_ref[...])
    kpe = _rope128(_dot(h, wkpe_ref[...]), cos, sin).astype(BF16)
    for hd in range(MLA_HEADS):
        lo = hd * head_w
        k_ref[:, lo:lo + MLA_NOPE] = kn[:, hd * MLA_NOPE:(hd + 1) * MLA_NOPE].astype(BF16)
        k_ref[:, lo + MLA_NOPE:lo + head_w] = kpe
    _store_slabs(vT_ref, _dot_nt(wvbT_ref[...], kvn), MLA_HEADS, MLA_V)

    sg_ref[...] = _silu(_dot(h, wgate_ref[...]))


def _mla_attn_kernel(q_ref, k_ref, vT_ref, sg_ref, o_ref):
    qi = pl.program_id(2)
    tq = tk = MLA_TQ
    slabs = tk // LANES
    head_w = MLA_NOPE + LANES
    heads = range(MLA_HEADS_PER_STEP)

    def scores(h, kb):
        cols = slice(h * head_w, (h + 1) * head_w)
        k = k_ref[0, pl.ds(pl.multiple_of(kb * tk, tk), tk), cols]
        return _dot_nt(k, q_ref[0, :, cols])

    def absorb(h, kb, s, carry):
        m, l, acc = carry
        m_new = jnp.maximum(m, jnp.max(s, axis=0, keepdims=True))
        alpha = jnp.exp2(m - m_new)
        p = jnp.exp2(s - m_new)
        l = alpha * l + jnp.sum(p, axis=0, keepdims=True)
        vt = jnp.concatenate([vT_ref[0, h, kb * slabs + c] for c in range(slabs)], axis=1)
        acc = alpha * acc + _dot(vt, p.astype(BF16))
        return m_new, l, acc

    def step(kb, carries, masked):
        s = [scores(h, kb) for h in heads]
        if masked:
            kpos = lax.broadcasted_iota(jnp.int32, (tk, tq), 0)
            qpos = lax.broadcasted_iota(jnp.int32, (tk, tq), 1)
            s = [jnp.where(kpos <= qpos, sh, NEG_INF) for sh in s]
        return tuple(absorb(h, kb, s[h], carries[h]) for h in heads)

    init = (jnp.full((1, tq), NEG_INF, F32), jnp.zeros((1, tq), F32), jnp.zeros((MLA_V, tq), F32))
    carries = lax.fori_loop(0, qi, lambda kb, c: step(kb, c, False), (init,) * len(heads))
    carries = step(qi, carries, True)
    for h, (_, l, acc) in enumerate(carries):
        cols = slice(h * MLA_V, (h + 1) * MLA_V)
        o = (acc * (1.0 / l)).T
        o_ref[0, :, cols] = (o * sg_ref[0, :, cols]).astype(BF16)


def _mla_layer(x2, B, S, norm_g, scale, shift, cos, sin, w_in, q_norm_g, kv_norm_g, w_q_b, w_kv_b):
    T, D = x2.shape
    H = MLA_HEADS
    head_w = MLA_NOPE + LANES
    o1 = MLA_Q_LORA
    o2 = o1 + MLA_KV_LORA
    o3 = o2 + MLA_ROPE
    w_qa, w_kva, w_kpe, w_gate = (w_in[:, :o1], w_in[:, o1:o2], w_in[:, o2:o3], w_in[:, o3:])
    w_kpe = jnp.pad(w_kpe, ((0, 0), (0, LANES - MLA_ROPE)))
    wq = w_q_b.reshape(MLA_Q_LORA, H, MLA_NOPE + MLA_ROPE)
    wq = jnp.pad(wq, ((0, 0), (0, 0), (0, head_w - MLA_NOPE - MLA_ROPE))).reshape(MLA_Q_LORA, H * head_w)
    wkv = w_kv_b.reshape(MLA_KV_LORA, H, MLA_NOPE + MLA_V)
    w_kb = wkv[:, :, :MLA_NOPE].reshape(MLA_KV_LORA, H * MLA_NOPE)
    w_vbT = wkv[:, :, MLA_NOPE:].reshape(MLA_KV_LORA, H * MLA_V).T
    W = H * MLA_V

    tm = PROJ_ROWS
    tpb = S // tm
    full = lambda shape: pl.BlockSpec(shape, lambda i: (0,) * len(shape))
    rows = lambda w: pl.BlockSpec((tm, w), lambda i: (i, 0))
    per_b = pl.BlockSpec((1, 1, D), lambda i: (i // tpb, 0, 0))
    tab = pl.BlockSpec((1, tm, LANES), lambda i: (i // tpb, i % tpb, 0))
    q, k, vT, sg = pl.pallas_call(
        _mla_proj_kernel,
        out_shape=(jax.ShapeDtypeStruct((T, H * head_w), BF16),
                   jax.ShapeDtypeStruct((T, H * head_w), BF16),
                   jax.ShapeDtypeStruct((B, H, S // LANES, MLA_V, LANES), BF16),
                   jax.ShapeDtypeStruct((T, W), F32)),
        grid=(T // tm,),
        in_specs=[rows(D), per_b, per_b, full((1, D)), tab, tab,
                  full((D, o1)), full((D, MLA_KV_LORA)), full((D, LANES)), full((D, W)),
                  full((1, o1)), full((1, MLA_KV_LORA)),
                  full((o1, H * head_w)), full((MLA_KV_LORA, H * MLA_NOPE)), full((W, MLA_KV_LORA))],
        out_specs=(rows(H * head_w), rows(H * head_w),
                   pl.BlockSpec((1, H, tm // LANES, MLA_V, LANES),
                                lambda i: (i // tpb, 0, i % tpb, 0, 0)),
                   rows(W)),
        compiler_params=_params(1),
        name="mla_proj",
    )(x2, scale[:, None, :], shift[:, None, :], norm_g[None, :], cos, sin,
      w_qa.astype(BF16), w_kva.astype(BF16), w_kpe.astype(BF16), w_gate.astype(BF16),
      q_norm_g[None, :], kv_norm_g[None, :],
      wq.astype(BF16), w_kb.astype(BF16), w_vbT.astype(BF16))

    tq = MLA_TQ
    hp = MLA_HEADS_PER_STEP
    q3 = q.reshape(B, S, H * head_w)
    k3 = k.reshape(B, S, H * head_w)
    sg3 = sg.reshape(B, S, W)
    og = pl.pallas_call(
        _mla_attn_kernel,
        out_shape=jax.ShapeDtypeStruct((B, S, W), BF16),
        grid=(B, H // hp, S // tq),
        in_specs=[pl.BlockSpec((1, tq, hp * head_w), lambda b, h, i: (b, i, h)),
                  pl.BlockSpec((1, S, hp * head_w), lambda b, h, i: (b, 0, h)),
                  pl.BlockSpec((1, hp, S // LANES, MLA_V, LANES), lambda b, h, i: (b, h, 0, 0, 0)),
                  pl.BlockSpec((1, tq, hp * MLA_V), lambda b, h, i: (b, i, h))],
        out_specs=pl.BlockSpec((1, tq, hp * MLA_V), lambda b, h, i: (b, i, h)),
        compiler_params=_params(3),
        name="mla_attn",
    )(q3, k3, vT, sg3)
    return og.reshape(T, W)


def _swa_proj_kernel(x_ref, scale_ref, shift_ref, g_ref, cos_ref, sin_ref,
                     wq_ref, wk_ref, wvT_ref, wgate_ref,
                     q_ref, k_ref, vT_ref, sg_ref):
    h = _modulated_norm(x_ref, g_ref, scale_ref, shift_ref)
    cos, sin = cos_ref[0], sin_ref[0]
    q = _rope_wide(_dot(h, wq_ref[...]) * (SWA_HEAD_DIM ** -0.5), cos, sin)
    _store_heads(q_ref, q, SWA_HEADS, SWA_HEAD_DIM, BF16)
    k = _rope_wide(_dot(h, wk_ref[...]), cos, sin)
    _store_heads(k_ref, k, SWA_KV_HEADS, SWA_HEAD_DIM, BF16)
    _store_slabs(vT_ref, _dot_nt(wvT_ref[...], h), SWA_KV_HEADS, SWA_HEAD_DIM)
    sg_ref[...] = _silu(_dot(h, wgate_ref[...]))


def _swa_attn_kernel(q_ref, k_ref, vT_ref, sink_ref, sg_ref, o_ref):
    qi = pl.program_id(2)
    tq = SWA_TQ
    G = SWA_HEADS // SWA_KV_HEADS
    n = G * tq
    span = SWA_WINDOW + tq
    q = q_ref[0].reshape(n, SWA_HEAD_DIM)
    blk0 = jnp.maximum(qi - SWA_WINDOW // tq, 0)
    base = pl.multiple_of(blk0 * LANES, LANES)
    k = k_ref[0, 0, pl.ds(base, span), :]
    s = _dot_nt(k, q)
    kpos = base + lax.broadcasted_iota(jnp.int32, s.shape, 0)
    qpos = qi * tq + (lax.broadcasted_iota(jnp.int32, s.shape, 1) & (tq - 1))
    diff = qpos - kpos
    s = jnp.where((diff >= 0) & (diff < SWA_WINDOW), s, NEG_INF)
    sink = sink_ref[0]
    m = jnp.maximum(jnp.max(s, axis=0, keepdims=True), sink)
    e = jnp.exp(s - m)
    denom = jnp.sum(e, axis=0, keepdims=True) + jnp.exp(sink - m)
    p = (e / denom).astype(BF16)
    vt = jnp.concatenate([vT_ref[0, 0, blk0 + c] for c in range(span // LANES)], axis=1)
    oT = _dot(vt, p)
    o = jnp.concatenate([oT[:, g * tq:(g + 1) * tq] for g in range(G)], axis=0).T
    o_ref[0] = (o * sg_ref[0]).astype(BF16)


def _swa_layer(x2, B, S, norm_g, scale, shift, cos, sin, w_in, sinks):
    T, D = x2.shape
    H, KV, d = SWA_HEADS, SWA_KV_HEADS, SWA_HEAD_DIM
    G = H // KV
    W = H * d
    KW = KV * d
    w_q, w_k, w_v, w_gate = (w_in[:, :W], w_in[:, W:W + KW], w_in[:, W + KW:W + 2 * KW],
                             w_in[:, W + 2 * KW:])
    tm = PROJ_ROWS
    tpb = S // tm
    full = lambda shape: pl.BlockSpec(shape, lambda i: (0,) * len(shape))
    rows = lambda w: pl.BlockSpec((tm, w), lambda i: (i, 0))
    per_b = pl.BlockSpec((1, 1, D), lambda i: (i // tpb, 0, 0))
    tab = pl.BlockSpec((1, tm, LANES), lambda i: (i // tpb, i % tpb, 0))
    heads = lambda nh: pl.BlockSpec((1, nh, tm, d), lambda i: (i // tpb, 0, i % tpb, 0))
    q, k, vT, sg = pl.pallas_call(
        _swa_proj_kernel,
        out_shape=(jax.ShapeDtypeStruct((B, H, S, d), BF16),
                   jax.ShapeDtypeStruct((B, KV, S, d), BF16),
                   jax.ShapeDtypeStruct((B, KV, S // LANES, d, LANES), BF16),
                   jax.ShapeDtypeStruct((T, W), F32)),
        grid=(T // tm,),
        in_specs=[rows(D), per_b, per_b, full((1, D)), tab, tab,
                  full((D, W)), full((D, KW)), full((KW, D)), full((D, W))],
        out_specs=(heads(H), heads(KV),
                   pl.BlockSpec((1, KV, tm // LANES, d, LANES), lambda i: (i // tpb, 0, i % tpb, 0, 0)),
                   rows(W)),
        compiler_params=_params(1),
        name="swa_proj",
    )(x2, scale[:, None, :], shift[:, None, :], norm_g[None, :], cos, sin,
      w_q.astype(BF16), w_k.astype(BF16), w_v.T.astype(BF16), w_gate.astype(BF16))

    tq = SWA_TQ
    n = G * tq
    sink_rows = jnp.repeat(sinks.astype(F32).reshape(KV, G), tq, axis=1)[:, None, :]
    og = pl.pallas_call(
        _swa_attn_kernel,
        out_shape=jax.ShapeDtypeStruct((B, S, W), BF16),
        grid=(B, KV, S // tq),
        in_specs=[pl.BlockSpec((1, G, tq, d), lambda b, kv, i: (b, kv, i, 0)),
                  pl.BlockSpec((1, 1, S, d), lambda b, kv, i: (b, kv, 0, 0)),
                  pl.BlockSpec((1, 1, S // LANES, d, LANES), lambda b, kv, i: (b, kv, 0, 0, 0)),
                  pl.BlockSpec((1, 1, n), lambda b, kv, i: (kv, 0, 0)),
                  pl.BlockSpec((1, tq, G * d), lambda b, kv, i: (b, i, kv))],
        out_specs=pl.BlockSpec((1, tq, G * d), lambda b, kv, i: (b, i, kv)),
        compiler_params=_params(3),
        name="swa_attn",
    )(q, k, vT, sink_rows, sg.reshape(B, S, W))
    return og.reshape(T, W)


def _nsa_proj_kernel(x_ref, scale_ref, shift_ref, g_ref, cos_ref, sin_ref,
                     wq_ref, wkc_ref, wvc_ref, wks_ref, wvsT_ref, wkw_ref, wvwT_ref, wgT_ref, wgate_ref,
                     q_ref, kc_ref, vc_ref, ks_ref, vsT_ref, kw_ref, vwT_ref, gT_ref, sg_ref, *, tiles_per_seq):
    h = _modulated_norm(x_ref, g_ref, scale_ref, shift_ref)
    cos, sin = cos_ref[0], sin_ref[0]
    KV, d = NSA_KV_HEADS, NSA_HEAD_DIM
    tm = h.shape[0]
    q = _rope_wide(_dot(h, wq_ref[...]) * (d ** -0.5 * LOG2_E), cos, sin)
    _store_heads_padded(q_ref, q, NSA_HEADS)
    _store_heads(kc_ref, _dot(h, wkc_ref[...]), KV, d, F32)
    _store_heads(vc_ref, _dot(h, wvc_ref[...]), KV, d, F32)
    pos = (pl.program_id(0) % tiles_per_seq) * tm + lax.broadcasted_iota(jnp.int32, (tm, LANES), 0)
    lane = lax.broadcasted_iota(jnp.int32, (tm, LANES), 1)
    sel_shift = NSA_SEL_BLOCK.bit_length() - 1
    block_onehot = jnp.where(lane - LANES // 2 == lax.shift_right_logical(pos, sel_shift), 1.0, 0.0)
    _store_heads_padded(ks_ref, _rope_wide(_dot(h, wks_ref[...]), cos, sin), KV, upper=block_onehot)
    _store_heads_padded(kw_ref, _rope_wide(_dot(h, wkw_ref[...]), cos, sin), KV)
    _store_slabs(vsT_ref, _dot_nt(wvsT_ref[...], h), KV, d)
    _store_slabs(vwT_ref, _dot_nt(wvwT_ref[...], h), KV, d)
    gT = _sigmoid(_dot_nt(wgT_ref[...], h))
    for kv in range(KV):
        gT_ref[0, kv] = gT[kv * 16:(kv + 1) * 16, :]
    sg_ref[...] = _silu(_dot(h, wgate_ref[...]))


def _nsa_compress_kernel(kc_ref, vc_ref, pe_ref, wk1_ref, wk2_ref, wv1_ref, wv2T_ref, kout_ref, vT_ref):
    pe_top, pe_bot = pe_ref[0], pe_ref[1]

    def hidden(chunks, w1_ref):
        top = _dot((chunks + pe_top).astype(BF16), w1_ref[0])
        bot = _dot((chunks + pe_bot).astype(BF16), w1_ref[1])
        n = bot.shape[0]
        return _silu(top + pltpu.roll(bot, n - 1, axis=0)).astype(BF16)

    hk = hidden(kc_ref[0, 0], wk1_ref)
    kc = _dot(hk, wk2_ref[...])
    row = lax.broadcasted_iota(jnp.int32, kc.shape, 0)
    kout_ref[0, 0] = jnp.where(row < kc.shape[0] - 1, kc, 0.0).astype(BF16)
    hv = hidden(vc_ref[0, 0], wv1_ref)
    vT = _dot_nt(wv2T_ref[...], hv)
    col = lax.broadcasted_iota(jnp.int32, vT.shape, 1)
    vT_ref[0, 0] = jnp.where(col < vT.shape[1] - 1, vT, 0.0).astype(BF16)


def _band_valid(diff, window):
    return lax.bitcast_convert_type(diff, jnp.uint32) < jnp.uint32(window)


def _nsa_attn_kernel(q_ref, kc_ref, vcT_ref, ovT_ref, ks_ref, vsT_ref, kw_ref, vwT_ref, gT_ref, sg_ref,
                     o_ref, *, n_top):
    qi = pl.program_id(2)
    tq = NSA_TQ
    G = NSA_HEADS // NSA_KV_HEADS
    d = NSA_HEAD_DIM
    n = G * tq
    q = q_ref[0].reshape(n, LANES)
    q0 = qi * tq

    def col_qpos(shape):
        return q0 + (lax.broadcasted_iota(jnp.int32, shape, 1) & (tq - 1))

    s = _dot_nt(kc_ref[0, 0], q)
    cmp_end = lax.broadcasted_iota(jnp.int32, s.shape, 0) * NSA_CMP_STRIDE + (NSA_CMP_BLOCK - 1)
    valid = cmp_end <= col_qpos(s.shape)
    s = jnp.where(valid, s, NEG_INF)
    m = jnp.max(s, axis=0, keepdims=True)
    e = jnp.where(valid, jnp.exp2(s - m), 0.0)
    l = jnp.sum(e, axis=0, keepdims=True)
    p = (e * jnp.where(l > 0.0, 1.0 / l, 0.0)).astype(BF16)
    o_c = _dot(vcT_ref[0, 0], p)

    imp_all = _dot(ovT_ref[...], p)
    imp = imp_all[:, 0:tq]
    for g in range(1, G):
        imp = imp + imp_all[:, g * tq:(g + 1) * tq]
    ns = imp.shape[0]
    blk = lax.broadcasted_iota(jnp.int32, imp.shape, 0)
    sel_shift = NSA_SEL_BLOCK.bit_length() - 1
    q_blk = lax.shift_right_logical(q0 + lax.broadcasted_iota(jnp.int32, imp.shape, 1), sel_shift)
    causal = blk <= q_blk
    forced = (blk == 0) | (blk == q_blk) | (blk == q_blk - 1)
    val = jnp.where(causal, imp + jnp.where(forced, NSA_FORCE_BONUS, 0.0), -1.0)
    rank = jnp.zeros(imp.shape, F32)
    for i in range(ns):
        row = val[i:i + 1, :]
        rank = rank + jnp.where(blk > i, jnp.where(row >= val, 1.0, 0.0), jnp.where(row > val, 1.0, 0.0))
    bias = jnp.where(causal, jnp.where(rank < n_top, 0.0, NEG_INF), NEG_INF)
    half = LANES // 2
    bias_rows = [jnp.zeros((half, tq), F32), bias]
    if ns < half:
        bias_rows.append(jnp.zeros((half - ns, tq), F32))
    bias_rows = jnp.concatenate(bias_rows, axis=0).T.astype(BF16)
    lane = lax.broadcasted_iota(jnp.int32, (G, tq, LANES), 2)
    q_sel = jnp.where(lane < half, q.reshape(G, tq, LANES), bias_rows[None]).reshape(n, LANES)

    tk = NSA_SEL_TK
    slabs = tk // LANES

    def sel_step(kb, carry, diagonal):
        m, l, acc = carry
        k = ks_ref[0, 0, pl.ds(pl.multiple_of(kb * tk, tk), tk), :]
        s = _dot_nt(k, q_sel)
        if diagonal:
            kpos = kb * tk + lax.broadcasted_iota(jnp.int32, s.shape, 0)
            s = jnp.where(kpos <= col_qpos(s.shape), s, NEG_INF)
        m_new = jnp.maximum(m, jnp.max(s, axis=0, keepdims=True))
        alpha = jnp.exp2(m - m_new)
        p = jnp.exp2(s - m_new)
        l = alpha * l + jnp.sum(p, axis=0, keepdims=True)
        vt = jnp.concatenate([vsT_ref[0, 0, kb * slabs + c] for c in range(slabs)], axis=1)
        acc = alpha * acc + _dot(vt, p.astype(BF16))
        return m_new, l, acc

    init = (jnp.full((1, n), NEG_INF, F32), jnp.zeros((1, n), F32), jnp.zeros((d, n), F32))
    kb_diag = lax.shift_right_logical(q0, tk.bit_length() - 1)
    carry = lax.fori_loop(0, kb_diag, lambda kb, c: sel_step(kb, c, False), init)
    _, l_s, acc_s = sel_step(kb_diag, carry, True)
    o_s = acc_s * (1.0 / l_s)

    span = NSA_WINDOW + tq
    blk0 = jnp.maximum(qi - NSA_WINDOW // tq, 0)
    base = pl.multiple_of(blk0 * LANES, LANES)
    s = _dot_nt(kw_ref[0, 0, pl.ds(base, span), :], q)
    diff = col_qpos(s.shape) - (base + lax.broadcasted_iota(jnp.int32, s.shape, 0))
    s = jnp.where(_band_valid(diff, NSA_WINDOW), s, NEG_INF)
    e = jnp.exp2(s - jnp.max(s, axis=0, keepdims=True))
    vt = jnp.concatenate([vwT_ref[0, 0, blk0 + c] for c in range(span // LANES)], axis=1)
    o_w = _dot(vt, e.astype(BF16)) * (1.0 / jnp.sum(e, axis=0, keepdims=True))

    gates = gT_ref[0, 0]
    outs = []
    for g in range(G):
        cols = slice(g * tq, (g + 1) * tq)
        outs.append(gates[3 * g:3 * g + 1, :] * o_c[:, cols]
                    + gates[3 * g + 1:3 * g + 2, :] * o_s[:, cols]
                    + gates[3 * g + 2:3 * g + 3, :] * o_w[:, cols])
    o = jnp.concatenate(outs, axis=0).T
    o_ref[0] = (o * sg_ref[0]).astype(BF16)


def _nsa_overlap_T(nc_pad, ns):
    nc = nc_pad - (NSA_CMP_BLOCK // NSA_CMP_STRIDE - 1)
    cs = np.arange(nc_pad)[None, :] * NSA_CMP_STRIDE
    ss = np.arange(ns)[:, None] * NSA_SEL_BLOCK
    ov = np.clip(np.minimum(cs + NSA_CMP_BLOCK, ss + NSA_SEL_BLOCK) - np.maximum(cs, ss), 0, None)
    ov = np.where(np.arange(nc_pad)[None, :] < nc, ov, 0)
    return jnp.asarray(ov / NSA_CMP_BLOCK, dtype=BF16)


def _nsa_layer(x2, B, S, norm_g, scale, shift, cos, sin, w_in, cmp_pos, w_k1, w_k2, w_v1, w_v2):
    T, D = x2.shape
    H, KV, d = NSA_HEADS, NSA_KV_HEADS, NSA_HEAD_DIM
    G = H // KV
    W = H * d
    KW = KV * d
    offs = np.cumsum([0, W] + [KW] * 6 + [3 * H, W])
    w_q, w_kc, w_vc, w_ks, w_vs, w_kw, w_vw, w_g, w_gate = (
        w_in[:, offs[i]:offs[i + 1]] for i in range(9))
    w_gT = jnp.pad(w_g.T.reshape(KV, 3 * G, D), ((0, 0), (0, 16 - 3 * G), (0, 0))).reshape(KV * 16, D)

    tm = PROJ_ROWS
    tpb = S // tm
    full = lambda shape: pl.BlockSpec(shape, lambda i: (0,) * len(shape))
    rows = lambda w: pl.BlockSpec((tm, w), lambda i: (i, 0))
    per_b = pl.BlockSpec((1, 1, D), lambda i: (i // tpb, 0, 0))
    tab = pl.BlockSpec((1, tm, LANES), lambda i: (i // tpb, i % tpb, 0))
    heads = lambda nh: pl.BlockSpec((1, nh, tm, d), lambda i: (i // tpb, 0, i % tpb, 0))
    wide = lambda nh: pl.BlockSpec((1, nh, tm, LANES), lambda i: (i // tpb, 0, i % tpb, 0))
    slab = pl.BlockSpec((1, KV, tm // LANES, d, LANES), lambda i: (i // tpb, 0, i % tpb, 0, 0))
    kv_f32 = jax.ShapeDtypeStruct((B, KV, S, d), F32)
    kv_bf16 = jax.ShapeDtypeStruct((B, KV, S, LANES), BF16)
    kv_slab = jax.ShapeDtypeStruct((B, KV, S // LANES, d, LANES), BF16)
    assert S // NSA_SEL_BLOCK <= LANES // 2, "selection-block one-hot must fit in lanes 64..127"
    q, kc, vc, ks, vsT, kw, vwT, gT, sg = pl.pallas_call(
        functools.partial(_nsa_proj_kernel, tiles_per_seq=tpb),
        out_shape=(jax.ShapeDtypeStruct((B, H, S, LANES), BF16), kv_f32, kv_f32,
                   kv_bf16, kv_slab, kv_bf16, kv_slab,
                   jax.ShapeDtypeStruct((B, KV, 16, S), F32),
                   jax.ShapeDtypeStruct((T, W), F32)),
        grid=(T // tm,),
        in_specs=[rows(D), per_b, per_b, full((1, D)), tab, tab,
                  full((D, W)), full((D, KW)), full((D, KW)), full((D, KW)), full((KW, D)),
                  full((D, KW)), full((KW, D)), full((KV * 16, D)), full((D, W))],
        out_specs=(wide(H), heads(KV), heads(KV), wide(KV), slab, wide(KV), slab,
                   pl.BlockSpec((1, KV, 16, tm), lambda i: (i // tpb, 0, 0, i % tpb)),
                   rows(W)),
        compiler_params=_params(1),
        name="nsa_proj",
    )(x2, scale[:, None, :], shift[:, None, :], norm_g[None, :], cos, sin,
      w_q.astype(BF16), w_kc.astype(BF16), w_vc.astype(BF16), w_ks.astype(BF16), w_vs.T.astype(BF16),
      w_kw.astype(BF16), w_vw.T.astype(BF16), w_gT.astype(BF16), w_gate.astype(BF16))

    n_chunks = S // NSA_CMP_STRIDE
    cw = NSA_CMP_STRIDE * d
    pe = cmp_pos.reshape(2, 1, cw)
    chunk_spec = pl.BlockSpec((1, 1, n_chunks, cw), lambda b, kv: (b, kv, 0, 0))
    full2 = lambda shape: pl.BlockSpec(shape, lambda b, kv: (0,) * len(shape))
    k_c, v_cT = pl.pallas_call(
        _nsa_compress_kernel,
        out_shape=(jax.ShapeDtypeStruct((B, KV, n_chunks, LANES), BF16),
                   jax.ShapeDtypeStruct((B, KV, d, n_chunks), BF16)),
        grid=(B, KV),
        in_specs=[chunk_spec, chunk_spec, full2((2, 1, cw)),
                  full2((2, cw, NSA_CMP_HIDDEN)), full2((NSA_CMP_HIDDEN, LANES)),
                  full2((2, cw, NSA_CMP_HIDDEN)), full2((d, NSA_CMP_HIDDEN))],
        out_specs=(pl.BlockSpec((1, 1, n_chunks, LANES), lambda b, kv: (b, kv, 0, 0)),
                   pl.BlockSpec((1, 1, d, n_chunks), lambda b, kv: (b, kv, 0, 0))),
        compiler_params=_params(2),
        name="nsa_compress",
    )(kc.reshape(B, KV, n_chunks, cw), vc.reshape(B, KV, n_chunks, cw), pe,
      w_k1.reshape(2, cw, NSA_CMP_HIDDEN).astype(BF16),
      jnp.pad(w_k2, ((0, 0), (0, LANES - d))).astype(BF16),
      w_v1.reshape(2, cw, NSA_CMP_HIDDEN).astype(BF16), w_v2.T.astype(BF16))

    tq = NSA_TQ
    ns = S // NSA_SEL_BLOCK
    ovT = _nsa_overlap_T(n_chunks, ns)
    whole = lambda shape: pl.BlockSpec((1, 1) + shape, lambda b, kv, i: (b, kv) + (0,) * len(shape))
    og = pl.pallas_call(
        functools.partial(_nsa_attn_kernel, n_top=min(NSA_N_SELECT, ns)),
        out_shape=jax.ShapeDtypeStruct((B, S, W), BF16),
        grid=(B, KV, S // tq),
        in_specs=[pl.BlockSpec((1, G, tq, LANES), lambda b, kv, i: (b, kv, i, 0)),
                  whole((n_chunks, LANES)), whole((d, n_chunks)),
                  pl.BlockSpec((ns, n_chunks), lambda b, kv, i: (0, 0)),
                  whole((S, LANES)), whole((S // LANES, d, LANES)),
                  whole((S, LANES)), whole((S // LANES, d, LANES)),
                  pl.BlockSpec((1, 1, 16, tq), lambda b, kv, i: (b, kv, 0, i)),
                  pl.BlockSpec((1, tq, G * d), lambda b, kv, i: (b, i, kv))],
        out_specs=pl.BlockSpec((1, tq, G * d), lambda b, kv, i: (b, i, kv)),
        compiler_params=_params(3),
        name="nsa_attn",
    )(q, k_c, v_cT, ovT, ks, vsT, kw, vwT, gT, sg.reshape(B, S, W))
    return og.reshape(T, W)


def kernel(x, c, positions, norm_g, ada_w, ada_b, mla_w_in, mla_q_norm_g, mla_kv_norm_g, mla_w_q_b, mla_w_kv_b, mla_w_out, swa_w_in, swa_sinks, swa_w_out, nsa_w_in, nsa_cmp_pos, nsa_w_cmp_k1, nsa_w_cmp_k2, nsa_w_cmp_v1, nsa_w_cmp_v2, nsa_w_out, final_norm_g):
    B, S, D = x.shape
    depth = norm_g.shape[0]
    cos, sin = _rope_tables(positions)
    mod = _ada_modulation(c, ada_w, ada_b)
    x2 = x.reshape(B * S, D)
    for i in range(depth):
        shift, scale, gate = mod[i, :, :D], mod[i, :, D:2 * D], mod[i, :, 2 * D:]
        kind, j = i % 3, i // 3
        if kind == 0:
            og = _mla_layer(x2, B, S, norm_g[i], scale, shift, cos, sin, mla_w_in[j], mla_q_norm_g[j],
                            mla_kv_norm_g[j], mla_w_q_b[j], mla_w_kv_b[j])
            w_out = mla_w_out[j]
        elif kind == 1:
            og = _swa_layer(x2, B, S, norm_g[i], scale, shift, cos, sin, swa_w_in[j], swa_sinks[j])
            w_out = swa_w_out[j]
        else:
            og = _nsa_layer(x2, B, S, norm_g[i], scale, shift, cos, sin, nsa_w_in[j], nsa_cmp_pos[j],
                            nsa_w_cmp_k1[j], nsa_w_cmp_k2[j], nsa_w_cmp_v1[j], nsa_w_cmp_v2[j])
            w_out = nsa_w_out[j]
        x2 = _out_proj(og, w_out, x2, gate, final_norm_g, S, final=(i == depth - 1))
    return x2.reshape(B, S, D)
```

```python
import functools

import jax
import jax.numpy as jnp
import numpy as np
from jax import lax
from jax.experimental import pallas as pl
from jax.experimental.pallas import tpu as pltpu

F32 = jnp.float32
BF16 = jnp.bfloat16

ROPE_THETA = 10000.0
RMS_EPS = 1e-6
NEG_INF = -1e30
LOG2_E = 1.4426950408889634

MLA_HEADS = 8
MLA_NOPE = 128
MLA_ROPE = 64
MLA_V = 128
MLA_Q_LORA = 256
MLA_KV_LORA = 128

SWA_HEADS = 16
SWA_KV_HEADS = 2
SWA_HEAD_DIM = 64
SWA_WINDOW = 128

NSA_HEADS = 16
NSA_KV_HEADS = 4
NSA_HEAD_DIM = 64
NSA_CMP_BLOCK = 32
NSA_CMP_STRIDE = 16
NSA_CMP_HIDDEN = 128
NSA_SEL_BLOCK = 64
NSA_N_SELECT = 16
NSA_WINDOW = 512
NSA_FORCE_BONUS = 1e4

LANES = 128
ROPE_HALF = 32
VMEM_LIMIT = 56 * 1024 * 1024

PROJ_ROWS = 512
OUT_ROWS = 512
MLA_TQ = 512
MLA_HEADS_PER_STEP = 4
SWA_TQ = 128
NSA_TQ = 128
NSA_SEL_TK = 512
NSA_KV_PER_STEP = 4


def _dot(a, b):
    return jnp.dot(a, b, preferred_element_type=F32)


def _dot_nt(a, b):
    return lax.dot_general(a, b, (((1,), (1,)), ((), ())), preferred_element_type=F32)


def _silu(x):
    return x / (1.0 + jnp.exp(-x))


def _sigmoid(x):
    return 1.0 / (1.0 + jnp.exp(-x))


def _rms(x, g):
    return x * lax.rsqrt(jnp.mean(x * x, axis=-1, keepdims=True) + RMS_EPS) * g


def _params(n_axes):
    return pltpu.CompilerParams(dimension_semantics=("arbitrary",) * n_axes,
                                vmem_limit_bytes=VMEM_LIMIT)


def _rope128(x, cos, sin_signed):
    lane = lax.broadcasted_iota(jnp.int32, x.shape, 1)
    lower_half = (lane & (2 * ROPE_HALF - 1)) < ROPE_HALF
    partner = jnp.where(lower_half, pltpu.roll(x, LANES - ROPE_HALF, axis=1),
                        pltpu.roll(x, ROPE_HALF, axis=1))
    return x * cos + partner * sin_signed


def _rope_table_kernel(pos_ref, freq_ref, sign_ref, cos_ref, sin_ref):
    ang = pos_ref[0] * freq_ref[...]
    cos_ref[0] = jnp.cos(ang)
    sin_ref[0] = jnp.sin(ang) * sign_ref[...]


def _rope_tables(positions):
    B, S = positions.shape
    inv_freq = ROPE_THETA ** (-jnp.arange(ROPE_HALF, dtype=F32) / ROPE_HALF)
    freq = jnp.tile(inv_freq, LANES // ROPE_HALF)[None, :]
    sign = jnp.tile(jnp.concatenate([-jnp.ones(ROPE_HALF, F32), jnp.ones(ROPE_HALF, F32)]),
                    LANES // (2 * ROPE_HALF))[None, :]
    pos = positions.astype(F32)[..., None]
    ts = min(S, 512)
    spec = pl.BlockSpec((1, ts, LANES), lambda b, s: (b, s, 0))
    return pl.pallas_call(
        _rope_table_kernel,
        out_shape=(jax.ShapeDtypeStruct((B, S, LANES), F32),) * 2,
        grid=(B, S // ts),
        in_specs=[pl.BlockSpec((1, ts, 1), lambda b, s: (b, s, 0)),
                  pl.BlockSpec((1, LANES), lambda b, s: (0, 0)),
                  pl.BlockSpec((1, LANES), lambda b, s: (0, 0))],
        out_specs=(spec, spec),
        compiler_params=_params(2),
        name="rope_tables",
    )(pos, freq, sign)


def _ada_kernel(c_ref, w_ref, b_ref, o_ref):
    cond = _silu(c_ref[...]).astype(BF16)
    o_ref[0] = _dot(cond, w_ref[0]) + b_ref[0]


def _ada_modulation(c, ada_w, ada_b):
    depth, D, D3 = ada_w.shape
    B = c.shape[0]
    Bp = -(-B // 16) * 16
    c_pad = jnp.pad(c, ((0, Bp - B), (0, 0)))
    tn = 1024
    out = pl.pallas_call(
        _ada_kernel,
        out_shape=jax.ShapeDtypeStruct((depth, Bp, D3), F32),
        grid=(depth, D3 // tn),
        in_specs=[pl.BlockSpec((Bp, D), lambda i, n: (0, 0)),
                  pl.BlockSpec((1, D, tn), lambda i, n: (i, 0, n)),
                  pl.BlockSpec((1, 1, tn), lambda i, n: (i, 0, n))],
        out_specs=pl.BlockSpec((1, Bp, tn), lambda i, n: (i, 0, n)),
        compiler_params=_params(2),
        name="ada_modulation",
    )(c_pad, ada_w.astype(BF16), ada_b[:, None, :])
    return out[:, :B]


def _out_proj_kernel(og_ref, w_ref, x_ref, gate_ref, fg_ref, o_ref, *, final):
    y = _dot(og_ref[...], w_ref[...])
    xn = x_ref[...] + gate_ref[0] * y
    if final:
        xn = _rms(xn, fg_ref[...])
    o_ref[...] = xn


def _out_proj(og, w_out, x2, gate_c, final_g, S, final):
    T, D = x2.shape
    W = og.shape[1]
    tm = OUT_ROWS
    tpb = S // tm
    return pl.pallas_call(
        functools.partial(_out_proj_kernel, final=final),
        out_shape=jax.ShapeDtypeStruct((T, D), F32),
        grid=(T // tm,),
        in_specs=[pl.BlockSpec((tm, W), lambda i: (i, 0)),
                  pl.BlockSpec((W, D), lambda i: (0, 0)),
                  pl.BlockSpec((tm, D), lambda i: (i, 0)),
                  pl.BlockSpec((1, 1, D), lambda i: (i // tpb, 0, 0)),
                  pl.BlockSpec((1, D), lambda i: (0, 0))],
        out_specs=pl.BlockSpec((tm, D), lambda i: (i, 0)),
        compiler_params=_params(1),
        name="out_proj",
    )(og, w_out.astype(BF16), x2, gate_c[:, None, :], final_g[None, :])


def _modulated_norm(x_ref, g_ref, scale_ref, shift_ref):
    y = _rms(x_ref[...], g_ref[...])
    return (y * (1.0 + scale_ref[0]) + shift_ref[0]).astype(BF16)


def _store_heads(dst_ref, val, n_heads, width, dtype):
    for h in range(n_heads):
        dst_ref[0, h] = val[:, h * width:(h + 1) * width].astype(dtype)


def _store_heads_padded(dst_ref, val, n_heads, upper=None):
    half = LANES // 2
    lane = lax.broadcasted_iota(jnp.int32, (val.shape[0], LANES), 1)
    fill = 0.0 if upper is None else upper
    for h in range(n_heads):
        chunk = val[:, (h // 2) * LANES:(h // 2 + 1) * LANES]
        if h % 2:
            chunk = pltpu.roll(chunk, half, axis=1)
        dst_ref[0, h] = jnp.where(lane < half, chunk, fill).astype(BF16)


def _store_slabs(dst_ref, val, n_heads, d):
    for h in range(n_heads):
        for c in range(val.shape[1] // LANES):
            dst_ref[0, h, c] = val[h * d:(h + 1) * d, c * LANES:(c + 1) * LANES].astype(BF16)


def _rope_wide(val, cos, sin_signed):
    return jnp.concatenate(
        [_rope128(val[:, c * LANES:(c + 1) * LANES], cos, sin_signed)
         for c in range(val.shape[1] // LANES)], axis=1)


def _mla_proj_kernel(x_ref, scale_ref, shift_ref, g_ref, cos_ref, sin_ref,
                     wqa_ref, wkva_ref, wkpe_ref, wgate_ref, qg_ref, kvg_ref,
                     wqb_ref, wkb_ref, wvbT_ref,
                     q_ref, k_ref, vT_ref, sg_ref):
    h = _modulated_norm(x_ref, g_ref, scale_ref, shift_ref)
    cos, sin = cos_ref[0], sin_ref[0]
    head_w = MLA_NOPE + LANES
    q_scale = (MLA_NOPE + MLA_ROPE) ** -0.5 * LOG2_E

    qn = _rms(_dot(h, wqa_ref[...]), qg_ref[...]).astype(BF16)
    q = _dot(qn, wqb_ref[...]) * q_scale
    for hd in range(MLA_HEADS):
        lo = hd * head_w
        q_ref[:, lo:lo + MLA_NOPE] = q[:, lo:lo + MLA_NOPE].astype(BF16)
        q_ref[:, lo + MLA_NOPE:lo + head_w] = _rope128(
            q[:, lo + MLA_NOPE:lo + head_w], cos, sin).astype(BF16)

    kvn = _rms(_dot(h, wkva_ref[...]), kvg_ref[...]).astype(BF16)
    kn = _dot(kvn, wkb_ref[...])
    kpe = _rope128(_dot(h, wkpe_ref[...]), cos, sin).astype(BF16)
    for hd in range(MLA_HEADS):
        lo = hd * head_w
        k_ref[:, lo:lo + MLA_NOPE] = kn[:, hd * MLA_NOPE:(hd + 1) * MLA_NOPE].astype(BF16)
        k_ref[:, lo + MLA_NOPE:lo + head_w] = kpe
    _store_slabs(vT_ref, _dot_nt(wvbT_ref[...], kvn), MLA_HEADS, MLA_V)

    sg_ref[...] = _silu(_dot(h, wgate_ref[...]))


def _mla_attn_kernel(q_ref, k_ref, vT_ref, sg_ref, o_ref):
    qi = pl.program_id(2)
    tq = tk = MLA_TQ
    slabs = tk // LANES
    head_w = MLA_NOPE + LANES
    heads = range(MLA_HEADS_PER_STEP)

    def scores(h, kb):
        cols = slice(h * head_w, (h + 1) * head_w)
        k = k_ref[0, pl.ds(pl.multiple_of(kb * tk, tk), tk), cols]
        return _dot_nt(k, q_ref[0, :, cols])

    def absorb(h, kb, s, carry):
        m, l, acc = carry
        m_new = jnp.maximum(m, jnp.max(s, axis=0, keepdims=True))
        alpha = jnp.exp2(m - m_new)
        p = jnp.exp2(s - m_new)
        l = alpha * l + jnp.sum(p, axis=0, keepdims=True)
        vt = jnp.concatenate([vT_ref[0, h, kb * slabs + c] for c in range(slabs)], axis=1)
        acc = alpha * acc + _dot(vt, p.astype(BF16))
        return m_new, l, acc

    def step(kb, carries, masked):
        s = [scores(h, kb) for h in heads]
        if masked:
            kpos = lax.broadcasted_iota(jnp.int32, (tk, tq), 0)
            qpos = lax.broadcasted_iota(jnp.int32, (tk, tq), 1)
            s = [jnp.where(kpos <= qpos, sh, NEG_INF) for sh in s]
        return tuple(absorb(h, kb, s[h], carries[h]) for h in heads)

    init = (jnp.full((1, tq), NEG_INF, F32), jnp.zeros((1, tq), F32), jnp.zeros((MLA_V, tq), F32))
    carries = lax.fori_loop(0, qi, lambda kb, c: step(kb, c, False), (init,) * len(heads))
    carries = step(qi, carries, True)
    for h, (_, l, acc) in enumerate(carries):
        cols = slice(h * MLA_V, (h + 1) * MLA_V)
        o = (acc * (1.0 / l)).T
        o_ref[0, :, cols] = (o * sg_ref[0, :, cols]).astype(BF16)


def _mla_layer(x2, B, S, norm_g, scale, shift, cos, sin, w_in, q_norm_g, kv_norm_g, w_q_b, w_kv_b):
    T, D = x2.shape
    H = MLA_HEADS
    head_w = MLA_NOPE + LANES
    o1 = MLA_Q_LORA
    o2 = o1 + MLA_KV_LORA
    o3 = o2 + MLA_ROPE
    w_qa, w_kva, w_kpe, w_gate = (w_in[:, :o1], w_in[:, o1:o2], w_in[:, o2:o3], w_in[:, o3:])
    w_kpe = jnp.pad(w_kpe, ((0, 0), (0, LANES - MLA_ROPE)))
    wq = w_q_b.reshape(MLA_Q_LORA, H, MLA_NOPE + MLA_ROPE)
    wq = jnp.pad(wq, ((0, 0), (0, 0), (0, head_w - MLA_NOPE - MLA_ROPE))).reshape(MLA_Q_LORA, H * head_w)
    wkv = w_kv_b.reshape(MLA_KV_LORA, H, MLA_NOPE + MLA_V)
    w_kb = wkv[:, :, :MLA_NOPE].reshape(MLA_KV_LORA, H * MLA_NOPE)
    w_vbT = wkv[:, :, MLA_NOPE:].reshape(MLA_KV_LORA, H * MLA_V).T
    W = H * MLA_V

    tm = PROJ_ROWS
    tpb = S // tm
    full = lambda shape: pl.BlockSpec(shape, lambda i: (0,) * len(shape))
    rows = lambda w: pl.BlockSpec((tm, w), lambda i: (i, 0))
    per_b = pl.BlockSpec((1, 1, D), lambda i: (i // tpb, 0, 0))
    tab = pl.BlockSpec((1, tm, LANES), lambda i: (i // tpb, i % tpb, 0))
    q, k, vT, sg = pl.pallas_call(
        _mla_proj_kernel,
        out_shape=(jax.ShapeDtypeStruct((T, H * head_w), BF16),
                   jax.ShapeDtypeStruct((T, H * head_w), BF16),
                   jax.ShapeDtypeStruct((B, H, S // LANES, MLA_V, LANES), BF16),
                   jax.ShapeDtypeStruct((T, W), F32)),
        grid=(T // tm,),
        in_specs=[rows(D), per_b, per_b, full((1, D)), tab, tab,
                  full((D, o1)), full((D, MLA_KV_LORA)), full((D, LANES)), full((D, W)),
                  full((1, o1)), full((1, MLA_KV_LORA)),
                  full((o1, H * head_w)), full((MLA_KV_LORA, H * MLA_NOPE)), full((W, MLA_KV_LORA))],
        out_specs=(rows(H * head_w), rows(H * head_w),
                   pl.BlockSpec((1, H, tm // LANES, MLA_V, LANES),
                                lambda i: (i // tpb, 0, i % tpb, 0, 0)),
                   rows(W)),
        compiler_params=_params(1),
        name="mla_proj",
    )(x2, scale[:, None, :], shift[:, None, :], norm_g[None, :], cos, sin,
      w_qa.astype(BF16), w_kva.astype(BF16), w_kpe.astype(BF16), w_gate.astype(BF16),
      q_norm_g[None, :], kv_norm_g[None, :],
      wq.astype(BF16), w_kb.astype(BF16), w_vbT.astype(BF16))

    tq = MLA_TQ
    hp = MLA_HEADS_PER_STEP
    q3 = q.reshape(B, S, H * head_w)
    k3 = k.reshape(B, S, H * head_w)
    sg3 = sg.reshape(B, S, W)
    og = pl.pallas_call(
        _mla_attn_kernel,
        out_shape=jax.ShapeDtypeStruct((B, S, W), BF16),
        grid=(B, H // hp, S // tq),
        in_specs=[pl.BlockSpec((1, tq, hp * head_w), lambda b, h, i: (b, i, h)),
                  pl.BlockSpec((1, S, hp * head_w), lambda b, h, i: (b, 0, h)),
                  pl.BlockSpec((1, hp, S // LANES, MLA_V, LANES), lambda b, h, i: (b, h, 0, 0, 0)),
                  pl.BlockSpec((1, tq, hp * MLA_V), lambda b, h, i: (b, i, h))],
        out_specs=pl.BlockSpec((1, tq, hp * MLA_V), lambda b, h, i: (b, i, h)),
        compiler_params=_params(3),
        name="mla_attn",
    )(q3, k3, vT, sg3)
    return og.reshape(T, W)


def _swa_proj_kernel(x_ref, scale_ref, shift_ref, g_ref, cos_ref, sin_ref,
                     wq_ref, wk_ref, wvT_ref, wgate_ref,
                     q_ref, k_ref, vT_ref, sg_ref):
    h = _modulated_norm(x_ref, g_ref, scale_ref, shift_ref)
    cos, sin = cos_ref[0], sin_ref[0]
    q = _rope_wide(_dot(h, wq_ref[...]) * (SWA_HEAD_DIM ** -0.5 * LOG2_E), cos, sin)
    _store_heads(q_ref, q, SWA_HEADS, SWA_HEAD_DIM, BF16)
    k = _rope_wide(_dot(h, wk_ref[...]), cos, sin)
    _store_heads(k_ref, k, SWA_KV_HEADS, SWA_HEAD_DIM, BF16)
    _store_slabs(vT_ref, _dot_nt(wvT_ref[...], h), SWA_KV_HEADS, SWA_HEAD_DIM)
    sg_ref[...] = _silu(_dot(h, wgate_ref[...]))


def _band_valid(diff, window):
    return lax.bitcast_convert_type(diff, jnp.uint32) < jnp.uint32(window)


def _swa_attn_kernel(q_ref, k_ref, vT_ref, sink_ref, sg_ref, o_ref):
    qi = pl.program_id(1)
    tq = SWA_TQ
    G = SWA_HEADS // SWA_KV_HEADS
    d = SWA_HEAD_DIM
    n = G * tq
    span = SWA_WINDOW + tq
    groups = range(SWA_KV_HEADS)
    blk0 = jnp.maximum(qi - SWA_WINDOW // tq, 0)
    base = pl.multiple_of(blk0 * LANES, LANES)
    s_all = [_dot_nt(k_ref[0, i, pl.ds(base, span), :], q_ref[0, i * G:(i + 1) * G].reshape(n, d))
             for i in groups]
    kpos = base + lax.broadcasted_iota(jnp.int32, (span, n), 0)
    qpos = qi * tq + (lax.broadcasted_iota(jnp.int32, (span, n), 1) & (tq - 1))
    in_band = _band_valid(qpos - kpos, SWA_WINDOW)
    for i in groups:
        s = jnp.where(in_band, s_all[i], NEG_INF)
        sink = sink_ref[i]
        m = jnp.maximum(jnp.max(s, axis=0, keepdims=True), sink)
        e = jnp.exp2(s - m)
        denom = jnp.sum(e, axis=0, keepdims=True) + jnp.exp2(sink - m)
        vt = jnp.concatenate([vT_ref[0, i, blk0 + c] for c in range(span // LANES)], axis=1)
        oT = _dot(vt, e.astype(BF16)) * (1.0 / denom)
        o = jnp.concatenate([oT[:, g * tq:(g + 1) * tq] for g in range(G)], axis=0).T
        cols = slice(i * G * d, (i + 1) * G * d)
        o_ref[0, :, cols] = (o * sg_ref[0, :, cols]).astype(BF16)


def _swa_layer(x2, B, S, norm_g, scale, shift, cos, sin, w_in, sinks):
    T, D = x2.shape
    H, KV, d = SWA_HEADS, SWA_KV_HEADS, SWA_HEAD_DIM
    G = H // KV
    W = H * d
    KW = KV * d
    w_q, w_k, w_v, w_gate = (w_in[:, :W], w_in[:, W:W + KW], w_in[:, W + KW:W + 2 * KW],
                             w_in[:, W + 2 * KW:])
    tm = PROJ_ROWS
    tpb = S // tm
    full = lambda shape: pl.BlockSpec(shape, lambda i: (0,) * len(shape))
    rows = lambda w: pl.BlockSpec((tm, w), lambda i: (i, 0))
    per_b = pl.BlockSpec((1, 1, D), lambda i: (i // tpb, 0, 0))
    tab = pl.BlockSpec((1, tm, LANES), lambda i: (i // tpb, i % tpb, 0))
    heads = lambda nh: pl.BlockSpec((1, nh, tm, d), lambda i: (i // tpb, 0, i % tpb, 0))
    q, k, vT, sg = pl.pallas_call(
        _swa_proj_kernel,
        out_shape=(jax.ShapeDtypeStruct((B, H, S, d), BF16),
                   jax.ShapeDtypeStruct((B, KV, S, d), BF16),
                   jax.ShapeDtypeStruct((B, KV, S // LANES, d, LANES), BF16),
                   jax.ShapeDtypeStruct((T, W), F32)),
        grid=(T // tm,),
        in_specs=[rows(D), per_b, per_b, full((1, D)), tab, tab,
                  full((D, W)), full((D, KW)), full((KW, D)), full((D, W))],
        out_specs=(heads(H), heads(KV),
                   pl.BlockSpec((1, KV, tm // LANES, d, LANES), lambda i: (i // tpb, 0, i % tpb, 0, 0)),
                   rows(W)),
        compiler_params=_params(1),
        name="swa_proj",
    )(x2, scale[:, None, :], shift[:, None, :], norm_g[None, :], cos, sin,
      w_q.astype(BF16), w_k.astype(BF16), w_v.T.astype(BF16), w_gate.astype(BF16))

    tq = SWA_TQ
    n = G * tq
    sink_rows = jnp.repeat(sinks.astype(F32).reshape(KV, G) * LOG2_E, tq, axis=1)[:, None, :]
    og = pl.pallas_call(
        _swa_attn_kernel,
        out_shape=jax.ShapeDtypeStruct((B, S, W), BF16),
        grid=(B, S // tq),
        in_specs=[pl.BlockSpec((1, H, tq, d), lambda b, i: (b, 0, i, 0)),
                  pl.BlockSpec((1, KV, S, d), lambda b, i: (b, 0, 0, 0)),
                  pl.BlockSpec((1, KV, S // LANES, d, LANES), lambda b, i: (b, 0, 0, 0, 0)),
                  pl.BlockSpec((KV, 1, n), lambda b, i: (0, 0, 0)),
                  pl.BlockSpec((1, tq, W), lambda b, i: (b, i, 0))],
        out_specs=pl.BlockSpec((1, tq, W), lambda b, i: (b, i, 0)),
        compiler_params=_params(2),
        name="swa_attn",
    )(q, k, vT, sink_rows, sg.reshape(B, S, W))
    return og.reshape(T, W)


def _nsa_proj_kernel(x_ref, scale_ref, shift_ref, g_ref, cos_ref, sin_ref,
                     wq_ref, wkc_ref, wvc_ref, wks_ref, wvsT_ref, wkw_ref, wvwT_ref, wgT_ref, wgate_ref,
                     q_ref, kc_ref, vc_ref, ks_ref, vsT_ref, kw_ref, vwT_ref, gT_ref, sg_ref, *, tiles_per_seq):
    h = _modulated_norm(x_ref, g_ref, scale_ref, shift_ref)
    cos, sin = cos_ref[0], sin_ref[0]
    KV, d = NSA_KV_HEADS, NSA_HEAD_DIM
    tm = h.shape[0]
    q = _rope_wide(_dot(h, wq_ref[...]) * (d ** -0.5 * LOG2_E), cos, sin)
    _store_heads_padded(q_ref, q, NSA_HEADS)
    _store_heads(kc_ref, _dot(h, wkc_ref[...]), KV, d, F32)
    _store_heads(vc_ref, _dot(h, wvc_ref[...]), KV, d, F32)
    pos = (pl.program_id(0) % tiles_per_seq) * tm + lax.broadcasted_iota(jnp.int32, (tm, LANES), 0)
    lane = lax.broadcasted_iota(jnp.int32, (tm, LANES), 1)
    sel_shift = NSA_SEL_BLOCK.bit_length() - 1
    block_onehot = jnp.where(lane - LANES // 2 == lax.shift_right_logical(pos, sel_shift), 1.0, 0.0)
    _store_heads_padded(ks_ref, _rope_wide(_dot(h, wks_ref[...]), cos, sin), KV, upper=block_onehot)
    _store_heads_padded(kw_ref, _rope_wide(_dot(h, wkw_ref[...]), cos, sin), KV)
    _store_slabs(vsT_ref, _dot_nt(wvsT_ref[...], h), KV, d)
    _store_slabs(vwT_ref, _dot_nt(wvwT_ref[...], h), KV, d)
    gT = _sigmoid(_dot_nt(wgT_ref[...], h))
    for kv in range(KV):
        gT_ref[0, kv] = gT[kv * 16:(kv + 1) * 16, :]
    sg_ref[...] = _silu(_dot(h, wgate_ref[...]))


def _nsa_compress_kernel(kc_ref, vc_ref, pe_ref, wk1_ref, wk2_ref, wv1_ref, wv2T_ref, kout_ref, vT_ref):
    pe_top, pe_bot = pe_ref[0], pe_ref[1]

    def hidden(chunks, w1_ref):
        top = _dot((chunks + pe_top).astype(BF16), w1_ref[0])
        bot = _dot((chunks + pe_bot).astype(BF16), w1_ref[1])
        n = bot.shape[0]
        return _silu(top + pltpu.roll(bot, n - 1, axis=0)).astype(BF16)

    hk = hidden(kc_ref[0, 0], wk1_ref)
    kc = _dot(hk, wk2_ref[...])
    row = lax.broadcasted_iota(jnp.int32, kc.shape, 0)
    kout_ref[0, 0] = jnp.where(row < kc.shape[0] - 1, kc, 0.0).astype(BF16)
    hv = hidden(vc_ref[0, 0], wv1_ref)
    vT = _dot_nt(wv2T_ref[...], hv)
    col = lax.broadcasted_iota(jnp.int32, vT.shape, 1)
    vT_ref[0, 0] = jnp.where(col < vT.shape[1] - 1, vT, 0.0).astype(BF16)


def _nsa_attn_kernel(q_ref, kc_ref, vcT_ref, ovT_ref, ks_ref, vsT_ref, kw_ref, vwT_ref, gT_ref, sg_ref,
                     o_ref, *, n_top):
    qi = pl.program_id(2)
    tq = NSA_TQ
    G = NSA_HEADS // NSA_KV_HEADS
    d = NSA_HEAD_DIM
    n = G * tq
    groups = range(NSA_KV_PER_STEP)
    q0 = qi * tq
    q = [q_ref[0, i * G:(i + 1) * G].reshape(n, LANES) for i in groups]

    def col_qpos(shape):
        return q0 + (lax.broadcasted_iota(jnp.int32, shape, 1) & (tq - 1))

    span = NSA_WINDOW + tq
    blk0 = jnp.maximum(qi * (tq // LANES) - NSA_WINDOW // LANES, 0)
    base = pl.multiple_of(blk0 * LANES, LANES)
    s_cmp = [_dot_nt(kc_ref[0, i], q[i]) for i in groups]
    s_win = [_dot_nt(kw_ref[0, i, pl.ds(base, span), :], q[i]) for i in groups]

    cmp_end = lax.broadcasted_iota(jnp.int32, s_cmp[0].shape, 0) * NSA_CMP_STRIDE + (NSA_CMP_BLOCK - 1)
    valid = cmp_end <= col_qpos(s_cmp[0].shape)
    o_c, imp = [], []
    for i in groups:
        s = jnp.where(valid, s_cmp[i], NEG_INF)
        m = jnp.max(s, axis=0, keepdims=True)
        e = jnp.where(valid, jnp.exp2(s - m), 0.0)
        l = jnp.sum(e, axis=0, keepdims=True)
        p = (e * jnp.where(l > 0.0, 1.0 / l, 0.0)).astype(BF16)
        o_c.append(_dot(vcT_ref[0, i], p))
        imp_all = _dot(ovT_ref[...], p)
        acc = imp_all[:, 0:tq]
        for g in range(1, G):
            acc = acc + imp_all[:, g * tq:(g + 1) * tq]
        imp.append(acc)

    diff = col_qpos(s_win[0].shape) - (base + lax.broadcasted_iota(jnp.int32, s_win[0].shape, 0))
    in_band = _band_valid(diff, NSA_WINDOW)
    o_w = []
    for i in groups:
        s = jnp.where(in_band, s_win[i], NEG_INF)
        e = jnp.exp2(s - jnp.max(s, axis=0, keepdims=True))
        vt = jnp.concatenate([vwT_ref[0, i, blk0 + c] for c in range(span // LANES)], axis=1)
        o_w.append(_dot(vt, e.astype(BF16)) * (1.0 / jnp.sum(e, axis=0, keepdims=True)))

    ns = imp[0].shape[0]
    blk = lax.broadcasted_iota(jnp.int32, imp[0].shape, 0)
    sel_shift = NSA_SEL_BLOCK.bit_length() - 1
    q_blk = lax.shift_right_logical(q0 + lax.broadcasted_iota(jnp.int32, imp[0].shape, 1), sel_shift)
    causal = blk <= q_blk
    forced = (blk == 0) | (blk == q_blk) | (blk == q_blk - 1)
    half = LANES // 2
    lane = lax.broadcasted_iota(jnp.int32, (G, tq, LANES), 2)
    q_sel = []
    for i in groups:
        val = jnp.where(causal, imp[i] + jnp.where(forced, NSA_FORCE_BONUS, 0.0), -1.0)
        rank = jnp.zeros(val.shape, F32)
        for r in range(ns):
            row = val[r:r + 1, :]
            rank = rank + jnp.where(blk > r, jnp.where(row >= val, 1.0, 0.0), jnp.where(row > val, 1.0, 0.0))
        bias = jnp.where(causal, jnp.where(rank < n_top, 0.0, NEG_INF), NEG_INF)
        bias_rows = [jnp.zeros((half, tq), F32), bias]
        if ns < half:
            bias_rows.append(jnp.zeros((half - ns, tq), F32))
        bias_rows = jnp.concatenate(bias_rows, axis=0).T.astype(BF16)
        q_sel.append(jnp.where(lane < half, q[i].reshape(G, tq, LANES), bias_rows[None]).reshape(n, LANES))

    tk = NSA_SEL_TK
    slabs = tk // LANES

    def sel_absorb(i, kb, s, carry):
        m, l, acc = carry
        m_new = jnp.maximum(m, jnp.max(s, axis=0, keepdims=True))
        alpha = jnp.exp2(m - m_new)
        p = jnp.exp2(s - m_new)
        l = alpha * l + jnp.sum(p, axis=0, keepdims=True)
        vt = jnp.concatenate([vsT_ref[0, i, kb * slabs + c] for c in range(slabs)], axis=1)
        acc = alpha * acc + _dot(vt, p.astype(BF16))
        return m_new, l, acc

    def sel_step(kb, carries, diagonal):
        rows = pl.ds(pl.multiple_of(kb * tk, tk), tk)
        s = [_dot_nt(ks_ref[0, i, rows, :], q_sel[i]) for i in groups]
        if diagonal:
            kpos = kb * tk + lax.broadcasted_iota(jnp.int32, (tk, n), 0)
            keep = kpos <= col_qpos((tk, n))
            s = [jnp.where(keep, si, NEG_INF) for si in s]
        return tuple(sel_absorb(i, kb, s[i], carries[i]) for i in groups)

    init = (jnp.full((1, n), NEG_INF, F32), jnp.zeros((1, n), F32), jnp.zeros((d, n), F32))
    kb_diag = lax.shift_right_logical(q0, tk.bit_length() - 1)
    carries = lax.fori_loop(0, kb_diag, lambda kb, c: sel_step(kb, c, False), (init,) * len(groups))
    carries = sel_step(kb_diag, carries, True)

    for i in groups:
        _, l_s, acc_s = carries[i]
        o_s = acc_s * (1.0 / l_s)
        gates = gT_ref[0, i]
        outs = []
        for g in range(G):
            cols = slice(g * tq, (g + 1) * tq)
            outs.append(gates[3 * g:3 * g + 1, :] * o_c[i][:, cols]
                        + gates[3 * g + 1:3 * g + 2, :] * o_s[:, cols]
                        + gates[3 * g + 2:3 * g + 3, :] * o_w[i][:, cols])
        o = jnp.concatenate(outs, axis=0).T
        cols = slice(i * G * d, (i + 1) * G * d)
        o_ref[0, :, cols] = (o * sg_ref[0, :, cols]).astype(BF16)


def _nsa_overlap_T(nc_pad, ns):
    nc = nc_pad - (NSA_CMP_BLOCK // NSA_CMP_STRIDE - 1)
    cs = np.arange(nc_pad)[None, :] * NSA_CMP_STRIDE
    ss = np.arange(ns)[:, None] * NSA_SEL_BLOCK
    ov = np.clip(np.minimum(cs + NSA_CMP_BLOCK, ss + NSA_SEL_BLOCK) - np.maximum(cs, ss), 0, None)
    ov = np.where(np.arange(nc_pad)[None, :] < nc, ov, 0)
    return jnp.asarray(ov / NSA_CMP_BLOCK, dtype=BF16)


def _nsa_layer(x2, B, S, norm_g, scale, shift, cos, sin, w_in, cmp_pos, w_k1, w_k2, w_v1, w_v2):
    T, D = x2.shape
    H, KV, d = NSA_HEADS, NSA_KV_HEADS, NSA_HEAD_DIM
    G = H // KV
    W = H * d
    KW = KV * d
    offs = np.cumsum([0, W] + [KW] * 6 + [3 * H, W])
    w_q, w_kc, w_vc, w_ks, w_vs, w_kw, w_vw, w_g, w_gate = (
        w_in[:, offs[i]:offs[i + 1]] for i in range(9))
    w_gT = jnp.pad(w_g.T.reshape(KV, 3 * G, D), ((0, 0), (0, 16 - 3 * G), (0, 0))).reshape(KV * 16, D)

    tm = PROJ_ROWS
    tpb = S // tm
    full = lambda shape: pl.BlockSpec(shape, lambda i: (0,) * len(shape))
    rows = lambda w: pl.BlockSpec((tm, w), lambda i: (i, 0))
    per_b = pl.BlockSpec((1, 1, D), lambda i: (i // tpb, 0, 0))
    tab = pl.BlockSpec((1, tm, LANES), lambda i: (i // tpb, i % tpb, 0))
    heads = lambda nh: pl.BlockSpec((1, nh, tm, d), lambda i: (i // tpb, 0, i % tpb, 0))
    wide = lambda nh: pl.BlockSpec((1, nh, tm, LANES), lambda i: (i // tpb, 0, i % tpb, 0))
    slab = pl.BlockSpec((1, KV, tm // LANES, d, LANES), lambda i: (i // tpb, 0, i % tpb, 0, 0))
    kv_f32 = jax.ShapeDtypeStruct((B, KV, S, d), F32)
    kv_bf16 = jax.ShapeDtypeStruct((B, KV, S, LANES), BF16)
    kv_slab = jax.ShapeDtypeStruct((B, KV, S // LANES, d, LANES), BF16)
    assert S // NSA_SEL_BLOCK <= LANES // 2, "selection-block one-hot must fit in lanes 64..127"
    q, kc, vc, ks, vsT, kw, vwT, gT, sg = pl.pallas_call(
        functools.partial(_nsa_proj_kernel, tiles_per_seq=tpb),
        out_shape=(jax.ShapeDtypeStruct((B, H, S, LANES), BF16), kv_f32, kv_f32,
                   kv_bf16, kv_slab, kv_bf16, kv_slab,
                   jax.ShapeDtypeStruct((B, KV, 16, S), F32),
                   jax.ShapeDtypeStruct((T, W), F32)),
        grid=(T // tm,),
        in_specs=[rows(D), per_b, per_b, full((1, D)), tab, tab,
                  full((D, W)), full((D, KW)), full((D, KW)), full((D, KW)), full((KW, D)),
                  full((D, KW)), full((KW, D)), full((KV * 16, D)), full((D, W))],
        out_specs=(wide(H), heads(KV), heads(KV), wide(KV), slab, wide(KV), slab,
                   pl.BlockSpec((1, KV, 16, tm), lambda i: (i // tpb, 0, 0, i % tpb)),
                   rows(W)),
        compiler_params=_params(1),
        name="nsa_proj",
    )(x2, scale[:, None, :], shift[:, None, :], norm_g[None, :], cos, sin,
      w_q.astype(BF16), w_kc.astype(BF16), w_vc.astype(BF16), w_ks.astype(BF16), w_vs.T.astype(BF16),
      w_kw.astype(BF16), w_vw.T.astype(BF16), w_gT.astype(BF16), w_gate.astype(BF16))

    n_chunks = S // NSA_CMP_STRIDE
    cw = NSA_CMP_STRIDE * d
    pe = cmp_pos.reshape(2, 1, cw)
    chunk_spec = pl.BlockSpec((1, 1, n_chunks, cw), lambda b, kv: (b, kv, 0, 0))
    full2 = lambda shape: pl.BlockSpec(shape, lambda b, kv: (0,) * len(shape))
    k_c, v_cT = pl.pallas_call(
        _nsa_compress_kernel,
        out_shape=(jax.ShapeDtypeStruct((B, KV, n_chunks, LANES), BF16),
                   jax.ShapeDtypeStruct((B, KV, d, n_chunks), BF16)),
        grid=(B, KV),
        in_specs=[chunk_spec, chunk_spec, full2((2, 1, cw)),
                  full2((2, cw, NSA_CMP_HIDDEN)), full2((NSA_CMP_HIDDEN, LANES)),
                  full2((2, cw, NSA_CMP_HIDDEN)), full2((d, NSA_CMP_HIDDEN))],
        out_specs=(pl.BlockSpec((1, 1, n_chunks, LANES), lambda b, kv: (b, kv, 0, 0)),
                   pl.BlockSpec((1, 1, d, n_chunks), lambda b, kv: (b, kv, 0, 0))),
        compiler_params=_params(2),
        name="nsa_compress",
    )(kc.reshape(B, KV, n_chunks, cw), vc.reshape(B, KV, n_chunks, cw), pe,
      w_k1.reshape(2, cw, NSA_CMP_HIDDEN).astype(BF16),
      jnp.pad(w_k2, ((0, 0), (0, LANES - d))).astype(BF16),
      w_v1.reshape(2, cw, NSA_CMP_HIDDEN).astype(BF16), w_v2.T.astype(BF16))

    tq = NSA_TQ
    ns = S // NSA_SEL_BLOCK
    ovT = _nsa_overlap_T(n_chunks, ns)
    kp = NSA_KV_PER_STEP
    whole = lambda shape: pl.BlockSpec((1, kp) + shape, lambda b, kv, i: (b, kv) + (0,) * len(shape))
    og = pl.pallas_call(
        functools.partial(_nsa_attn_kernel, n_top=min(NSA_N_SELECT, ns)),
        out_shape=jax.ShapeDtypeStruct((B, S, W), BF16),
        grid=(B, KV // kp, S // tq),
        in_specs=[pl.BlockSpec((1, kp * G, tq, LANES), lambda b, kv, i: (b, kv, i, 0)),
                  whole((n_chunks, LANES)), whole((d, n_chunks)),
                  pl.BlockSpec((ns, n_chunks), lambda b, kv, i: (0, 0)),
                  whole((S, LANES)), whole((S // LANES, d, LANES)),
                  whole((S, LANES)), whole((S // LANES, d, LANES)),
                  pl.BlockSpec((1, kp, 16, tq), lambda b, kv, i: (b, kv, 0, i)),
                  pl.BlockSpec((1, tq, kp * G * d), lambda b, kv, i: (b, i, kv))],
        out_specs=pl.BlockSpec((1, tq, kp * G * d), lambda b, kv, i: (b, i, kv)),
        compiler_params=_params(3),
        name="nsa_attn",
    )(q, k_c, v_cT, ovT, ks, vsT, kw, vwT, gT, sg.reshape(B, S, W))
    return og.reshape(T, W)


def kernel(x, c, positions, norm_g, ada_w, ada_b, mla_w_in, mla_q_norm_g, mla_kv_norm_g, mla_w_q_b, mla_w_kv_b, mla_w_out, swa_w_in, swa_sinks, swa_w_out, nsa_w_in, nsa_cmp_pos, nsa_w_cmp_k1, nsa_w_cmp_k2, nsa_w_cmp_v1, nsa_w_cmp_v2, nsa_w_out, final_norm_g):
    B, S, D = x.shape
    depth = norm_g.shape[0]
    cos, sin = _rope_tables(positions)
    mod = _ada_modulation(c, ada_w, ada_b)
    x2 = x.reshape(B * S, D)
    for i in range(depth):
        shift, scale, gate = mod[i, :, :D], mod[i, :, D:2 * D], mod[i, :, 2 * D:]
        kind, j = i % 3, i // 3
        if kind == 0:
            og = _mla_layer(x2, B, S, norm_g[i], scale, shift, cos, sin, mla_w_in[j], mla_q_norm_g[j],
                            mla_kv_norm_g[j], mla_w_q_b[j], mla_w_kv_b[j])
            w_out = mla_w_out[j]
        elif kind == 1:
            og = _swa_layer(x2, B, S, norm_g[i], scale, shift, cos, sin, swa_w_in[j], swa_sinks[j])
            w_out = swa_w_out[j]
        else:
            og = _nsa_layer(x2, B, S, norm_g[i], scale, shift, cos, sin, nsa_w_in[j], nsa_cmp_pos[j],
                            nsa_w_cmp_k1[j], nsa_w_cmp_k2[j], nsa_w_cmp_v1[j], nsa_w_cmp_v2[j])
            w_out = nsa_w_out[j]
        x2 = _out_proj(og, w_out, x2, gate, final_norm_g, S, final=(i == depth - 1))
    return x2.reshape(B, S, D)
```

```python
import functools

import jax
import jax.numpy as jnp
import numpy as np
from jax import lax
from jax.experimental import pallas as pl
from jax.experimental.pallas import tpu as pltpu

F32 = jnp.float32
BF16 = jnp.bfloat16

ROPE_THETA = 10000.0
RMS_EPS = 1e-6
NEG_INF = -1e30
LOG2_E = 1.4426950408889634

MLA_HEADS = 8
MLA_NOPE = 128
MLA_ROPE = 64
MLA_V = 128
MLA_Q_LORA = 256
MLA_KV_LORA = 128

SWA_HEADS = 16
SWA_KV_HEADS = 2
SWA_HEAD_DIM = 64
SWA_WINDOW = 128

NSA_HEADS = 16
NSA_KV_HEADS = 4
NSA_HEAD_DIM = 64
NSA_CMP_BLOCK = 32
NSA_CMP_STRIDE = 16
NSA_CMP_HIDDEN = 128
NSA_SEL_BLOCK = 64
NSA_N_SELECT = 16
NSA_WINDOW = 512
NSA_FORCE_BONUS = 1e4

LANES = 128
ROPE_HALF = 32
VMEM_LIMIT = 56 * 1024 * 1024

PROJ_ROWS = 512
OUT_ROWS = 512
MLA_TQ = 512
MLA_HEADS_PER_STEP = 4
SWA_TQ = 128
NSA_TQ = 128
NSA_SEL_TK = 512
NSA_KV_PER_STEP = 4


def _dot(a, b):
    return jnp.dot(a, b, preferred_element_type=F32)


def _dot_nt(a, b):
    return lax.dot_general(a, b, (((1,), (1,)), ((), ())), preferred_element_type=F32)


def _silu(x):
    return x / (1.0 + jnp.exp(-x))


def _sigmoid(x):
    return 1.0 / (1.0 + jnp.exp(-x))


def _rms(x, g):
    return x * lax.rsqrt(jnp.mean(x * x, axis=-1, keepdims=True) + RMS_EPS) * g


def _params(n_axes):
    return pltpu.CompilerParams(dimension_semantics=("arbitrary",) * n_axes,
                                vmem_limit_bytes=VMEM_LIMIT)


def _rope128(x, cos, sin_signed):
    lane = lax.broadcasted_iota(jnp.int32, x.shape, 1)
    lower_half = (lane & (2 * ROPE_HALF - 1)) < ROPE_HALF
    partner = jnp.where(lower_half, pltpu.roll(x, LANES - ROPE_HALF, axis=1),
                        pltpu.roll(x, ROPE_HALF, axis=1))
    return x * cos + partner * sin_signed


def _rope_table_kernel(pos_ref, freq_ref, sign_ref, cos_ref, sin_ref):
    ang = pos_ref[0] * freq_ref[...]
    cos_ref[0] = jnp.cos(ang)
    sin_ref[0] = jnp.sin(ang) * sign_ref[...]


def _rope_tables(positions):
    B, S = positions.shape
    inv_freq = ROPE_THETA ** (-jnp.arange(ROPE_HALF, dtype=F32) / ROPE_HALF)
    freq = jnp.tile(inv_freq, LANES // ROPE_HALF)[None, :]
    sign = jnp.tile(jnp.concatenate([-jnp.ones(ROPE_HALF, F32), jnp.ones(ROPE_HALF, F32)]),
                    LANES // (2 * ROPE_HALF))[None, :]
    pos = positions.astype(F32)[..., None]
    ts = min(S, 512)
    spec = pl.BlockSpec((1, ts, LANES), lambda b, s: (b, s, 0))
    return pl.pallas_call(
        _rope_table_kernel,
        out_shape=(jax.ShapeDtypeStruct((B, S, LANES), F32),) * 2,
        grid=(B, S // ts),
        in_specs=[pl.BlockSpec((1, ts, 1), lambda b, s: (b, s, 0)),
                  pl.BlockSpec((1, LANES), lambda b, s: (0, 0)),
                  pl.BlockSpec((1, LANES), lambda b, s: (0, 0))],
        out_specs=(spec, spec),
        compiler_params=_params(2),
        name="rope_tables",
    )(pos, freq, sign)


def _ada_kernel(c_ref, w_ref, b_ref, o_ref):
    cond = _silu(c_ref[...]).astype(BF16)
    o_ref[0] = _dot(cond, w_ref[0]) + b_ref[0]


def _ada_modulation(c, ada_w, ada_b):
    depth, D, D3 = ada_w.shape
    B = c.shape[0]
    Bp = -(-B // 16) * 16
    c_pad = jnp.pad(c, ((0, Bp - B), (0, 0)))
    tn = 1024
    out = pl.pallas_call(
        _ada_kernel,
        out_shape=jax.ShapeDtypeStruct((depth, Bp, D3), F32),
        grid=(depth, D3 // tn),
        in_specs=[pl.BlockSpec((Bp, D), lambda i, n: (0, 0)),
                  pl.BlockSpec((1, D, tn), lambda i, n: (i, 0, n)),
                  pl.BlockSpec((1, 1, tn), lambda i, n: (i, 0, n))],
        out_specs=pl.BlockSpec((1, Bp, tn), lambda i, n: (i, 0, n)),
        compiler_params=_params(2),
        name="ada_modulation",
    )(c_pad, ada_w.astype(BF16), ada_b[:, None, :])
    return out[:, :B]


def _out_proj_kernel(og_ref, w_ref, x_ref, gate_ref, fg_ref, o_ref, *, final):
    y = _dot(og_ref[...], w_ref[...])
    xn = x_ref[...] + gate_ref[0] * y
    if final:
        xn = _rms(xn, fg_ref[...])
    o_ref[...] = xn


def _out_proj(og, w_out, x2, gate_c, final_g, S, final):
    T, D = x2.shape
    W = og.shape[1]
    tm = OUT_ROWS
    tpb = S // tm
    return pl.pallas_call(
        functools.partial(_out_proj_kernel, final=final),
        out_shape=jax.ShapeDtypeStruct((T, D), F32),
        grid=(T // tm,),
        in_specs=[pl.BlockSpec((tm, W), lambda i: (i, 0)),
                  pl.BlockSpec((W, D), lambda i: (0, 0)),
                  pl.BlockSpec((tm, D), lambda i: (i, 0)),
                  pl.BlockSpec((1, 1, D), lambda i: (i // tpb, 0, 0)),
                  pl.BlockSpec((1, D), lambda i: (0, 0))],
        out_specs=pl.BlockSpec((tm, D), lambda i: (i, 0)),
        compiler_params=_params(1),
        name="out_proj",
    )(og, w_out.astype(BF16), x2, gate_c[:, None, :], final_g[None, :])


def _modulated_norm(x_ref, g_ref, scale_ref, shift_ref):
    y = _rms(x_ref[...], g_ref[...])
    return (y * (1.0 + scale_ref[0]) + shift_ref[0]).astype(BF16)


def _store_heads(dst_ref, val, n_heads, width, dtype):
    for h in range(n_heads):
        dst_ref[0, h] = val[:, h * width:(h + 1) * width].astype(dtype)


def _store_heads_padded(dst_ref, val, n_heads, upper=None):
    half = LANES // 2
    lane = lax.broadcasted_iota(jnp.int32, (val.shape[0], LANES), 1)
    fill = 0.0 if upper is None else upper
    for h in range(n_heads):
        chunk = val[:, (h // 2) * LANES:(h // 2 + 1) * LANES]
        if h % 2:
            chunk = pltpu.roll(chunk, half, axis=1)
        dst_ref[0, h] = jnp.where(lane < half, chunk, fill).astype(BF16)


def _store_slabs(dst_ref, val, n_heads, d):
    for h in range(n_heads):
        for c in range(val.shape[1] // LANES):
            dst_ref[0, h, c] = val[h * d:(h + 1) * d, c * LANES:(c + 1) * LANES].astype(BF16)


def _rope_wide(val, cos, sin_signed):
    return jnp.concatenate(
        [_rope128(val[:, c * LANES:(c + 1) * LANES], cos, sin_signed)
         for c in range(val.shape[1] // LANES)], axis=1)


def _mla_proj_kernel(x_ref, scale_ref, shift_ref, g_ref, cos_ref, sin_ref,
                     wqa_ref, wkva_ref, wkpe_ref, wgate_ref, qg_ref, kvg_ref,
                     wqb_ref, wkb_ref, wvbT_ref,
                     q_ref, k_ref, vT_ref, sg_ref):
    h = _modulated_norm(x_ref, g_ref, scale_ref, shift_ref)
    cos, sin = cos_ref[0], sin_ref[0]
    head_w = MLA_NOPE + LANES
    q_scale = (MLA_NOPE + MLA_ROPE) ** -0.5 * LOG2_E

    qn = _rms(_dot(h, wqa_ref[...]), qg_ref[...]).astype(BF16)
    q = _dot(qn, wqb_ref[...]) * q_scale
    for hd in range(MLA_HEADS):
        lo = hd * head_w
        q_ref[:, lo:lo + MLA_NOPE] = q[:, lo:lo + MLA_NOPE].astype(BF16)
        q_ref[:, lo + MLA_NOPE:lo + head_w] = _rope128(
            q[:, lo + MLA_NOPE:lo + head_w], cos, sin).astype(BF16)

    kvn = _rms(_dot(h, wkva_ref[...]), kvg_ref[...]).astype(BF16)
    kn = _dot(kvn, wkb_ref[...])
    kpe = _rope128(_dot(h, wkpe_ref[...]), cos, sin).astype(BF16)
    for hd in range(MLA_HEADS):
        lo = hd * head_w
        k_ref[:, lo:lo + MLA_NOPE] = kn[:, hd * MLA_NOPE:(hd + 1) * MLA_NOPE].astype(BF16)
        k_ref[:, lo + MLA_NOPE:lo + head_w] = kpe
    _store_slabs(vT_ref, _dot_nt(wvbT_ref[...], kvn), MLA_HEADS, MLA_V)

    sg_ref[...] = _silu(_dot(h, wgate_ref[...]))


def _mla_attn_kernel(q_ref, k_ref, vT_ref, sg_ref, o_ref, s_ref, m_ref, l_ref, acc_ref):
    qi = pl.program_id(2)
    tq = tk = MLA_TQ
    slabs = tk // LANES
    head_w = MLA_NOPE + LANES
    heads = range(MLA_HEADS_PER_STEP)

    def scores(slot, kb):
        rows = pl.ds(pl.multiple_of(kb * tk, tk), tk)
        for h in heads:
            cols = slice(h * head_w, (h + 1) * head_w)
            s_ref[slot, h] = _dot_nt(k_ref[0, rows, cols], q_ref[0, :, cols])

    def absorb(slot, kb, masked):
        if masked:
            kpos = lax.broadcasted_iota(jnp.int32, (tk, tq), 0)
            qpos = lax.broadcasted_iota(jnp.int32, (tk, tq), 1)
            visible = kpos <= qpos
        for h in heads:
            s = s_ref[slot, h]
            if masked:
                s = jnp.where(visible, s, NEG_INF)
            m = m_ref[h]
            m_new = jnp.maximum(m, jnp.max(s, axis=0, keepdims=True))
            alpha = jnp.exp2(m - m_new)
            p = jnp.exp2(s - m_new)
            m_ref[h] = m_new
            l_ref[h] = alpha * l_ref[h] + jnp.sum(p, axis=0, keepdims=True)
            vt = jnp.concatenate([vT_ref[0, h, kb * slabs + c] for c in range(slabs)], axis=1)
            acc_ref[h] = alpha * acc_ref[h] + _dot(vt, p.astype(BF16))

    m_ref[...] = jnp.full(m_ref.shape, NEG_INF, F32)
    l_ref[...] = jnp.zeros(l_ref.shape, F32)
    acc_ref[...] = jnp.zeros(acc_ref.shape, F32)
    scores(0, 0)

    def pair(j, carry):
        kb = 2 * j
        scores(1, kb + 1)
        absorb(0, kb, False)
        scores(0, kb + 2)
        absorb(1, kb + 1, False)
        return carry

    lax.fori_loop(0, lax.shift_right_logical(qi, 1), pair, 0)

    @pl.when((qi & 1) == 1)
    def _():
        scores(1, qi)
        absorb(0, qi - 1, False)
        absorb(1, qi, True)

    @pl.when((qi & 1) == 0)
    def _():
        absorb(0, qi, True)

    for h in heads:
        cols = slice(h * MLA_V, (h + 1) * MLA_V)
        o = (acc_ref[h] * (1.0 / l_ref[h])).T
        o_ref[0, :, cols] = (o * sg_ref[0, :, cols]).astype(BF16)


def _mla_layer(x2, B, S, norm_g, scale, shift, cos, sin, w_in, q_norm_g, kv_norm_g, w_q_b, w_kv_b):
    T, D = x2.shape
    H = MLA_HEADS
    head_w = MLA_NOPE + LANES
    o1 = MLA_Q_LORA
    o2 = o1 + MLA_KV_LORA
    o3 = o2 + MLA_ROPE
    w_qa, w_kva, w_kpe, w_gate = (w_in[:, :o1], w_in[:, o1:o2], w_in[:, o2:o3], w_in[:, o3:])
    w_kpe = jnp.pad(w_kpe, ((0, 0), (0, LANES - MLA_ROPE)))
    wq = w_q_b.reshape(MLA_Q_LORA, H, MLA_NOPE + MLA_ROPE)
    wq = jnp.pad(wq, ((0, 0), (0, 0), (0, head_w - MLA_NOPE - MLA_ROPE))).reshape(MLA_Q_LORA, H * head_w)
    wkv = w_kv_b.reshape(MLA_KV_LORA, H, MLA_NOPE + MLA_V)
    w_kb = wkv[:, :, :MLA_NOPE].reshape(MLA_KV_LORA, H * MLA_NOPE)
    w_vbT = wkv[:, :, MLA_NOPE:].reshape(MLA_KV_LORA, H * MLA_V).T
    W = H * MLA_V

    tm = PROJ_ROWS
    tpb = S // tm
    full = lambda shape: pl.BlockSpec(shape, lambda i: (0,) * len(shape))
    rows = lambda w: pl.BlockSpec((tm, w), lambda i: (i, 0))
    per_b = pl.BlockSpec((1, 1, D), lambda i: (i // tpb, 0, 0))
    tab = pl.BlockSpec((1, tm, LANES), lambda i: (i // tpb, i % tpb, 0))
    q, k, vT, sg = pl.pallas_call(
        _mla_proj_kernel,
        out_shape=(jax.ShapeDtypeStruct((T, H * head_w), BF16),
                   jax.ShapeDtypeStruct((T, H * head_w), BF16),
                   jax.ShapeDtypeStruct((B, H, S // LANES, MLA_V, LANES), BF16),
                   jax.ShapeDtypeStruct((T, W), F32)),
        grid=(T // tm,),
        in_specs=[rows(D), per_b, per_b, full((1, D)), tab, tab,
                  full((D, o1)), full((D, MLA_KV_LORA)), full((D, LANES)), full((D, W)),
                  full((1, o1)), full((1, MLA_KV_LORA)),
                  full((o1, H * head_w)), full((MLA_KV_LORA, H * MLA_NOPE)), full((W, MLA_KV_LORA))],
        out_specs=(rows(H * head_w), rows(H * head_w),
                   pl.BlockSpec((1, H, tm // LANES, MLA_V, LANES),
                                lambda i: (i // tpb, 0, i % tpb, 0, 0)),
                   rows(W)),
        compiler_params=_params(1),
        name="mla_proj",
    )(x2, scale[:, None, :], shift[:, None, :], norm_g[None, :], cos, sin,
      w_qa.astype(BF16), w_kva.astype(BF16), w_kpe.astype(BF16), w_gate.astype(BF16),
      q_norm_g[None, :], kv_norm_g[None, :],
      wq.astype(BF16), w_kb.astype(BF16), w_vbT.astype(BF16))

    tq = MLA_TQ
    hp = MLA_HEADS_PER_STEP
    q3 = q.reshape(B, S, H * head_w)
    k3 = k.reshape(B, S, H * head_w)
    sg3 = sg.reshape(B, S, W)
    og = pl.pallas_call(
        _mla_attn_kernel,
        out_shape=jax.ShapeDtypeStruct((B, S, W), BF16),
        grid=(B, H // hp, S // tq),
        in_specs=[pl.BlockSpec((1, tq, hp * head_w), lambda b, h, i: (b, i, h)),
                  pl.BlockSpec((1, S, hp * head_w), lambda b, h, i: (b, 0, h)),
                  pl.BlockSpec((1, hp, S // LANES, MLA_V, LANES), lambda b, h, i: (b, h, 0, 0, 0)),
                  pl.BlockSpec((1, tq, hp * MLA_V), lambda b, h, i: (b, i, h))],
        out_specs=pl.BlockSpec((1, tq, hp * MLA_V), lambda b, h, i: (b, i, h)),
        scratch_shapes=[pltpu.VMEM((2, hp, tq, tq), F32),
                        pltpu.VMEM((hp, 1, tq), F32), pltpu.VMEM((hp, 1, tq), F32),
                        pltpu.VMEM((hp, MLA_V, tq), F32)],
        compiler_params=_params(3),
        name="mla_attn",
    )(q3, k3, vT, sg3)
    return og.reshape(T, W)


def _swa_proj_kernel(x_ref, scale_ref, shift_ref, g_ref, cos_ref, sin_ref,
                     wq_ref, wk_ref, wvT_ref, wgate_ref,
                     q_ref, k_ref, vT_ref, sg_ref):
    h = _modulated_norm(x_ref, g_ref, scale_ref, shift_ref)
    cos, sin = cos_ref[0], sin_ref[0]
    q = _rope_wide(_dot(h, wq_ref[...]) * (SWA_HEAD_DIM ** -0.5 * LOG2_E), cos, sin)
    _store_heads(q_ref, q, SWA_HEADS, SWA_HEAD_DIM, BF16)
    k = _rope_wide(_dot(h, wk_ref[...]), cos, sin)
    _store_heads(k_ref, k, SWA_KV_HEADS, SWA_HEAD_DIM, BF16)
    _store_slabs(vT_ref, _dot_nt(wvT_ref[...], h), SWA_KV_HEADS, SWA_HEAD_DIM)
    sg_ref[...] = _silu(_dot(h, wgate_ref[...]))


def _band_valid(diff, window):
    return lax.bitcast_convert_type(diff, jnp.uint32) < jnp.uint32(window)


def _swa_attn_kernel(q_ref, k_ref, vT_ref, sink_ref, sg_ref, o_ref):
    qi = pl.program_id(1)
    tq = SWA_TQ
    G = SWA_HEADS // SWA_KV_HEADS
    d = SWA_HEAD_DIM
    n = G * tq
    span = SWA_WINDOW + tq
    groups = range(SWA_KV_HEADS)
    blk0 = jnp.maximum(qi - SWA_WINDOW // tq, 0)
    base = pl.multiple_of(blk0 * LANES, LANES)
    s_all = [_dot_nt(k_ref[0, i, pl.ds(base, span), :], q_ref[0, i * G:(i + 1) * G].reshape(n, d))
             for i in groups]
    kpos = base + lax.broadcasted_iota(jnp.int32, (span, n), 0)
    qpos = qi * tq + (lax.broadcasted_iota(jnp.int32, (span, n), 1) & (tq - 1))
    in_band = _band_valid(qpos - kpos, SWA_WINDOW)
    for i in groups:
        s = jnp.where(in_band, s_all[i], NEG_INF)
        sink = sink_ref[i]
        m = jnp.maximum(jnp.max(s, axis=0, keepdims=True), sink)
        e = jnp.exp2(s - m)
        denom = jnp.sum(e, axis=0, keepdims=True) + jnp.exp2(sink - m)
        vt = jnp.concatenate([vT_ref[0, i, blk0 + c] for c in range(span // LANES)], axis=1)
        oT = _dot(vt, e.astype(BF16)) * (1.0 / denom)
        o = jnp.concatenate([oT[:, g * tq:(g + 1) * tq] for g in range(G)], axis=0).T
        cols = slice(i * G * d, (i + 1) * G * d)
        o_ref[0, :, cols] = (o * sg_ref[0, :, cols]).astype(BF16)


def _swa_layer(x2, B, S, norm_g, scale, shift, cos, sin, w_in, sinks):
    T, D = x2.shape
    H, KV, d = SWA_HEADS, SWA_KV_HEADS, SWA_HEAD_DIM
    G = H // KV
    W = H * d
    KW = KV * d
    w_q, w_k, w_v, w_gate = (w_in[:, :W], w_in[:, W:W + KW], w_in[:, W + KW:W + 2 * KW],
                             w_in[:, W + 2 * KW:])
    tm = PROJ_ROWS
    tpb = S // tm
    full = lambda shape: pl.BlockSpec(shape, lambda i: (0,) * len(shape))
    rows = lambda w: pl.BlockSpec((tm, w), lambda i: (i, 0))
    per_b = pl.BlockSpec((1, 1, D), lambda i: (i // tpb, 0, 0))
    tab = pl.BlockSpec((1, tm, LANES), lambda i: (i // tpb, i % tpb, 0))
    heads = lambda nh: pl.BlockSpec((1, nh, tm, d), lambda i: (i // tpb, 0, i % tpb, 0))
    q, k, vT, sg = pl.pallas_call(
        _swa_proj_kernel,
        out_shape=(jax.ShapeDtypeStruct((B, H, S, d), BF16),
                   jax.ShapeDtypeStruct((B, KV, S, d), BF16),
                   jax.ShapeDtypeStruct((B, KV, S // LANES, d, LANES), BF16),
                   jax.ShapeDtypeStruct((T, W), F32)),
        grid=(T // tm,),
        in_specs=[rows(D), per_b, per_b, full((1, D)), tab, tab,
                  full((D, W)), full((D, KW)), full((KW, D)), full((D, W))],
        out_specs=(heads(H), heads(KV),
                   pl.BlockSpec((1, KV, tm // LANES, d, LANES), lambda i: (i // tpb, 0, i % tpb, 0, 0)),
                   rows(W)),
        compiler_params=_params(1),
        name="swa_proj",
    )(x2, scale[:, None, :], shift[:, None, :], norm_g[None, :], cos, sin,
      w_q.astype(BF16), w_k.astype(BF16), w_v.T.astype(BF16), w_gate.astype(BF16))

    tq = SWA_TQ
    n = G * tq
    sink_rows = jnp.repeat(sinks.astype(F32).reshape(KV, G) * LOG2_E, tq, axis=1)[:, None, :]
    og = pl.pallas_call(
        _swa_attn_kernel,
        out_shape=jax.ShapeDtypeStruct((B, S, W), BF16),
        grid=(B, S // tq),
        in_specs=[pl.BlockSpec((1, H, tq, d), lambda b, i: (b, 0, i, 0)),
                  pl.BlockSpec((1, KV, S, d), lambda b, i: (b, 0, 0, 0)),
                  pl.BlockSpec((1, KV, S // LANES, d, LANES), lambda b, i: (b, 0, 0, 0, 0)),
                  pl.BlockSpec((KV, 1, n), lambda b, i: (0, 0, 0)),
                  pl.BlockSpec((1, tq, W), lambda b, i: (b, i, 0))],
        out_specs=pl.BlockSpec((1, tq, W), lambda b, i: (b, i, 0)),
        compiler_params=_params(2),
        name="swa_attn",
    )(q, k, vT, sink_rows, sg.reshape(B, S, W))
    return og.reshape(T, W)


def _nsa_proj_kernel(x_ref, scale_ref, shift_ref, g_ref, cos_ref, sin_ref,
                     wq_ref, wkc_ref, wvc_ref, wks_ref, wvsT_ref, wkw_ref, wvwT_ref, wgT_ref, wgate_ref,
                     q_ref, kc_ref, vc_ref, ks_ref, vsT_ref, kw_ref, vwT_ref, gT_ref, sg_ref, *, tiles_per_seq):
    h = _modulated_norm(x_ref, g_ref, scale_ref, shift_ref)
    cos, sin = cos_ref[0], sin_ref[0]
    KV, d = NSA_KV_HEADS, NSA_HEAD_DIM
    tm = h.shape[0]
    q = _rope_wide(_dot(h, wq_ref[...]) * (d ** -0.5 * LOG2_E), cos, sin)
    _store_heads_padded(q_ref, q, NSA_HEADS)
    _store_heads(kc_ref, _dot(h, wkc_ref[...]), KV, d, F32)
    _store_heads(vc_ref, _dot(h, wvc_ref[...]), KV, d, F32)
    pos = (pl.program_id(0) % tiles_per_seq) * tm + lax.broadcasted_iota(jnp.int32, (tm, LANES), 0)
    lane = lax.broadcasted_iota(jnp.int32, (tm, LANES), 1)
    sel_shift = NSA_SEL_BLOCK.bit_length() - 1
    block_onehot = jnp.where(lane - LANES // 2 == lax.shift_right_logical(pos, sel_shift), 1.0, 0.0)
    _store_heads_padded(ks_ref, _rope_wide(_dot(h, wks_ref[...]), cos, sin), KV, upper=block_onehot)
    _store_heads_padded(kw_ref, _rope_wide(_dot(h, wkw_ref[...]), cos, sin), KV)
    _store_slabs(vsT_ref, _dot_nt(wvsT_ref[...], h), KV, d)
    _store_slabs(vwT_ref, _dot_nt(wvwT_ref[...], h), KV, d)
    gT = _sigmoid(_dot_nt(wgT_ref[...], h))
    for kv in range(KV):
        gT_ref[0, kv] = gT[kv * 16:(kv + 1) * 16, :]
    sg_ref[...] = _silu(_dot(h, wgate_ref[...]))


def _nsa_compress_kernel(kc_ref, vc_ref, pe_ref, wk1_ref, wk2_ref, wv1_ref, wv2T_ref, kout_ref, vT_ref):
    pe_top, pe_bot = pe_ref[0], pe_ref[1]

    def hidden(chunks, w1_ref):
        top = _dot((chunks + pe_top).astype(BF16), w1_ref[0])
        bot = _dot((chunks + pe_bot).astype(BF16), w1_ref[1])
        n = bot.shape[0]
        return _silu(top + pltpu.roll(bot, n - 1, axis=0)).astype(BF16)

    hk = hidden(kc_ref[0, 0], wk1_ref)
    kc = _dot(hk, wk2_ref[...])
    row = lax.broadcasted_iota(jnp.int32, kc.shape, 0)
    kout_ref[0, 0] = jnp.where(row < kc.shape[0] - 1, kc, 0.0).astype(BF16)
    hv = hidden(vc_ref[0, 0], wv1_ref)
    vT = _dot_nt(wv2T_ref[...], hv)
    col = lax.broadcasted_iota(jnp.int32, vT.shape, 1)
    vT_ref[0, 0] = jnp.where(col < vT.shape[1] - 1, vT, 0.0).astype(BF16)


def _nsa_attn_kernel(q_ref, kc_ref, vcT_ref, ovT_ref, ks_ref, vsT_ref, kw_ref, vwT_ref, gT_ref, sg_ref,
                     o_ref, qsel_ref, s_ref, m_ref, l_ref, acc_ref, *, n_top):
    qi = pl.program_id(2)
    tq = NSA_TQ
    G = NSA_HEADS // NSA_KV_HEADS
    d = NSA_HEAD_DIM
    n = G * tq
    groups = range(NSA_KV_PER_STEP)
    q0 = qi * tq
    q = [q_ref[0, i * G:(i + 1) * G].reshape(n, LANES) for i in groups]

    def col_qpos(shape):
        return q0 + (lax.broadcasted_iota(jnp.int32, shape, 1) & (tq - 1))

    span = NSA_WINDOW + tq
    blk0 = jnp.maximum(qi * (tq // LANES) - NSA_WINDOW // LANES, 0)
    base = pl.multiple_of(blk0 * LANES, LANES)
    s_cmp = [_dot_nt(kc_ref[0, i], q[i]) for i in groups]
    s_win = [_dot_nt(kw_ref[0, i, pl.ds(base, span), :], q[i]) for i in groups]

    cmp_end = lax.broadcasted_iota(jnp.int32, s_cmp[0].shape, 0) * NSA_CMP_STRIDE + (NSA_CMP_BLOCK - 1)
    valid = cmp_end <= col_qpos(s_cmp[0].shape)
    o_c, imp = [], []
    for i in groups:
        s = jnp.where(valid, s_cmp[i], NEG_INF)
        m = jnp.max(s, axis=0, keepdims=True)
        e = jnp.where(valid, jnp.exp2(s - m), 0.0)
        l = jnp.sum(e, axis=0, keepdims=True)
        p = (e * jnp.where(l > 0.0, 1.0 / l, 0.0)).astype(BF16)
        o_c.append(_dot(vcT_ref[0, i], p))
        imp_all = _dot(ovT_ref[...], p)
        acc = imp_all[:, 0:tq]
        for g in range(1, G):
            acc = acc + imp_all[:, g * tq:(g + 1) * tq]
        imp.append(acc)

    ns = imp[0].shape[0]
    blk = lax.broadcasted_iota(jnp.int32, imp[0].shape, 0)
    sel_shift = NSA_SEL_BLOCK.bit_length() - 1
    q_blk = lax.shift_right_logical(q0 + lax.broadcasted_iota(jnp.int32, imp[0].shape, 1), sel_shift)
    causal = blk <= q_blk
    forced = (blk == 0) | (blk == q_blk) | (blk == q_blk - 1)
    half = LANES // 2
    lane = lax.broadcasted_iota(jnp.int32, (G, tq, LANES), 2)
    for i in groups:
        val = jnp.where(causal, imp[i] + jnp.where(forced, NSA_FORCE_BONUS, 0.0), -1.0)
        rank = jnp.zeros(val.shape, F32)
        for r in range(ns):
            row = val[r:r + 1, :]
            rank = rank + jnp.where(blk > r, jnp.where(row >= val, 1.0, 0.0), jnp.where(row > val, 1.0, 0.0))
        bias = jnp.where(causal, jnp.where(rank < n_top, 0.0, NEG_INF), NEG_INF)
        bias_rows = [jnp.zeros((half, tq), F32), bias]
        if ns < half:
            bias_rows.append(jnp.zeros((half - ns, tq), F32))
        bias_rows = jnp.concatenate(bias_rows, axis=0).T.astype(BF16)
        qsel_ref[i] = jnp.where(lane < half, q[i].reshape(G, tq, LANES), bias_rows[None]).reshape(n, LANES)

    tk = NSA_SEL_TK
    slabs = tk // LANES

    def sel_scores(slot, kb):
        rows = pl.ds(pl.multiple_of(kb * tk, tk), tk)
        for i in groups:
            s_ref[slot, i] = _dot_nt(ks_ref[0, i, rows, :], qsel_ref[i])

    def sel_absorb(slot, kb, diagonal):
        if diagonal:
            kpos = kb * tk + lax.broadcasted_iota(jnp.int32, (tk, n), 0)
            keep = kpos <= col_qpos((tk, n))
        for i in groups:
            s = s_ref[slot, i]
            if diagonal:
                s = jnp.where(keep, s, NEG_INF)
            m = m_ref[i]
            m_new = jnp.maximum(m, jnp.max(s, axis=0, keepdims=True))
            alpha = jnp.exp2(m - m_new)
            p = jnp.exp2(s - m_new)
            m_ref[i] = m_new
            l_ref[i] = alpha * l_ref[i] + jnp.sum(p, axis=0, keepdims=True)
            vt = jnp.concatenate([vsT_ref[0, i, kb * slabs + c] for c in range(slabs)], axis=1)
            acc_ref[i] = alpha * acc_ref[i] + _dot(vt, p.astype(BF16))

    m_ref[...] = jnp.full(m_ref.shape, NEG_INF, F32)
    l_ref[...] = jnp.zeros(l_ref.shape, F32)
    acc_ref[...] = jnp.zeros(acc_ref.shape, F32)
    kb_diag = lax.shift_right_logical(q0, tk.bit_length() - 1)
    sel_scores(0, 0)

    diff = col_qpos(s_win[0].shape) - (base + lax.broadcasted_iota(jnp.int32, s_win[0].shape, 0))
    in_band = _band_valid(diff, NSA_WINDOW)
    o_w = []
    for i in groups:
        s = jnp.where(in_band, s_win[i], NEG_INF)
        e = jnp.exp2(s - jnp.max(s, axis=0, keepdims=True))
        vt = jnp.concatenate([vwT_ref[0, i, blk0 + c] for c in range(span // LANES)], axis=1)
        o_w.append(_dot(vt, e.astype(BF16)) * (1.0 / jnp.sum(e, axis=0, keepdims=True)))

    def sel_pair(j, carry):
        kb = 2 * j
        sel_scores(1, kb + 1)
        sel_absorb(0, kb, False)
        sel_scores(0, kb + 2)
        sel_absorb(1, kb + 1, False)
        return carry

    lax.fori_loop(0, lax.shift_right_logical(kb_diag, 1), sel_pair, 0)

    @pl.when((kb_diag & 1) == 1)
    def _():
        sel_scores(1, kb_diag)
        sel_absorb(0, kb_diag - 1, False)
        sel_absorb(1, kb_diag, True)

    @pl.when((kb_diag & 1) == 0)
    def _():
        sel_absorb(0, kb_diag, True)

    for i in groups:
        o_s = acc_ref[i] * (1.0 / l_ref[i])
        gates = gT_ref[0, i]
        outs = []
        for g in range(G):
            cols = slice(g * tq, (g + 1) * tq)
            outs.append(gates[3 * g:3 * g + 1, :] * o_c[i][:, cols]
                        + gates[3 * g + 1:3 * g + 2, :] * o_s[:, cols]
                        + gates[3 * g + 2:3 * g + 3, :] * o_w[i][:, cols])
        o = jnp.concatenate(outs, axis=0).T
        cols = slice(i * G * d, (i + 1) * G * d)
        o_ref[0, :, cols] = (o * sg_ref[0, :, cols]).astype(BF16)


def _nsa_overlap_T(nc_pad, ns):
    nc = nc_pad - (NSA_CMP_BLOCK // NSA_CMP_STRIDE - 1)
    cs = np.arange(nc_pad)[None, :] * NSA_CMP_STRIDE
    ss = np.arange(ns)[:, None] * NSA_SEL_BLOCK
    ov = np.clip(np.minimum(cs + NSA_CMP_BLOCK, ss + NSA_SEL_BLOCK) - np.maximum(cs, ss), 0, None)
    ov = np.where(np.arange(nc_pad)[None, :] < nc, ov, 0)
    return jnp.asarray(ov / NSA_CMP_BLOCK, dtype=BF16)


def _nsa_layer(x2, B, S, norm_g, scale, shift, cos, sin, w_in, cmp_pos, w_k1, w_k2, w_v1, w_v2):
    T, D = x2.shape
    H, KV, d = NSA_HEADS, NSA_KV_HEADS, NSA_HEAD_DIM
    G = H // KV
    W = H * d
    KW = KV * d
    offs = np.cumsum([0, W] + [KW] * 6 + [3 * H, W])
    w_q, w_kc, w_vc, w_ks, w_vs, w_kw, w_vw, w_g, w_gate = (
        w_in[:, offs[i]:offs[i + 1]] for i in range(9))
    w_gT = jnp.pad(w_g.T.reshape(KV, 3 * G, D), ((0, 0), (0, 16 - 3 * G), (0, 0))).reshape(KV * 16, D)

    tm = PROJ_ROWS
    tpb = S // tm
    full = lambda shape: pl.BlockSpec(shape, lambda i: (0,) * len(shape))
    rows = lambda w: pl.BlockSpec((tm, w), lambda i: (i, 0))
    per_b = pl.BlockSpec((1, 1, D), lambda i: (i // tpb, 0, 0))
    tab = pl.BlockSpec((1, tm, LANES), lambda i: (i // tpb, i % tpb, 0))
    heads = lambda nh: pl.BlockSpec((1, nh, tm, d), lambda i: (i // tpb, 0, i % tpb, 0))
    wide = lambda nh: pl.BlockSpec((1, nh, tm, LANES), lambda i: (i // tpb, 0, i % tpb, 0))
    slab = pl.BlockSpec((1, KV, tm // LANES, d, LANES), lambda i: (i // tpb, 0, i % tpb, 0, 0))
    kv_f32 = jax.ShapeDtypeStruct((B, KV, S, d), F32)
    kv_bf16 = jax.ShapeDtypeStruct((B, KV, S, LANES), BF16)
    kv_slab = jax.ShapeDtypeStruct((B, KV, S // LANES, d, LANES), BF16)
    assert S // NSA_SEL_BLOCK <= LANES // 2, "selection-block one-hot must fit in lanes 64..127"
    q, kc, vc, ks, vsT, kw, vwT, gT, sg = pl.pallas_call(
        functools.partial(_nsa_proj_kernel, tiles_per_seq=tpb),
        out_shape=(jax.ShapeDtypeStruct((B, H, S, LANES), BF16), kv_f32, kv_f32,
                   kv_bf16, kv_slab, kv_bf16, kv_slab,
                   jax.ShapeDtypeStruct((B, KV, 16, S), F32),
                   jax.ShapeDtypeStruct((T, W), F32)),
        grid=(T // tm,),
        in_specs=[rows(D), per_b, per_b, full((1, D)), tab, tab,
                  full((D, W)), full((D, KW)), full((D, KW)), full((D, KW)), full((KW, D)),
                  full((D, KW)), full((KW, D)), full((KV * 16, D)), full((D, W))],
        out_specs=(wide(H), heads(KV), heads(KV), wide(KV), slab, wide(KV), slab,
                   pl.BlockSpec((1, KV, 16, tm), lambda i: (i // tpb, 0, 0, i % tpb)),
                   rows(W)),
        compiler_params=_params(1),
        name="nsa_proj",
    )(x2, scale[:, None, :], shift[:, None, :], norm_g[None, :], cos, sin,
      w_q.astype(BF16), w_kc.astype(BF16), w_vc.astype(BF16), w_ks.astype(BF16), w_vs.T.astype(BF16),
      w_kw.astype(BF16), w_vw.T.astype(BF16), w_gT.astype(BF16), w_gate.astype(BF16))

    n_chunks = S // NSA_CMP_STRIDE
    cw = NSA_CMP_STRIDE * d
    pe = cmp_pos.reshape(2, 1, cw)
    chunk_spec = pl.BlockSpec((1, 1, n_chunks, cw), lambda b, kv: (b, kv, 0, 0))
    full2 = lambda shape: pl.BlockSpec(shape, lambda b, kv: (0,) * len(shape))
    k_c, v_cT = pl.pallas_call(
        _nsa_compress_kernel,
        out_shape=(jax.ShapeDtypeStruct((B, KV, n_chunks, LANES), BF16),
                   jax.ShapeDtypeStruct((B, KV, d, n_chunks), BF16)),
        grid=(B, KV),
        in_specs=[chunk_spec, chunk_spec, full2((2, 1, cw)),
                  full2((2, cw, NSA_CMP_HIDDEN)), full2((NSA_CMP_HIDDEN, LANES)),
                  full2((2, cw, NSA_CMP_HIDDEN)), full2((d, NSA_CMP_HIDDEN))],
        out_specs=(pl.BlockSpec((1, 1, n_chunks, LANES), lambda b, kv: (b, kv, 0, 0)),
                   pl.BlockSpec((1, 1, d, n_chunks), lambda b, kv: (b, kv, 0, 0))),
        compiler_params=_params(2),
        name="nsa_compress",
    )(kc.reshape(B, KV, n_chunks, cw), vc.reshape(B, KV, n_chunks, cw), pe,
      w_k1.reshape(2, cw, NSA_CMP_HIDDEN).astype(BF16),
      jnp.pad(w_k2, ((0, 0), (0, LANES - d))).astype(BF16),
      w_v1.reshape(2, cw, NSA_CMP_HIDDEN).astype(BF16), w_v2.T.astype(BF16))

    tq = NSA_TQ
    ns = S // NSA_SEL_BLOCK
    ovT = _nsa_overlap_T(n_chunks, ns)
    kp = NSA_KV_PER_STEP
    whole = lambda shape: pl.BlockSpec((1, kp) + shape, lambda b, kv, i: (b, kv) + (0,) * len(shape))
    og = pl.pallas_call(
        functools.partial(_nsa_attn_kernel, n_top=min(NSA_N_SELECT, ns)),
        out_shape=jax.ShapeDtypeStruct((B, S, W), BF16),
        grid=(B, KV // kp, S // tq),
        in_specs=[pl.BlockSpec((1, kp * G, tq, LANES), lambda b, kv, i: (b, kv, i, 0)),
                  whole((n_chunks, LANES)), whole((d, n_chunks)),
                  pl.BlockSpec((ns, n_chunks), lambda b, kv, i: (0, 0)),
                  whole((S, LANES)), whole((S // LANES, d, LANES)),
                  whole((S, LANES)), whole((S // LANES, d, LANES)),
                  pl.BlockSpec((1, kp, 16, tq), lambda b, kv, i: (b, kv, 0, i)),
                  pl.BlockSpec((1, tq, kp * G * d), lambda b, kv, i: (b, i, kv))],
        out_specs=pl.BlockSpec((1, tq, kp * G * d), lambda b, kv, i: (b, i, kv)),
        scratch_shapes=[pltpu.VMEM((kp, G * tq, LANES), BF16),
                        pltpu.VMEM((2, kp, NSA_SEL_TK, G * tq), F32),
                        pltpu.VMEM((kp, 1, G * tq), F32), pltpu.VMEM((kp, 1, G * tq), F32),
                        pltpu.VMEM((kp, d, G * tq), F32)],
        compiler_params=_params(3),
        name="nsa_attn",
    )(q, k_c, v_cT, ovT, ks, vsT, kw, vwT, gT, sg.reshape(B, S, W))
    return og.reshape(T, W)


def kernel(x, c, positions, norm_g, ada_w, ada_b, mla_w_in, mla_q_norm_g, mla_kv_norm_g, mla_w_q_b, mla_w_kv_b, mla_w_out, swa_w_in, swa_sinks, swa_w_out, nsa_w_in, nsa_cmp_pos, nsa_w_cmp_k1, nsa_w_cmp_k2, nsa_w_cmp_v1, nsa_w_cmp_v2, nsa_w_out, final_norm_g):
    B, S, D = x.shape
    depth = norm_g.shape[0]
    cos, sin = _rope_tables(positions)
    mod = _ada_modulation(c, ada_w, ada_b)
    x2 = x.reshape(B * S, D)
    for i in range(depth):
        shift, scale, gate = mod[i, :, :D], mod[i, :, D:2 * D], mod[i, :, 2 * D:]
        kind, j = i % 3, i // 3
        if kind == 0:
            og = _mla_layer(x2, B, S, norm_g[i], scale, shift, cos, sin, mla_w_in[j], mla_q_norm_g[j],
                            mla_kv_norm_g[j], mla_w_q_b[j], mla_w_kv_b[j])
            w_out = mla_w_out[j]
        elif kind == 1:
            og = _swa_layer(x2, B, S, norm_g[i], scale, shift, cos, sin, swa_w_in[j], swa_sinks[j])
            w_out = swa_w_out[j]
        else:
            og = _nsa_layer(x2, B, S, norm_g[i], scale, shift, cos, sin, nsa_w_in[j], nsa_cmp_pos[j],
                            nsa_w_cmp_k1[j], nsa_w_cmp_k2[j], nsa_w_cmp_v1[j], nsa_w_cmp_v2[j])
            w_out = nsa_w_out[j]
        x2 = _out_proj(og, w_out, x2, gate, final_norm_g, S, final=(i == depth - 1))
    return x2.reshape(B, S, D)
```

```python
import functools

import jax
import jax.numpy as jnp
import numpy as np
from jax import lax
from jax.experimental import pallas as pl
from jax.experimental.pallas import tpu as pltpu

F32 = jnp.float32
BF16 = jnp.bfloat16

ROPE_THETA = 10000.0
RMS_EPS = 1e-6
NEG_INF = -1e30
LOG2_E = 1.4426950408889634

MLA_HEADS = 8
MLA_NOPE = 128
MLA_ROPE = 64
MLA_V = 128
MLA_Q_LORA = 256
MLA_KV_LORA = 128

SWA_HEADS = 16
SWA_KV_HEADS = 2
SWA_HEAD_DIM = 64
SWA_WINDOW = 128

NSA_HEADS = 16
NSA_KV_HEADS = 4
NSA_HEAD_DIM = 64
NSA_CMP_BLOCK = 32
NSA_CMP_STRIDE = 16
NSA_CMP_HIDDEN = 128
NSA_SEL_BLOCK = 64
NSA_N_SELECT = 16
NSA_WINDOW = 512
NSA_FORCE_BONUS = 1e4

LANES = 128
ROPE_HALF = 32
VMEM_LIMIT = 56 * 1024 * 1024

PROJ_ROWS = 512
OUT_ROWS = 512
MLA_TQ = 512
MLA_HEADS_PER_STEP = 4
SWA_TQ = 128
NSA_TQ = 128
NSA_SEL_TK = 512
NSA_KV_PER_STEP = 4


def _dot(a, b):
    return jnp.dot(a, b, preferred_element_type=F32)


def _dot_nt(a, b):
    return lax.dot_general(a, b, (((1,), (1,)), ((), ())), preferred_element_type=F32)


def _silu(x):
    return x / (1.0 + jnp.exp(-x))


def _sigmoid(x):
    return 1.0 / (1.0 + jnp.exp(-x))


def _rms(x, g):
    return x * lax.rsqrt(jnp.mean(x * x, axis=-1, keepdims=True) + RMS_EPS) * g


def _params(n_axes):
    return pltpu.CompilerParams(dimension_semantics=("arbitrary",) * n_axes,
                                vmem_limit_bytes=VMEM_LIMIT)


def _rope128(x, cos, sin_signed):
    lane = lax.broadcasted_iota(jnp.int32, x.shape, 1)
    lower_half = (lane & (2 * ROPE_HALF - 1)) < ROPE_HALF
    partner = jnp.where(lower_half, pltpu.roll(x, LANES - ROPE_HALF, axis=1),
                        pltpu.roll(x, ROPE_HALF, axis=1))
    return x * cos + partner * sin_signed


def _rope_table_kernel(pos_ref, freq_ref, sign_ref, cos_ref, sin_ref):
    ang = pos_ref[0] * freq_ref[...]
    cos_ref[0] = jnp.cos(ang)
    sin_ref[0] = jnp.sin(ang) * sign_ref[...]


def _rope_tables(positions):
    B, S = positions.shape
    inv_freq = ROPE_THETA ** (-jnp.arange(ROPE_HALF, dtype=F32) / ROPE_HALF)
    freq = jnp.tile(inv_freq, LANES // ROPE_HALF)[None, :]
    sign = jnp.tile(jnp.concatenate([-jnp.ones(ROPE_HALF, F32), jnp.ones(ROPE_HALF, F32)]),
                    LANES // (2 * ROPE_HALF))[None, :]
    pos = positions.astype(F32)[..., None]
    ts = min(S, 512)
    spec = pl.BlockSpec((1, ts, LANES), lambda b, s: (b, s, 0))
    return pl.pallas_call(
        _rope_table_kernel,
        out_shape=(jax.ShapeDtypeStruct((B, S, LANES), F32),) * 2,
        grid=(B, S // ts),
        in_specs=[pl.BlockSpec((1, ts, 1), lambda b, s: (b, s, 0)),
                  pl.BlockSpec((1, LANES), lambda b, s: (0, 0)),
                  pl.BlockSpec((1, LANES), lambda b, s: (0, 0))],
        out_specs=(spec, spec),
        compiler_params=_params(2),
        name="rope_tables",
    )(pos, freq, sign)


def _ada_kernel(c_ref, w_ref, b_ref, o_ref):
    cond = _silu(c_ref[...]).astype(BF16)
    o_ref[0] = _dot(cond, w_ref[0]) + b_ref[0]


def _ada_modulation(c, ada_w, ada_b):
    depth, D, D3 = ada_w.shape
    B = c.shape[0]
    Bp = -(-B // 16) * 16
    c_pad = jnp.pad(c, ((0, Bp - B), (0, 0)))
    tn = 1024
    out = pl.pallas_call(
        _ada_kernel,
        out_shape=jax.ShapeDtypeStruct((depth, Bp, D3), F32),
        grid=(depth, D3 // tn),
        in_specs=[pl.BlockSpec((Bp, D), lambda i, n: (0, 0)),
                  pl.BlockSpec((1, D, tn), lambda i, n: (i, 0, n)),
                  pl.BlockSpec((1, 1, tn), lambda i, n: (i, 0, n))],
        out_specs=pl.BlockSpec((1, Bp, tn), lambda i, n: (i, 0, n)),
        compiler_params=_params(2),
        name="ada_modulation",
    )(c_pad, ada_w.astype(BF16), ada_b[:, None, :])
    return out[:, :B]


def _out_proj_kernel(og_ref, w_ref, x_ref, gate_ref, fg_ref, o_ref):
    y = _dot(og_ref[...], w_ref[...])
    o_ref[...] = _rms(x_ref[...] + gate_ref[0] * y, fg_ref[...])


def _final_out_proj(og, w_out, x2, gate_c, final_g, S):
    T, D = x2.shape
    W = og.shape[1]
    tm = OUT_ROWS
    tpb = S // tm
    return pl.pallas_call(
        _out_proj_kernel,
        out_shape=jax.ShapeDtypeStruct((T, D), F32),
        grid=(T // tm,),
        in_specs=[pl.BlockSpec((tm, W), lambda i: (i, 0)),
                  pl.BlockSpec((W, D), lambda i: (0, 0)),
                  pl.BlockSpec((tm, D), lambda i: (i, 0)),
                  pl.BlockSpec((1, 1, D), lambda i: (i // tpb, 0, 0)),
                  pl.BlockSpec((1, D), lambda i: (0, 0))],
        out_specs=pl.BlockSpec((tm, D), lambda i: (i, 0)),
        compiler_params=_params(1),
        name="out_proj",
    )(og, w_out.astype(BF16), x2, gate_c[:, None, :], final_g[None, :])


def _modulated_norm(x_ref, g_ref, scale_ref, shift_ref):
    y = _rms(x_ref[...], g_ref[...])
    return (y * (1.0 + scale_ref[0]) + shift_ref[0]).astype(BF16)


def _with_pending_residual(proj_kernel, n_in, n_out):
    def kernel(x_ref, *refs, **kwargs):
        ins, (og_ref, w_ref, gate_ref) = refs[:n_in - 1], refs[n_in - 1:n_in + 2]
        outs, xo_ref = refs[n_in + 2:n_in + 2 + n_out], refs[n_in + 2 + n_out]
        xo_ref[...] = x_ref[...] + gate_ref[0] * _dot(og_ref[...], w_ref[...])
        proj_kernel(xo_ref, *ins, *outs, **kwargs)
    return kernel


def _run_proj(proj_kernel, name, S, pending, in_specs, args, out_specs, out_shape):
    x2 = args[0]
    T, D = x2.shape
    tm = PROJ_ROWS
    tpb = S // tm
    if pending is not None:
        og, w_out, gate_c = pending
        W = og.shape[1]
        proj_kernel = _with_pending_residual(proj_kernel, len(in_specs), len(out_specs))
        in_specs = in_specs + [pl.BlockSpec((tm, W), lambda i: (i, 0)),
                               pl.BlockSpec((W, D), lambda i: (0, 0)),
                               pl.BlockSpec((1, 1, D), lambda i: (i // tpb, 0, 0))]
        args = args + [og, w_out.astype(BF16), gate_c[:, None, :]]
        out_specs = out_specs + [pl.BlockSpec((tm, D), lambda i: (i, 0))]
        out_shape = out_shape + [jax.ShapeDtypeStruct((T, D), F32)]
    res = pl.pallas_call(
        proj_kernel,
        out_shape=tuple(out_shape),
        grid=(T // tm,),
        in_specs=in_specs,
        out_specs=tuple(out_specs),
        compiler_params=_params(1),
        name=name,
    )(*args)
    if pending is not None:
        return res[:-1], res[-1]
    return res, x2


def _store_heads(dst_ref, val, n_heads, width, dtype):
    for h in range(n_heads):
        dst_ref[0, h] = val[:, h * width:(h + 1) * width].astype(dtype)


def _store_heads_padded(dst_ref, val, n_heads, upper=None):
    half = LANES // 2
    lane = lax.broadcasted_iota(jnp.int32, (val.shape[0], LANES), 1)
    fill = 0.0 if upper is None else upper
    for h in range(n_heads):
        chunk = val[:, (h // 2) * LANES:(h // 2 + 1) * LANES]
        if h % 2:
            chunk = pltpu.roll(chunk, half, axis=1)
        dst_ref[0, h] = jnp.where(lane < half, chunk, fill).astype(BF16)


def _store_slabs(dst_ref, val, n_heads, d):
    for h in range(n_heads):
        for c in range(val.shape[1] // LANES):
            dst_ref[0, h, c] = val[h * d:(h + 1) * d, c * LANES:(c + 1) * LANES].astype(BF16)


def _rope_wide(val, cos, sin_signed):
    return jnp.concatenate(
        [_rope128(val[:, c * LANES:(c + 1) * LANES], cos, sin_signed)
         for c in range(val.shape[1] // LANES)], axis=1)


def _mla_proj_kernel(x_ref, scale_ref, shift_ref, g_ref, cos_ref, sin_ref,
                     wqa_ref, wkva_ref, wkpe_ref, wgate_ref, qg_ref, kvg_ref,
                     wqb_ref, wkb_ref, wvbT_ref,
                     q_ref, k_ref, vT_ref, sg_ref):
    h = _modulated_norm(x_ref, g_ref, scale_ref, shift_ref)
    cos, sin = cos_ref[0], sin_ref[0]
    head_w = MLA_NOPE + LANES
    q_scale = (MLA_NOPE + MLA_ROPE) ** -0.5 * LOG2_E

    qn = _rms(_dot(h, wqa_ref[...]), qg_ref[...]).astype(BF16)
    q = _dot(qn, wqb_ref[...]) * q_scale
    for hd in range(MLA_HEADS):
        lo = hd * head_w
        q_ref[:, lo:lo + MLA_NOPE] = q[:, lo:lo + MLA_NOPE].astype(BF16)
        q_ref[:, lo + MLA_NOPE:lo + head_w] = _rope128(
            q[:, lo + MLA_NOPE:lo + head_w], cos, sin).astype(BF16)

    kvn = _rms(_dot(h, wkva_ref[...]), kvg_ref[...]).astype(BF16)
    kn = _dot(kvn, wkb_ref[...])
    kpe = _rope128(_dot(h, wkpe_ref[...]), cos, sin).astype(BF16)
    for hd in range(MLA_HEADS):
        lo = hd * head_w
        k_ref[:, lo:lo + MLA_NOPE] = kn[:, hd * MLA_NOPE:(hd + 1) * MLA_NOPE].astype(BF16)
        k_ref[:, lo + MLA_NOPE:lo + head_w] = kpe
    _store_slabs(vT_ref, _dot_nt(wvbT_ref[...], kvn), MLA_HEADS, MLA_V)

    sg_ref[...] = _silu(_dot(h, wgate_ref[...]))


def _mla_attn_kernel(q_ref, k_ref, vT_ref, sg_ref, o_ref, s_ref, m_ref, l_ref, acc_ref):
    qi = pl.program_id(2)
    tq = tk = MLA_TQ
    slabs = tk // LANES
    head_w = MLA_NOPE + LANES
    heads = range(MLA_HEADS_PER_STEP)

    def scores(slot, kb):
        rows = pl.ds(pl.multiple_of(kb * tk, tk), tk)
        for h in heads:
            cols = slice(h * head_w, (h + 1) * head_w)
            s_ref[slot, h] = _dot_nt(k_ref[0, rows, cols], q_ref[0, :, cols])

    def absorb(slot, kb, masked):
        if masked:
            kpos = lax.broadcasted_iota(jnp.int32, (tk, tq), 0)
            qpos = lax.broadcasted_iota(jnp.int32, (tk, tq), 1)
            visible = kpos <= qpos
        for h in heads:
            s = s_ref[slot, h]
            if masked:
                s = jnp.where(visible, s, NEG_INF)
            m = m_ref[h]
            m_new = jnp.maximum(m, jnp.max(s, axis=0, keepdims=True))
            alpha = jnp.exp2(m - m_new)
            p = jnp.exp2(s - m_new)
            m_ref[h] = m_new
            l_ref[h] = alpha * l_ref[h] + jnp.sum(p, axis=0, keepdims=True)
            vt = jnp.concatenate([vT_ref[0, h, kb * slabs + c] for c in range(slabs)], axis=1)
            acc_ref[h] = alpha * acc_ref[h] + _dot(vt, p.astype(BF16))

    m_ref[...] = jnp.full(m_ref.shape, NEG_INF, F32)
    l_ref[...] = jnp.zeros(l_ref.shape, F32)
    acc_ref[...] = jnp.zeros(acc_ref.shape, F32)
    scores(0, 0)

    def pair(j, carry):
        kb = 2 * j
        scores(1, kb + 1)
        absorb(0, kb, False)
        scores(0, kb + 2)
        absorb(1, kb + 1, False)
        return carry

    lax.fori_loop(0, lax.shift_right_logical(qi, 1), pair, 0)

    @pl.when((qi & 1) == 1)
    def _():
        scores(1, qi)
        absorb(0, qi - 1, False)
        absorb(1, qi, True)

    @pl.when((qi & 1) == 0)
    def _():
        absorb(0, qi, True)

    for h in heads:
        cols = slice(h * MLA_V, (h + 1) * MLA_V)
        o = (acc_ref[h] * (1.0 / l_ref[h])).T
        o_ref[0, :, cols] = (o * sg_ref[0, :, cols]).astype(BF16)


def _mla_layer(x2, pending, B, S, norm_g, scale, shift, cos, sin, w_in, q_norm_g, kv_norm_g, w_q_b, w_kv_b):
    T, D = x2.shape
    H = MLA_HEADS
    head_w = MLA_NOPE + LANES
    o1 = MLA_Q_LORA
    o2 = o1 + MLA_KV_LORA
    o3 = o2 + MLA_ROPE
    w_qa, w_kva, w_kpe, w_gate = (w_in[:, :o1], w_in[:, o1:o2], w_in[:, o2:o3], w_in[:, o3:])
    w_kpe = jnp.pad(w_kpe, ((0, 0), (0, LANES - MLA_ROPE)))
    wq = w_q_b.reshape(MLA_Q_LORA, H, MLA_NOPE + MLA_ROPE)
    wq = jnp.pad(wq, ((0, 0), (0, 0), (0, head_w - MLA_NOPE - MLA_ROPE))).reshape(MLA_Q_LORA, H * head_w)
    wkv = w_kv_b.reshape(MLA_KV_LORA, H, MLA_NOPE + MLA_V)
    w_kb = wkv[:, :, :MLA_NOPE].reshape(MLA_KV_LORA, H * MLA_NOPE)
    w_vbT = wkv[:, :, MLA_NOPE:].reshape(MLA_KV_LORA, H * MLA_V).T
    W = H * MLA_V

    tm = PROJ_ROWS
    tpb = S // tm
    full = lambda shape: pl.BlockSpec(shape, lambda i: (0,) * len(shape))
    rows = lambda w: pl.BlockSpec((tm, w), lambda i: (i, 0))
    per_b = pl.BlockSpec((1, 1, D), lambda i: (i // tpb, 0, 0))
    tab = pl.BlockSpec((1, tm, LANES), lambda i: (i // tpb, i % tpb, 0))
    (q, k, vT, sg), x2 = _run_proj(
        _mla_proj_kernel, "mla_proj", S, pending,
        in_specs=[rows(D), per_b, per_b, full((1, D)), tab, tab,
                  full((D, o1)), full((D, MLA_KV_LORA)), full((D, LANES)), full((D, W)),
                  full((1, o1)), full((1, MLA_KV_LORA)),
                  full((o1, H * head_w)), full((MLA_KV_LORA, H * MLA_NOPE)), full((W, MLA_KV_LORA))],
        args=[x2, scale[:, None, :], shift[:, None, :], norm_g[None, :], cos, sin,
              w_qa.astype(BF16), w_kva.astype(BF16), w_kpe.astype(BF16), w_gate.astype(BF16),
              q_norm_g[None, :], kv_norm_g[None, :],
              wq.astype(BF16), w_kb.astype(BF16), w_vbT.astype(BF16)],
        out_specs=[rows(H * head_w), rows(H * head_w),
                   pl.BlockSpec((1, H, tm // LANES, MLA_V, LANES), lambda i: (i // tpb, 0, i % tpb, 0, 0)),
                   rows(W)],
        out_shape=[jax.ShapeDtypeStruct((T, H * head_w), BF16),
                   jax.ShapeDtypeStruct((T, H * head_w), BF16),
                   jax.ShapeDtypeStruct((B, H, S // LANES, MLA_V, LANES), BF16),
                   jax.ShapeDtypeStruct((T, W), F32)])

    tq = MLA_TQ
    hp = MLA_HEADS_PER_STEP
    q3 = q.reshape(B, S, H * head_w)
    k3 = k.reshape(B, S, H * head_w)
    sg3 = sg.reshape(B, S, W)
    og = pl.pallas_call(
        _mla_attn_kernel,
        out_shape=jax.ShapeDtypeStruct((B, S, W), BF16),
        grid=(B, H // hp, S // tq),
        in_specs=[pl.BlockSpec((1, tq, hp * head_w), lambda b, h, i: (b, i, h)),
                  pl.BlockSpec((1, S, hp * head_w), lambda b, h, i: (b, 0, h)),
                  pl.BlockSpec((1, hp, S // LANES, MLA_V, LANES), lambda b, h, i: (b, h, 0, 0, 0)),
                  pl.BlockSpec((1, tq, hp * MLA_V), lambda b, h, i: (b, i, h))],
        out_specs=pl.BlockSpec((1, tq, hp * MLA_V), lambda b, h, i: (b, i, h)),
        scratch_shapes=[pltpu.VMEM((2, hp, tq, tq), F32),
                        pltpu.VMEM((hp, 1, tq), F32), pltpu.VMEM((hp, 1, tq), F32),
                        pltpu.VMEM((hp, MLA_V, tq), F32)],
        compiler_params=_params(3),
        name="mla_attn",
    )(q3, k3, vT, sg3)
    return og.reshape(T, W), x2


def _swa_proj_kernel(x_ref, scale_ref, shift_ref, g_ref, cos_ref, sin_ref,
                     wq_ref, wk_ref, wvT_ref, wgate_ref,
                     q_ref, k_ref, vT_ref, sg_ref):
    h = _modulated_norm(x_ref, g_ref, scale_ref, shift_ref)
    cos, sin = cos_ref[0], sin_ref[0]
    q = _rope_wide(_dot(h, wq_ref[...]) * (SWA_HEAD_DIM ** -0.5 * LOG2_E), cos, sin)
    _store_heads(q_ref, q, SWA_HEADS, SWA_HEAD_DIM, BF16)
    k = _rope_wide(_dot(h, wk_ref[...]), cos, sin)
    _store_heads(k_ref, k, SWA_KV_HEADS, SWA_HEAD_DIM, BF16)
    _store_slabs(vT_ref, _dot_nt(wvT_ref[...], h), SWA_KV_HEADS, SWA_HEAD_DIM)
    sg_ref[...] = _silu(_dot(h, wgate_ref[...]))


def _band_valid(diff, window):
    return lax.bitcast_convert_type(diff, jnp.uint32) < jnp.uint32(window)


def _swa_attn_kernel(q_ref, k_ref, vT_ref, sink_ref, sg_ref, o_ref):
    qi = pl.program_id(1)
    tq = SWA_TQ
    G = SWA_HEADS // SWA_KV_HEADS
    d = SWA_HEAD_DIM
    n = G * tq
    span = SWA_WINDOW + tq
    groups = range(SWA_KV_HEADS)
    blk0 = jnp.maximum(qi - SWA_WINDOW // tq, 0)
    base = pl.multiple_of(blk0 * LANES, LANES)
    s_all = [_dot_nt(k_ref[0, i, pl.ds(base, span), :], q_ref[0, i * G:(i + 1) * G].reshape(n, d))
             for i in groups]
    kpos = base + lax.broadcasted_iota(jnp.int32, (span, n), 0)
    qpos = qi * tq + (lax.broadcasted_iota(jnp.int32, (span, n), 1) & (tq - 1))
    in_band = _band_valid(qpos - kpos, SWA_WINDOW)
    for i in groups:
        s = jnp.where(in_band, s_all[i], NEG_INF)
        sink = sink_ref[i]
        m = jnp.maximum(jnp.max(s, axis=0, keepdims=True), sink)
        e = jnp.exp2(s - m)
        denom = jnp.sum(e, axis=0, keepdims=True) + jnp.exp2(sink - m)
        vt = jnp.concatenate([vT_ref[0, i, blk0 + c] for c in range(span // LANES)], axis=1)
        oT = _dot(vt, e.astype(BF16)) * (1.0 / denom)
        o = jnp.concatenate([oT[:, g * tq:(g + 1) * tq] for g in range(G)], axis=0).T
        cols = slice(i * G * d, (i + 1) * G * d)
        o_ref[0, :, cols] = (o * sg_ref[0, :, cols]).astype(BF16)


def _swa_layer(x2, pending, B, S, norm_g, scale, shift, cos, sin, w_in, sinks):
    T, D = x2.shape
    H, KV, d = SWA_HEADS, SWA_KV_HEADS, SWA_HEAD_DIM
    G = H // KV
    W = H * d
    KW = KV * d
    w_q, w_k, w_v, w_gate = (w_in[:, :W], w_in[:, W:W + KW], w_in[:, W + KW:W + 2 * KW],
                             w_in[:, W + 2 * KW:])
    tm = PROJ_ROWS
    tpb = S // tm
    full = lambda shape: pl.BlockSpec(shape, lambda i: (0,) * len(shape))
    rows = lambda w: pl.BlockSpec((tm, w), lambda i: (i, 0))
    per_b = pl.BlockSpec((1, 1, D), lambda i: (i // tpb, 0, 0))
    tab = pl.BlockSpec((1, tm, LANES), lambda i: (i // tpb, i % tpb, 0))
    heads = lambda nh: pl.BlockSpec((1, nh, tm, d), lambda i: (i // tpb, 0, i % tpb, 0))
    (q, k, vT, sg), x2 = _run_proj(
        _swa_proj_kernel, "swa_proj", S, pending,
        in_specs=[rows(D), per_b, per_b, full((1, D)), tab, tab,
                  full((D, W)), full((D, KW)), full((KW, D)), full((D, W))],
        args=[x2, scale[:, None, :], shift[:, None, :], norm_g[None, :], cos, sin,
              w_q.astype(BF16), w_k.astype(BF16), w_v.T.astype(BF16), w_gate.astype(BF16)],
        out_specs=[heads(H), heads(KV),
                   pl.BlockSpec((1, KV, tm // LANES, d, LANES), lambda i: (i // tpb, 0, i % tpb, 0, 0)),
                   rows(W)],
        out_shape=[jax.ShapeDtypeStruct((B, H, S, d), BF16),
                   jax.ShapeDtypeStruct((B, KV, S, d), BF16),
                   jax.ShapeDtypeStruct((B, KV, S // LANES, d, LANES), BF16),
                   jax.ShapeDtypeStruct((T, W), F32)])

    tq = SWA_TQ
    n = G * tq
    sink_rows = jnp.repeat(sinks.astype(F32).reshape(KV, G) * LOG2_E, tq, axis=1)[:, None, :]
    og = pl.pallas_call(
        _swa_attn_kernel,
        out_shape=jax.ShapeDtypeStruct((B, S, W), BF16),
        grid=(B, S // tq),
        in_specs=[pl.BlockSpec((1, H, tq, d), lambda b, i: (b, 0, i, 0)),
                  pl.BlockSpec((1, KV, S, d), lambda b, i: (b, 0, 0, 0)),
                  pl.BlockSpec((1, KV, S // LANES, d, LANES), lambda b, i: (b, 0, 0, 0, 0)),
                  pl.BlockSpec((KV, 1, n), lambda b, i: (0, 0, 0)),
                  pl.BlockSpec((1, tq, W), lambda b, i: (b, i, 0))],
        out_specs=pl.BlockSpec((1, tq, W), lambda b, i: (b, i, 0)),
        compiler_params=_params(2),
        name="swa_attn",
    )(q, k, vT, sink_rows, sg.reshape(B, S, W))
    return og.reshape(T, W), x2


def _nsa_proj_kernel(x_ref, scale_ref, shift_ref, g_ref, cos_ref, sin_ref,
                     wq_ref, wkc_ref, wvc_ref, wks_ref, wvsT_ref, wkw_ref, wvwT_ref, wgT_ref, wgate_ref,
                     q_ref, kc_ref, vc_ref, ks_ref, vsT_ref, kw_ref, vwT_ref, gT_ref, sg_ref, *, tiles_per_seq):
    h = _modulated_norm(x_ref, g_ref, scale_ref, shift_ref)
    cos, sin = cos_ref[0], sin_ref[0]
    KV, d = NSA_KV_HEADS, NSA_HEAD_DIM
    tm = h.shape[0]
    q = _rope_wide(_dot(h, wq_ref[...]) * (d ** -0.5 * LOG2_E), cos, sin)
    _store_heads_padded(q_ref, q, NSA_HEADS)
    _store_heads(kc_ref, _dot(h, wkc_ref[...]), KV, d, F32)
    _store_heads(vc_ref, _dot(h, wvc_ref[...]), KV, d, F32)
    pos = (pl.program_id(0) % tiles_per_seq) * tm + lax.broadcasted_iota(jnp.int32, (tm, LANES), 0)
    lane = lax.broadcasted_iota(jnp.int32, (tm, LANES), 1)
    sel_shift = NSA_SEL_BLOCK.bit_length() - 1
    block_onehot = jnp.where(lane - LANES // 2 == lax.shift_right_logical(pos, sel_shift), 1.0, 0.0)
    _store_heads_padded(ks_ref, _rope_wide(_dot(h, wks_ref[...]), cos, sin), KV, upper=block_onehot)
    _store_heads_padded(kw_ref, _rope_wide(_dot(h, wkw_ref[...]), cos, sin), KV)
    _store_slabs(vsT_ref, _dot_nt(wvsT_ref[...], h), KV, d)
    _store_slabs(vwT_ref, _dot_nt(wvwT_ref[...], h), KV, d)
    gT = _sigmoid(_dot_nt(wgT_ref[...], h))
    for kv in range(KV):
        gT_ref[0, kv] = gT[kv * 16:(kv + 1) * 16, :]
    sg_ref[...] = _silu(_dot(h, wgate_ref[...]))


def _nsa_compress_kernel(kc_ref, vc_ref, pe_ref, wk1_ref, wk2_ref, wv1_ref, wv2T_ref, kout_ref, vT_ref):
    n_chunks = kout_ref.shape[2]
    stride = NSA_CMP_STRIDE

    def hidden(x_ref, w1_ref):
        top = bot = None
        for l in range(stride):
            x_l = x_ref[0, 0, pl.ds(l, n_chunks, stride=stride), :]
            t = _dot((x_l + pe_ref[l]).astype(BF16), w1_ref[l])
            b = _dot((x_l + pe_ref[stride + l]).astype(BF16), w1_ref[stride + l])
            top = t if top is None else top + t
            bot = b if bot is None else bot + b
        return _silu(top + pltpu.roll(bot, n_chunks - 1, axis=0)).astype(BF16)

    hk = hidden(kc_ref, wk1_ref)
    kc = _dot(hk, wk2_ref[...])
    row = lax.broadcasted_iota(jnp.int32, kc.shape, 0)
    kout_ref[0, 0] = jnp.where(row < n_chunks - 1, kc, 0.0).astype(BF16)
    hv = hidden(vc_ref, wv1_ref)
    vT = _dot_nt(wv2T_ref[...], hv)
    col = lax.broadcasted_iota(jnp.int32, vT.shape, 1)
    vT_ref[0, 0] = jnp.where(col < n_chunks - 1, vT, 0.0).astype(BF16)


def _nsa_attn_kernel(q_ref, kc_ref, vcT_ref, ovT_ref, ks_ref, vsT_ref, kw_ref, vwT_ref, gT_ref, sg_ref,
                     o_ref, qsel_ref, s_ref, m_ref, l_ref, acc_ref, *, n_top):
    qi = pl.program_id(2)
    tq = NSA_TQ
    G = NSA_HEADS // NSA_KV_HEADS
    d = NSA_HEAD_DIM
    n = G * tq
    groups = range(NSA_KV_PER_STEP)
    q0 = qi * tq
    q = [q_ref[0, i * G:(i + 1) * G].reshape(n, LANES) for i in groups]

    def col_qpos(shape):
        return q0 + (lax.broadcasted_iota(jnp.int32, shape, 1) & (tq - 1))

    span = NSA_WINDOW + tq
    blk0 = jnp.maximum(qi * (tq // LANES) - NSA_WINDOW // LANES, 0)
    base = pl.multiple_of(blk0 * LANES, LANES)
    s_cmp = [_dot_nt(kc_ref[0, i], q[i]) for i in groups]
    s_win = [_dot_nt(kw_ref[0, i, pl.ds(base, span), :], q[i]) for i in groups]

    cmp_end = lax.broadcasted_iota(jnp.int32, s_cmp[0].shape, 0) * NSA_CMP_STRIDE + (NSA_CMP_BLOCK - 1)
    valid = cmp_end <= col_qpos(s_cmp[0].shape)
    o_c, imp = [], []
    for i in groups:
        s = jnp.where(valid, s_cmp[i], NEG_INF)
        m = jnp.max(s, axis=0, keepdims=True)
        e = jnp.where(valid, jnp.exp2(s - m), 0.0)
        l = jnp.sum(e, axis=0, keepdims=True)
        p = (e * jnp.where(l > 0.0, 1.0 / l, 0.0)).astype(BF16)
        o_c.append(_dot(vcT_ref[0, i], p))
        imp_all = _dot(ovT_ref[...], p)
        acc = imp_all[:, 0:tq]
        for g in range(1, G):
            acc = acc + imp_all[:, g * tq:(g + 1) * tq]
        imp.append(acc)

    ns = imp[0].shape[0]
    blk = lax.broadcasted_iota(jnp.int32, imp[0].shape, 0)
    sel_shift = NSA_SEL_BLOCK.bit_length() - 1
    q_blk = lax.shift_right_logical(q0 + lax.broadcasted_iota(jnp.int32, imp[0].shape, 1), sel_shift)
    causal = blk <= q_blk
    forced = (blk == 0) | (blk == q_blk) | (blk == q_blk - 1)
    half = LANES // 2
    lane = lax.broadcasted_iota(jnp.int32, (G, tq, LANES), 2)
    for i in groups:
        val = jnp.where(causal, imp[i] + jnp.where(forced, NSA_FORCE_BONUS, 0.0), -1.0)
        rank = jnp.zeros(val.shape, F32)
        for r in range(ns):
            row = val[r:r + 1, :]
            rank = rank + jnp.where(blk > r, jnp.where(row >= val, 1.0, 0.0), jnp.where(row > val, 1.0, 0.0))
        bias = jnp.where(causal, jnp.where(rank < n_top, 0.0, NEG_INF), NEG_INF)
        bias_rows = [jnp.zeros((half, tq), F32), bias]
        if ns < half:
            bias_rows.append(jnp.zeros((half - ns, tq), F32))
        bias_rows = jnp.concatenate(bias_rows, axis=0).T.astype(BF16)
        qsel_ref[i] = jnp.where(lane < half, q[i].reshape(G, tq, LANES), bias_rows[None]).reshape(n, LANES)

    tk = NSA_SEL_TK
    slabs = tk // LANES

    def sel_scores(slot, kb):
        rows = pl.ds(pl.multiple_of(kb * tk, tk), tk)
        for i in groups:
            s_ref[slot, i] = _dot_nt(ks_ref[0, i, rows, :], qsel_ref[i])

    def sel_absorb(slot, kb, diagonal):
        if diagonal:
            kpos = kb * tk + lax.broadcasted_iota(jnp.int32, (tk, n), 0)
            keep = kpos <= col_qpos((tk, n))
        for i in groups:
            s = s_ref[slot, i]
            if diagonal:
                s = jnp.where(keep, s, NEG_INF)
            m = m_ref[i]
            m_new = jnp.maximum(m, jnp.max(s, axis=0, keepdims=True))
            alpha = jnp.exp2(m - m_new)
            p = jnp.exp2(s - m_new)
            m_ref[i] = m_new
            l_ref[i] = alpha * l_ref[i] + jnp.sum(p, axis=0, keepdims=True)
            vt = jnp.concatenate([vsT_ref[0, i, kb * slabs + c] for c in range(slabs)], axis=1)
            acc_ref[i] = alpha * acc_ref[i] + _dot(vt, p.astype(BF16))

    m_ref[...] = jnp.full(m_ref.shape, NEG_INF, F32)
    l_ref[...] = jnp.zeros(l_ref.shape, F32)
    acc_ref[...] = jnp.zeros(acc_ref.shape, F32)
    kb_diag = lax.shift_right_logical(q0, tk.bit_length() - 1)
    sel_scores(0, 0)

    diff = col_qpos(s_win[0].shape) - (base + lax.broadcasted_iota(jnp.int32, s_win[0].shape, 0))
    in_band = _band_valid(diff, NSA_WINDOW)
    o_w = []
    for i in groups:
        s = jnp.where(in_band, s_win[i], NEG_INF)
        e = jnp.exp2(s - jnp.max(s, axis=0, keepdims=True))
        vt = jnp.concatenate([vwT_ref[0, i, blk0 + c] for c in range(span // LANES)], axis=1)
        o_w.append(_dot(vt, e.astype(BF16)) * (1.0 / jnp.sum(e, axis=0, keepdims=True)))

    def sel_pair(j, carry):
        kb = 2 * j
        sel_scores(1, kb + 1)
        sel_absorb(0, kb, False)
        sel_scores(0, kb + 2)
        sel_absorb(1, kb + 1, False)
        return carry

    lax.fori_loop(0, lax.shift_right_logical(kb_diag, 1), sel_pair, 0)

    @pl.when((kb_diag & 1) == 1)
    def _():
        sel_scores(1, kb_diag)
        sel_absorb(0, kb_diag - 1, False)
        sel_absorb(1, kb_diag, True)

    @pl.when((kb_diag & 1) == 0)
    def _():
        sel_absorb(0, kb_diag, True)

    for i in groups:
        o_s = acc_ref[i] * (1.0 / l_ref[i])
        gates = gT_ref[0, i]
        outs = []
        for g in range(G):
            cols = slice(g * tq, (g + 1) * tq)
            outs.append(gates[3 * g:3 * g + 1, :] * o_c[i][:, cols]
                        + gates[3 * g + 1:3 * g + 2, :] * o_s[:, cols]
                        + gates[3 * g + 2:3 * g + 3, :] * o_w[i][:, cols])
        o = jnp.concatenate(outs, axis=0).T
        cols = slice(i * G * d, (i + 1) * G * d)
        o_ref[0, :, cols] = (o * sg_ref[0, :, cols]).astype(BF16)


def _nsa_overlap_T(nc_pad, ns):
    nc = nc_pad - (NSA_CMP_BLOCK // NSA_CMP_STRIDE - 1)
    cs = np.arange(nc_pad)[None, :] * NSA_CMP_STRIDE
    ss = np.arange(ns)[:, None] * NSA_SEL_BLOCK
    ov = np.clip(np.minimum(cs + NSA_CMP_BLOCK, ss + NSA_SEL_BLOCK) - np.maximum(cs, ss), 0, None)
    ov = np.where(np.arange(nc_pad)[None, :] < nc, ov, 0)
    return jnp.asarray(ov / NSA_CMP_BLOCK, dtype=BF16)


def _nsa_layer(x2, pending, B, S, norm_g, scale, shift, cos, sin, w_in, cmp_pos, w_k1, w_k2, w_v1, w_v2):
    T, D = x2.shape
    H, KV, d = NSA_HEADS, NSA_KV_HEADS, NSA_HEAD_DIM
    G = H // KV
    W = H * d
    KW = KV * d
    offs = np.cumsum([0, W] + [KW] * 6 + [3 * H, W])
    w_q, w_kc, w_vc, w_ks, w_vs, w_kw, w_vw, w_g, w_gate = (
        w_in[:, offs[i]:offs[i + 1]] for i in range(9))
    w_gT = jnp.pad(w_g.T.reshape(KV, 3 * G, D), ((0, 0), (0, 16 - 3 * G), (0, 0))).reshape(KV * 16, D)

    tm = PROJ_ROWS
    tpb = S // tm
    full = lambda shape: pl.BlockSpec(shape, lambda i: (0,) * len(shape))
    rows = lambda w: pl.BlockSpec((tm, w), lambda i: (i, 0))
    per_b = pl.BlockSpec((1, 1, D), lambda i: (i // tpb, 0, 0))
    tab = pl.BlockSpec((1, tm, LANES), lambda i: (i // tpb, i % tpb, 0))
    heads = lambda nh: pl.BlockSpec((1, nh, tm, d), lambda i: (i // tpb, 0, i % tpb, 0))
    wide = lambda nh: pl.BlockSpec((1, nh, tm, LANES), lambda i: (i // tpb, 0, i % tpb, 0))
    slab = pl.BlockSpec((1, KV, tm // LANES, d, LANES), lambda i: (i // tpb, 0, i % tpb, 0, 0))
    kv_f32 = jax.ShapeDtypeStruct((B, KV, S, d), F32)
    kv_bf16 = jax.ShapeDtypeStruct((B, KV, S, LANES), BF16)
    kv_slab = jax.ShapeDtypeStruct((B, KV, S // LANES, d, LANES), BF16)
    assert S // NSA_SEL_BLOCK <= LANES // 2, "selection-block one-hot must fit in lanes 64..127"
    (q, kc, vc, ks, vsT, kw, vwT, gT, sg), x2 = _run_proj(
        functools.partial(_nsa_proj_kernel, tiles_per_seq=tpb), "nsa_proj", S, pending,
        in_specs=[rows(D), per_b, per_b, full((1, D)), tab, tab,
                  full((D, W)), full((D, KW)), full((D, KW)), full((D, KW)), full((KW, D)),
                  full((D, KW)), full((KW, D)), full((KV * 16, D)), full((D, W))],
        args=[x2, scale[:, None, :], shift[:, None, :], norm_g[None, :], cos, sin,
              w_q.astype(BF16), w_kc.astype(BF16), w_vc.astype(BF16), w_ks.astype(BF16),
              w_vs.T.astype(BF16), w_kw.astype(BF16), w_vw.T.astype(BF16), w_gT.astype(BF16),
              w_gate.astype(BF16)],
        out_specs=[wide(H), heads(KV), heads(KV), wide(KV), slab, wide(KV), slab,
                   pl.BlockSpec((1, KV, 16, tm), lambda i: (i // tpb, 0, 0, i % tpb)),
                   rows(W)],
        out_shape=[jax.ShapeDtypeStruct((B, H, S, LANES), BF16), kv_f32, kv_f32,
                   kv_bf16, kv_slab, kv_bf16, kv_slab,
                   jax.ShapeDtypeStruct((B, KV, 16, S), F32),
                   jax.ShapeDtypeStruct((T, W), F32)])

    n_chunks = S // NSA_CMP_STRIDE
    nl = NSA_CMP_BLOCK
    token_spec = pl.BlockSpec((1, 1, S, d), lambda b, kv: (b, kv, 0, 0))
    full2 = lambda shape: pl.BlockSpec(shape, lambda b, kv: (0,) * len(shape))
    k_c, v_cT = pl.pallas_call(
        _nsa_compress_kernel,
        out_shape=(jax.ShapeDtypeStruct((B, KV, n_chunks, LANES), BF16),
                   jax.ShapeDtypeStruct((B, KV, d, n_chunks), BF16)),
        grid=(B, KV),
        in_specs=[token_spec, token_spec, full2((nl, 1, d)),
                  full2((nl, d, NSA_CMP_HIDDEN)), full2((NSA_CMP_HIDDEN, LANES)),
                  full2((nl, d, NSA_CMP_HIDDEN)), full2((d, NSA_CMP_HIDDEN))],
        out_specs=(pl.BlockSpec((1, 1, n_chunks, LANES), lambda b, kv: (b, kv, 0, 0)),
                   pl.BlockSpec((1, 1, d, n_chunks), lambda b, kv: (b, kv, 0, 0))),
        compiler_params=_params(2),
        name="nsa_compress",
    )(kc, vc, cmp_pos[:, None, :],
      w_k1.reshape(nl, d, NSA_CMP_HIDDEN).astype(BF16),
      jnp.pad(w_k2, ((0, 0), (0, LANES - d))).astype(BF16),
      w_v1.reshape(nl, d, NSA_CMP_HIDDEN).astype(BF16), w_v2.T.astype(BF16))

    tq = NSA_TQ
    ns = S // NSA_SEL_BLOCK
    ovT = _nsa_overlap_T(n_chunks, ns)
    kp = NSA_KV_PER_STEP
    whole = lambda shape: pl.BlockSpec((1, kp) + shape, lambda b, kv, i: (b, kv) + (0,) * len(shape))
    og = pl.pallas_call(
        functools.partial(_nsa_attn_kernel, n_top=min(NSA_N_SELECT, ns)),
        out_shape=jax.ShapeDtypeStruct((B, S, W), BF16),
        grid=(B, KV // kp, S // tq),
        in_specs=[pl.BlockSpec((1, kp * G, tq, LANES), lambda b, kv, i: (b, kv, i, 0)),
                  whole((n_chunks, LANES)), whole((d, n_chunks)),
                  pl.BlockSpec((ns, n_chunks), lambda b, kv, i: (0, 0)),
                  whole((S, LANES)), whole((S // LANES, d, LANES)),
                  whole((S, LANES)), whole((S // LANES, d, LANES)),
                  pl.BlockSpec((1, kp, 16, tq), lambda b, kv, i: (b, kv, 0, i)),
                  pl.BlockSpec((1, tq, kp * G * d), lambda b, kv, i: (b, i, kv))],
        out_specs=pl.BlockSpec((1, tq, kp * G * d), lambda b, kv, i: (b, i, kv)),
        scratch_shapes=[pltpu.VMEM((kp, G * tq, LANES), BF16),
                        pltpu.VMEM((2, kp, NSA_SEL_TK, G * tq), F32),
                        pltpu.VMEM((kp, 1, G * tq), F32), pltpu.VMEM((kp, 1, G * tq), F32),
                        pltpu.VMEM((kp, d, G * tq), F32)],
        compiler_params=_params(3),
        name="nsa_attn",
    )(q, k_c, v_cT, ovT, ks, vsT, kw, vwT, gT, sg.reshape(B, S, W))
    return og.reshape(T, W), x2


def kernel(x, c, positions, norm_g, ada_w, ada_b, mla_w_in, mla_q_norm_g, mla_kv_norm_g, mla_w_q_b, mla_w_kv_b, mla_w_out, swa_w_in, swa_sinks, swa_w_out, nsa_w_in, nsa_cmp_pos, nsa_w_cmp_k1, nsa_w_cmp_k2, nsa_w_cmp_v1, nsa_w_cmp_v2, nsa_w_out, final_norm_g):
    B, S, D = x.shape
    depth = norm_g.shape[0]
    cos, sin = _rope_tables(positions)
    mod = _ada_modulation(c, ada_w, ada_b)
    x2 = x.reshape(B * S, D)
    pending = None
    for i in range(depth):
        shift, scale, gate = mod[i, :, :D], mod[i, :, D:2 * D], mod[i, :, 2 * D:]
        kind, j = i % 3, i // 3
        if kind == 0:
            og, x2 = _mla_layer(x2, pending, B, S, norm_g[i], scale, shift, cos, sin, mla_w_in[j],
                                mla_q_norm_g[j], mla_kv_norm_g[j], mla_w_q_b[j], mla_w_kv_b[j])
            w_out = mla_w_out[j]
        elif kind == 1:
            og, x2 = _swa_layer(x2, pending, B, S, norm_g[i], scale, shift, cos, sin, swa_w_in[j], swa_sinks[j])
            w_out = swa_w_out[j]
        else:
            og, x2 = _nsa_layer(x2, pending, B, S, norm_g[i], scale, shift, cos, sin, nsa_w_in[j],
                                nsa_cmp_pos[j], nsa_w_cmp_k1[j], nsa_w_cmp_k2[j], nsa_w_cmp_v1[j],
                                nsa_w_cmp_v2[j])
            w_out = nsa_w_out[j]
        pending = (og, w_out, gate)
    return _final_out_proj(*pending[:2], x2, pending[2], final_norm_g, S).reshape(B, S, D)
```

```python
import functools

import jax
import jax.numpy as jnp
import numpy as np
from jax import lax
from jax.experimental import pallas as pl
from jax.experimental.pallas import tpu as pltpu

F32 = jnp.float32
BF16 = jnp.bfloat16

ROPE_THETA = 10000.0
RMS_EPS = 1e-6
NEG_INF = -1e30
LOG2_E = 1.4426950408889634

MLA_HEADS = 8
MLA_NOPE = 128
MLA_ROPE = 64
MLA_V = 128
MLA_Q_LORA = 256
MLA_KV_LORA = 128

SWA_HEADS = 16
SWA_KV_HEADS = 2
SWA_HEAD_DIM = 64
SWA_WINDOW = 128

NSA_HEADS = 16
NSA_KV_HEADS = 4
NSA_HEAD_DIM = 64
NSA_CMP_BLOCK = 32
NSA_CMP_STRIDE = 16
NSA_CMP_HIDDEN = 128
NSA_SEL_BLOCK = 64
NSA_N_SELECT = 16
NSA_WINDOW = 512
NSA_FORCE_BONUS = 1e4

LANES = 128
ROPE_HALF = 32
VMEM_LIMIT = 56 * 1024 * 1024

PROJ_ROWS = 512
OUT_ROWS = 512
MLA_TQ = 512
MLA_HEADS_PER_STEP = 4
SWA_TQ = 128
NSA_TQ = 128
NSA_SEL_TK = 512
NSA_KV_PER_STEP = 4
NSA_FRONT_VARIANTS = 4


def _dot(a, b):
    return jnp.dot(a, b, preferred_element_type=F32)


def _dot_nt(a, b):
    return lax.dot_general(a, b, (((1,), (1,)), ((), ())), preferred_element_type=F32)


def _silu(x):
    return x / (1.0 + jnp.exp(-x))


def _sigmoid(x):
    return 1.0 / (1.0 + jnp.exp(-x))


def _rms(x, g):
    return x * lax.rsqrt(jnp.mean(x * x, axis=-1, keepdims=True) + RMS_EPS) * g


def _params(n_axes):
    return pltpu.CompilerParams(dimension_semantics=("arbitrary",) * n_axes,
                                vmem_limit_bytes=VMEM_LIMIT)


def _rope128(x, cos, sin_signed):
    lane = lax.broadcasted_iota(jnp.int32, x.shape, 1)
    lower_half = (lane & (2 * ROPE_HALF - 1)) < ROPE_HALF
    partner = jnp.where(lower_half, pltpu.roll(x, LANES - ROPE_HALF, axis=1),
                        pltpu.roll(x, ROPE_HALF, axis=1))
    return x * cos + partner * sin_signed


def _rope_table_kernel(pos_ref, freq_ref, sign_ref, cos_ref, sin_ref):
    ang = pos_ref[0] * freq_ref[...]
    cos_ref[0] = jnp.cos(ang)
    sin_ref[0] = jnp.sin(ang) * sign_ref[...]


def _rope_tables(positions):
    B, S = positions.shape
    inv_freq = ROPE_THETA ** (-jnp.arange(ROPE_HALF, dtype=F32) / ROPE_HALF)
    freq = jnp.tile(inv_freq, LANES // ROPE_HALF)[None, :]
    sign = jnp.tile(jnp.concatenate([-jnp.ones(ROPE_HALF, F32), jnp.ones(ROPE_HALF, F32)]),
                    LANES // (2 * ROPE_HALF))[None, :]
    pos = positions.astype(F32)[..., None]
    ts = min(S, 512)
    spec = pl.BlockSpec((1, ts, LANES), lambda b, s: (b, s, 0))
    return pl.pallas_call(
        _rope_table_kernel,
        out_shape=(jax.ShapeDtypeStruct((B, S, LANES), F32),) * 2,
        grid=(B, S // ts),
        in_specs=[pl.BlockSpec((1, ts, 1), lambda b, s: (b, s, 0)),
                  pl.BlockSpec((1, LANES), lambda b, s: (0, 0)),
                  pl.BlockSpec((1, LANES), lambda b, s: (0, 0))],
        out_specs=(spec, spec),
        compiler_params=_params(2),
        name="rope_tables",
    )(pos, freq, sign)


def _ada_kernel(c_ref, w_ref, b_ref, o_ref):
    cond = _silu(c_ref[...]).astype(BF16)
    o_ref[0] = _dot(cond, w_ref[0]) + b_ref[0]


def _ada_modulation(c, ada_w, ada_b):
    depth, D, D3 = ada_w.shape
    B = c.shape[0]
    Bp = -(-B // 16) * 16
    c_pad = jnp.pad(c, ((0, Bp - B), (0, 0)))
    tn = 1024
    out = pl.pallas_call(
        _ada_kernel,
        out_shape=jax.ShapeDtypeStruct((depth, Bp, D3), F32),
        grid=(depth, D3 // tn),
        in_specs=[pl.BlockSpec((Bp, D), lambda i, n: (0, 0)),
                  pl.BlockSpec((1, D, tn), lambda i, n: (i, 0, n)),
                  pl.BlockSpec((1, 1, tn), lambda i, n: (i, 0, n))],
        out_specs=pl.BlockSpec((1, Bp, tn), lambda i, n: (i, 0, n)),
        compiler_params=_params(2),
        name="ada_modulation",
    )(c_pad, ada_w.astype(BF16), ada_b[:, None, :])
    return out[:, :B]


def _out_proj_kernel(og_ref, w_ref, x_ref, gate_ref, fg_ref, o_ref):
    y = _dot(og_ref[...], w_ref[...])
    o_ref[...] = _rms(x_ref[...] + gate_ref[0] * y, fg_ref[...])


def _final_out_proj(og, w_out, x2, gate_c, final_g, S):
    T, D = x2.shape
    W = og.shape[1]
    tm = OUT_ROWS
    tpb = S // tm
    return pl.pallas_call(
        _out_proj_kernel,
        out_shape=jax.ShapeDtypeStruct((T, D), F32),
        grid=(T // tm,),
        in_specs=[pl.BlockSpec((tm, W), lambda i: (i, 0)),
                  pl.BlockSpec((W, D), lambda i: (0, 0)),
                  pl.BlockSpec((tm, D), lambda i: (i, 0)),
                  pl.BlockSpec((1, 1, D), lambda i: (i // tpb, 0, 0)),
                  pl.BlockSpec((1, D), lambda i: (0, 0))],
        out_specs=pl.BlockSpec((tm, D), lambda i: (i, 0)),
        compiler_params=_params(1),
        name="out_proj",
    )(og, w_out.astype(BF16), x2, gate_c[:, None, :], final_g[None, :])


def _modulated_norm(x_ref, g_ref, scale_ref, shift_ref):
    y = _rms(x_ref[...], g_ref[...])
    return (y * (1.0 + scale_ref[0]) + shift_ref[0]).astype(BF16)


def _with_pending_residual(proj_kernel, n_in, n_out):
    def kernel(x_ref, *refs, **kwargs):
        ins, (og_ref, w_ref, gate_ref) = refs[:n_in - 1], refs[n_in - 1:n_in + 2]
        outs, xo_ref = refs[n_in + 2:n_in + 2 + n_out], refs[n_in + 2 + n_out]
        xo_ref[...] = x_ref[...] + gate_ref[0] * _dot(og_ref[...], w_ref[...])
        proj_kernel(xo_ref, *ins, *outs, **kwargs)
    return kernel


def _run_proj(proj_kernel, name, S, pending, in_specs, args, out_specs, out_shape):
    x2 = args[0]
    T, D = x2.shape
    tm = PROJ_ROWS
    tpb = S // tm
    if pending is not None:
        og, w_out, gate_c = pending
        W = og.shape[1]
        proj_kernel = _with_pending_residual(proj_kernel, len(in_specs), len(out_specs))
        in_specs = in_specs + [pl.BlockSpec((tm, W), lambda i: (i, 0)),
                               pl.BlockSpec((W, D), lambda i: (0, 0)),
                               pl.BlockSpec((1, 1, D), lambda i: (i // tpb, 0, 0))]
        args = args + [og, w_out.astype(BF16), gate_c[:, None, :]]
        out_specs = out_specs + [pl.BlockSpec((tm, D), lambda i: (i, 0))]
        out_shape = out_shape + [jax.ShapeDtypeStruct((T, D), F32)]
    res = pl.pallas_call(
        proj_kernel,
        out_shape=tuple(out_shape),
        grid=(T // tm,),
        in_specs=in_specs,
        out_specs=tuple(out_specs),
        compiler_params=_params(1),
        name=name,
    )(*args)
    if pending is not None:
        return res[:-1], res[-1]
    return res, x2


def _store_heads(dst_ref, val, n_heads, width, dtype):
    for h in range(n_heads):
        dst_ref[0, h] = val[:, h * width:(h + 1) * width].astype(dtype)


def _store_heads_padded(dst_ref, val, n_heads, upper=None):
    half = LANES // 2
    lane = lax.broadcasted_iota(jnp.int32, (val.shape[0], LANES), 1)
    fill = 0.0 if upper is None else upper
    for h in range(n_heads):
        chunk = val[:, (h // 2) * LANES:(h // 2 + 1) * LANES]
        if h % 2:
            chunk = pltpu.roll(chunk, half, axis=1)
        dst_ref[0, h] = jnp.where(lane < half, chunk, fill).astype(BF16)


def _store_slabs(dst_ref, val, n_heads, d):
    for h in range(n_heads):
        for c in range(val.shape[1] // LANES):
            dst_ref[0, h, c] = val[h * d:(h + 1) * d, c * LANES:(c + 1) * LANES].astype(BF16)


def _rope_wide(val, cos, sin_signed):
    return jnp.concatenate(
        [_rope128(val[:, c * LANES:(c + 1) * LANES], cos, sin_signed)
         for c in range(val.shape[1] // LANES)], axis=1)


def _mla_proj_kernel(x_ref, scale_ref, shift_ref, g_ref, cos_ref, sin_ref,
                     wqa_ref, wkva_ref, wkpe_ref, wgate_ref, qg_ref, kvg_ref,
                     wqb_ref, wkb_ref, wvbT_ref,
                     q_ref, k_ref, vT_ref, sg_ref):
    h = _modulated_norm(x_ref, g_ref, scale_ref, shift_ref)
    cos, sin = cos_ref[0], sin_ref[0]
    head_w = MLA_NOPE + LANES
    q_scale = (MLA_NOPE + MLA_ROPE) ** -0.5 * LOG2_E

    qn = _rms(_dot(h, wqa_ref[...]), qg_ref[...]).astype(BF16)
    q = _dot(qn, wqb_ref[...]) * q_scale
    for hd in range(MLA_HEADS):
        lo = hd * head_w
        q_ref[:, lo:lo + MLA_NOPE] = q[:, lo:lo + MLA_NOPE].astype(BF16)
        q_ref[:, lo + MLA_NOPE:lo + head_w] = _rope128(
            q[:, lo + MLA_NOPE:lo + head_w], cos, sin).astype(BF16)

    kvn = _rms(_dot(h, wkva_ref[...]), kvg_ref[...]).astype(BF16)
    kn = _dot(kvn, wkb_ref[...])
    kpe = _rope128(_dot(h, wkpe_ref[...]), cos, sin).astype(BF16)
    for hd in range(MLA_HEADS):
        lo = hd * head_w
        k_ref[:, lo:lo + MLA_NOPE] = kn[:, hd * MLA_NOPE:(hd + 1) * MLA_NOPE].astype(BF16)
        k_ref[:, lo + MLA_NOPE:lo + head_w] = kpe
    _store_slabs(vT_ref, _dot_nt(wvbT_ref[...], kvn), MLA_HEADS, MLA_V)

    sg_ref[...] = _silu(_dot(h, wgate_ref[...]))


def _mla_attn_kernel(q_ref, k_ref, vT_ref, sg_ref, o_ref, s_ref, m_ref, l_ref, acc_ref):
    qi = pl.program_id(2)
    tq = tk = MLA_TQ
    slabs = tk // LANES
    head_w = MLA_NOPE + LANES
    heads = range(MLA_HEADS_PER_STEP)

    def scores(slot, kb):
        rows = pl.ds(pl.multiple_of(kb * tk, tk), tk)
        for h in heads:
            cols = slice(h * head_w, (h + 1) * head_w)
            s_ref[slot, h] = _dot_nt(k_ref[0, rows, cols], q_ref[0, :, cols])

    def absorb(slot, kb, masked):
        if masked:
            kpos = lax.broadcasted_iota(jnp.int32, (tk, tq), 0)
            qpos = lax.broadcasted_iota(jnp.int32, (tk, tq), 1)
            visible = kpos <= qpos
        for h in heads:
            s = s_ref[slot, h]
            if masked:
                s = jnp.where(visible, s, NEG_INF)
            m = m_ref[h]
            m_new = jnp.maximum(m, jnp.max(s, axis=0, keepdims=True))
            alpha = jnp.exp2(m - m_new)
            p = jnp.exp2(s - m_new)
            m_ref[h] = m_new
            l_ref[h] = alpha * l_ref[h] + jnp.sum(p, axis=0, keepdims=True)
            vt = jnp.concatenate([vT_ref[0, h, kb * slabs + c] for c in range(slabs)], axis=1)
            acc_ref[h] = alpha * acc_ref[h] + _dot(vt, p.astype(BF16))

    m_ref[...] = jnp.full(m_ref.shape, NEG_INF, F32)
    l_ref[...] = jnp.zeros(l_ref.shape, F32)
    acc_ref[...] = jnp.zeros(acc_ref.shape, F32)
    scores(0, 0)

    def pair(j, carry):
        kb = 2 * j
        scores(1, kb + 1)
        absorb(0, kb, False)
        scores(0, kb + 2)
        absorb(1, kb + 1, False)
        return carry

    lax.fori_loop(0, lax.shift_right_logical(qi, 1), pair, 0)

    @pl.when((qi & 1) == 1)
    def _():
        scores(1, qi)
        absorb(0, qi - 1, False)
        absorb(1, qi, True)

    @pl.when((qi & 1) == 0)
    def _():
        absorb(0, qi, True)

    for h in heads:
        cols = slice(h * MLA_V, (h + 1) * MLA_V)
        o = (acc_ref[h] * (1.0 / l_ref[h])).T
        o_ref[0, :, cols] = (o * sg_ref[0, :, cols]).astype(BF16)


def _mla_layer(x2, pending, B, S, norm_g, scale, shift, cos, sin, w_in, q_norm_g, kv_norm_g, w_q_b, w_kv_b):
    T, D = x2.shape
    H = MLA_HEADS
    head_w = MLA_NOPE + LANES
    o1 = MLA_Q_LORA
    o2 = o1 + MLA_KV_LORA
    o3 = o2 + MLA_ROPE
    w_qa, w_kva, w_kpe, w_gate = (w_in[:, :o1], w_in[:, o1:o2], w_in[:, o2:o3], w_in[:, o3:])
    w_kpe = jnp.pad(w_kpe, ((0, 0), (0, LANES - MLA_ROPE)))
    wq = w_q_b.reshape(MLA_Q_LORA, H, MLA_NOPE + MLA_ROPE)
    wq = jnp.pad(wq, ((0, 0), (0, 0), (0, head_w - MLA_NOPE - MLA_ROPE))).reshape(MLA_Q_LORA, H * head_w)
    wkv = w_kv_b.reshape(MLA_KV_LORA, H, MLA_NOPE + MLA_V)
    w_kb = wkv[:, :, :MLA_NOPE].reshape(MLA_KV_LORA, H * MLA_NOPE)
    w_vbT = wkv[:, :, MLA_NOPE:].reshape(MLA_KV_LORA, H * MLA_V).T
    W = H * MLA_V

    tm = PROJ_ROWS
    tpb = S // tm
    full = lambda shape: pl.BlockSpec(shape, lambda i: (0,) * len(shape))
    rows = lambda w: pl.BlockSpec((tm, w), lambda i: (i, 0))
    per_b = pl.BlockSpec((1, 1, D), lambda i: (i // tpb, 0, 0))
    tab = pl.BlockSpec((1, tm, LANES), lambda i: (i // tpb, i % tpb, 0))
    (q, k, vT, sg), x2 = _run_proj(
        _mla_proj_kernel, "mla_proj", S, pending,
        in_specs=[rows(D), per_b, per_b, full((1, D)), tab, tab,
                  full((D, o1)), full((D, MLA_KV_LORA)), full((D, LANES)), full((D, W)),
                  full((1, o1)), full((1, MLA_KV_LORA)),
                  full((o1, H * head_w)), full((MLA_KV_LORA, H * MLA_NOPE)), full((W, MLA_KV_LORA))],
        args=[x2, scale[:, None, :], shift[:, None, :], norm_g[None, :], cos, sin,
              w_qa.astype(BF16), w_kva.astype(BF16), w_kpe.astype(BF16), w_gate.astype(BF16),
              q_norm_g[None, :], kv_norm_g[None, :],
              wq.astype(BF16), w_kb.astype(BF16), w_vbT.astype(BF16)],
        out_specs=[rows(H * head_w), rows(H * head_w),
                   pl.BlockSpec((1, H, tm // LANES, MLA_V, LANES), lambda i: (i // tpb, 0, i % tpb, 0, 0)),
                   rows(W)],
        out_shape=[jax.ShapeDtypeStruct((T, H * head_w), BF16),
                   jax.ShapeDtypeStruct((T, H * head_w), BF16),
                   jax.ShapeDtypeStruct((B, H, S // LANES, MLA_V, LANES), BF16),
                   jax.ShapeDtypeStruct((T, W), F32)])

    tq = MLA_TQ
    hp = MLA_HEADS_PER_STEP
    q3 = q.reshape(B, S, H * head_w)
    k3 = k.reshape(B, S, H * head_w)
    sg3 = sg.reshape(B, S, W)
    og = pl.pallas_call(
        _mla_attn_kernel,
        out_shape=jax.ShapeDtypeStruct((B, S, W), BF16),
        grid=(B, H // hp, S // tq),
        in_specs=[pl.BlockSpec((1, tq, hp * head_w), lambda b, h, i: (b, i, h)),
                  pl.BlockSpec((1, S, hp * head_w), lambda b, h, i: (b, 0, h)),
                  pl.BlockSpec((1, hp, S // LANES, MLA_V, LANES), lambda b, h, i: (b, h, 0, 0, 0)),
                  pl.BlockSpec((1, tq, hp * MLA_V), lambda b, h, i: (b, i, h))],
        out_specs=pl.BlockSpec((1, tq, hp * MLA_V), lambda b, h, i: (b, i, h)),
        scratch_shapes=[pltpu.VMEM((2, hp, tq, tq), F32),
                        pltpu.VMEM((hp, 1, tq), F32), pltpu.VMEM((hp, 1, tq), F32),
                        pltpu.VMEM((hp, MLA_V, tq), F32)],
        compiler_params=_params(3),
        name="mla_attn",
    )(q3, k3, vT, sg3)
    return og.reshape(T, W), x2


def _swa_proj_kernel(x_ref, scale_ref, shift_ref, g_ref, cos_ref, sin_ref,
                     wq_ref, wk_ref, wvT_ref, wgate_ref,
                     q_ref, k_ref, vT_ref, sg_ref):
    h = _modulated_norm(x_ref, g_ref, scale_ref, shift_ref)
    cos, sin = cos_ref[0], sin_ref[0]
    q = _rope_wide(_dot(h, wq_ref[...]) * (SWA_HEAD_DIM ** -0.5 * LOG2_E), cos, sin)
    _store_heads(q_ref, q, SWA_HEADS, SWA_HEAD_DIM, BF16)
    k = _rope_wide(_dot(h, wk_ref[...]), cos, sin)
    _store_heads(k_ref, k, SWA_KV_HEADS, SWA_HEAD_DIM, BF16)
    _store_slabs(vT_ref, _dot_nt(wvT_ref[...], h), SWA_KV_HEADS, SWA_HEAD_DIM)
    sg_ref[...] = _silu(_dot(h, wgate_ref[...]))


def _band_valid(diff, window):
    return lax.bitcast_convert_type(diff, jnp.uint32) < jnp.uint32(window)


def _swa_attn_kernel(q_ref, k_ref, vT_ref, sink_ref, sg_ref, o_ref):
    qi = pl.program_id(1)
    tq = SWA_TQ
    G = SWA_HEADS // SWA_KV_HEADS
    d = SWA_HEAD_DIM
    n = G * tq
    span = SWA_WINDOW + tq
    groups = range(SWA_KV_HEADS)
    blk0 = jnp.maximum(qi - SWA_WINDOW // tq, 0)
    base = pl.multiple_of(blk0 * LANES, LANES)
    s_all = [_dot_nt(k_ref[0, i, pl.ds(base, span), :], q_ref[0, i * G:(i + 1) * G].reshape(n, d))
             for i in groups]
    kpos = base + lax.broadcasted_iota(jnp.int32, (span, n), 0)
    qpos = qi * tq + (lax.broadcasted_iota(jnp.int32, (span, n), 1) & (tq - 1))
    in_band = _band_valid(qpos - kpos, SWA_WINDOW)
    for i in groups:
        s = jnp.where(in_band, s_all[i], NEG_INF)
        sink = sink_ref[i]
        m = jnp.maximum(jnp.max(s, axis=0, keepdims=True), sink)
        e = jnp.exp2(s - m)
        denom = jnp.sum(e, axis=0, keepdims=True) + jnp.exp2(sink - m)
        vt = jnp.concatenate([vT_ref[0, i, blk0 + c] for c in range(span // LANES)], axis=1)
        oT = _dot(vt, e.astype(BF16)) * (1.0 / denom)
        o = jnp.concatenate([oT[:, g * tq:(g + 1) * tq] for g in range(G)], axis=0).T
        cols = slice(i * G * d, (i + 1) * G * d)
        o_ref[0, :, cols] = (o * sg_ref[0, :, cols]).astype(BF16)


def _swa_layer(x2, pending, B, S, norm_g, scale, shift, cos, sin, w_in, sinks):
    T, D = x2.shape
    H, KV, d = SWA_HEADS, SWA_KV_HEADS, SWA_HEAD_DIM
    G = H // KV
    W = H * d
    KW = KV * d
    w_q, w_k, w_v, w_gate = (w_in[:, :W], w_in[:, W:W + KW], w_in[:, W + KW:W + 2 * KW],
                             w_in[:, W + 2 * KW:])
    tm = PROJ_ROWS
    tpb = S // tm
    full = lambda shape: pl.BlockSpec(shape, lambda i: (0,) * len(shape))
    rows = lambda w: pl.BlockSpec((tm, w), lambda i: (i, 0))
    per_b = pl.BlockSpec((1, 1, D), lambda i: (i // tpb, 0, 0))
    tab = pl.BlockSpec((1, tm, LANES), lambda i: (i // tpb, i % tpb, 0))
    heads = lambda nh: pl.BlockSpec((1, nh, tm, d), lambda i: (i // tpb, 0, i % tpb, 0))
    (q, k, vT, sg), x2 = _run_proj(
        _swa_proj_kernel, "swa_proj", S, pending,
        in_specs=[rows(D), per_b, per_b, full((1, D)), tab, tab,
                  full((D, W)), full((D, KW)), full((KW, D)), full((D, W))],
        args=[x2, scale[:, None, :], shift[:, None, :], norm_g[None, :], cos, sin,
              w_q.astype(BF16), w_k.astype(BF16), w_v.T.astype(BF16), w_gate.astype(BF16)],
        out_specs=[heads(H), heads(KV),
                   pl.BlockSpec((1, KV, tm // LANES, d, LANES), lambda i: (i // tpb, 0, i % tpb, 0, 0)),
                   rows(W)],
        out_shape=[jax.ShapeDtypeStruct((B, H, S, d), BF16),
                   jax.ShapeDtypeStruct((B, KV, S, d), BF16),
                   jax.ShapeDtypeStruct((B, KV, S // LANES, d, LANES), BF16),
                   jax.ShapeDtypeStruct((T, W), F32)])

    tq = SWA_TQ
    n = G * tq
    sink_rows = jnp.repeat(sinks.astype(F32).reshape(KV, G) * LOG2_E, tq, axis=1)[:, None, :]
    og = pl.pallas_call(
        _swa_attn_kernel,
        out_shape=jax.ShapeDtypeStruct((B, S, W), BF16),
        grid=(B, S // tq),
        in_specs=[pl.BlockSpec((1, H, tq, d), lambda b, i: (b, 0, i, 0)),
                  pl.BlockSpec((1, KV, S, d), lambda b, i: (b, 0, 0, 0)),
                  pl.BlockSpec((1, KV, S // LANES, d, LANES), lambda b, i: (b, 0, 0, 0, 0)),
                  pl.BlockSpec((KV, 1, n), lambda b, i: (0, 0, 0)),
                  pl.BlockSpec((1, tq, W), lambda b, i: (b, i, 0))],
        out_specs=pl.BlockSpec((1, tq, W), lambda b, i: (b, i, 0)),
        compiler_params=_params(2),
        name="swa_attn",
    )(q, k, vT, sink_rows, sg.reshape(B, S, W))
    return og.reshape(T, W), x2


def _nsa_proj_kernel(x_ref, scale_ref, shift_ref, g_ref, cos_ref, sin_ref,
                     wq_ref, wkc_ref, wvc_ref, wks_ref, wvsT_ref, wkw_ref, wvwT_ref, wgT_ref, wgate_ref,
                     q_ref, kc_ref, vc_ref, ks_ref, vsT_ref, kw_ref, vwT_ref, gT_ref, sg_ref, *, tiles_per_seq):
    h = _modulated_norm(x_ref, g_ref, scale_ref, shift_ref)
    cos, sin = cos_ref[0], sin_ref[0]
    KV, d = NSA_KV_HEADS, NSA_HEAD_DIM
    tm = h.shape[0]
    q = _rope_wide(_dot(h, wq_ref[...]) * (d ** -0.5 * LOG2_E), cos, sin)
    _store_heads_padded(q_ref, q, NSA_HEADS)
    _store_heads(kc_ref, _dot(h, wkc_ref[...]), KV, d, F32)
    _store_heads(vc_ref, _dot(h, wvc_ref[...]), KV, d, F32)
    pos = (pl.program_id(0) % tiles_per_seq) * tm + lax.broadcasted_iota(jnp.int32, (tm, LANES), 0)
    lane = lax.broadcasted_iota(jnp.int32, (tm, LANES), 1)
    sel_shift = NSA_SEL_BLOCK.bit_length() - 1
    block_onehot = jnp.where(lane - LANES // 2 == lax.shift_right_logical(pos, sel_shift), 1.0, 0.0)
    _store_heads_padded(ks_ref, _rope_wide(_dot(h, wks_ref[...]), cos, sin), KV, upper=block_onehot)
    _store_heads_padded(kw_ref, _rope_wide(_dot(h, wkw_ref[...]), cos, sin), KV)
    _store_slabs(vsT_ref, _dot_nt(wvsT_ref[...], h), KV, d)
    _store_slabs(vwT_ref, _dot_nt(wvwT_ref[...], h), KV, d)
    gT = _sigmoid(_dot_nt(wgT_ref[...], h))
    for kv in range(KV):
        gT_ref[0, kv] = gT[kv * 16:(kv + 1) * 16, :]
    sg_ref[...] = _silu(_dot(h, wgate_ref[...]))


def _nsa_compress_kernel(kc_ref, vc_ref, pe_ref, wk1_ref, wk2_ref, wv1_ref, wv2T_ref, kout_ref, vT_ref):
    n_chunks = kout_ref.shape[2]
    stride = NSA_CMP_STRIDE

    def hidden(x_ref, w1_ref):
        top = bot = None
        for l in range(stride):
            x_l = x_ref[0, 0, pl.ds(l, n_chunks, stride=stride), :]
            t = _dot((x_l + pe_ref[l]).astype(BF16), w1_ref[l])
            b = _dot((x_l + pe_ref[stride + l]).astype(BF16), w1_ref[stride + l])
            top = t if top is None else top + t
            bot = b if bot is None else bot + b
        return _silu(top + pltpu.roll(bot, n_chunks - 1, axis=0)).astype(BF16)

    hk = hidden(kc_ref, wk1_ref)
    kc = _dot(hk, wk2_ref[...])
    row = lax.broadcasted_iota(jnp.int32, kc.shape, 0)
    kout_ref[0, 0] = jnp.where(row < n_chunks - 1, kc, 0.0).astype(BF16)
    hv = hidden(vc_ref, wv1_ref)
    vT = _dot_nt(wv2T_ref[...], hv)
    col = lax.broadcasted_iota(jnp.int32, vT.shape, 1)
    vT_ref[0, 0] = jnp.where(col < n_chunks - 1, vT, 0.0).astype(BF16)


def _nsa_attn_kernel(q_ref, kc_ref, vcT_ref, ovT_ref, ks_ref, vsT_ref, kw_ref, vwT_ref, gT_ref, sg_ref,
                     o_ref, qsel_ref, s_ref, m_ref, l_ref, acc_ref, oc_ref, ow_ref, *, n_top, n_q_tiles):
    qi = pl.program_id(2)
    tq = NSA_TQ
    G = NSA_HEADS // NSA_KV_HEADS
    d = NSA_HEAD_DIM
    n = G * tq
    groups = range(NSA_KV_PER_STEP)
    q0 = qi * tq
    q = [q_ref[0, i * G:(i + 1) * G].reshape(n, LANES) for i in groups]

    def col_qpos(shape):
        return q0 + (lax.broadcasted_iota(jnp.int32, shape, 1) & (tq - 1))

    tk = NSA_SEL_TK
    slabs = tk // LANES

    def sel_scores(slot, kb):
        rows = pl.ds(pl.multiple_of(kb * tk, tk), tk)
        for i in groups:
            s_ref[slot, i] = _dot_nt(ks_ref[0, i, rows, :], qsel_ref[i])

    def sel_absorb(slot, kb, diagonal):
        if diagonal:
            kpos = kb * tk + lax.broadcasted_iota(jnp.int32, (tk, n), 0)
            keep = kpos <= col_qpos((tk, n))
        for i in groups:
            s = s_ref[slot, i]
            if diagonal:
                s = jnp.where(keep, s, NEG_INF)
            m = m_ref[i]
            m_new = jnp.maximum(m, jnp.max(s, axis=0, keepdims=True))
            alpha = jnp.exp2(m - m_new)
            p = jnp.exp2(s - m_new)
            m_ref[i] = m_new
            l_ref[i] = alpha * l_ref[i] + jnp.sum(p, axis=0, keepdims=True)
            vt = jnp.concatenate([vsT_ref[0, i, kb * slabs + c] for c in range(slabs)], axis=1)
            acc_ref[i] = alpha * acc_ref[i] + _dot(vt, p.astype(BF16))

    m_ref[...] = jnp.full(m_ref.shape, NEG_INF, F32)
    l_ref[...] = jnp.zeros(l_ref.shape, F32)
    acc_ref[...] = jnp.zeros(acc_ref.shape, F32)
    kb_diag = lax.shift_right_logical(q0, tk.bit_length() - 1)
    span = NSA_WINDOW + tq
    blk0 = jnp.maximum(qi * (tq // LANES) - NSA_WINDOW // LANES, 0)
    base = pl.multiple_of(blk0 * LANES, LANES)
    sel_shift = NSA_SEL_BLOCK.bit_length() - 1
    half = LANES // 2

    def front(ns_eff, nc_eff):
        s_cmp = [_dot_nt(kc_ref[0, i, :nc_eff, :], q[i]) for i in groups]
        s_win = [_dot_nt(kw_ref[0, i, pl.ds(base, span), :], q[i]) for i in groups]

        cmp_end = lax.broadcasted_iota(jnp.int32, (nc_eff, n), 0) * NSA_CMP_STRIDE + (NSA_CMP_BLOCK - 1)
        valid = cmp_end <= col_qpos((nc_eff, n))
        imp = []
        for i in groups:
            s = jnp.where(valid, s_cmp[i], NEG_INF)
            m = jnp.max(s, axis=0, keepdims=True)
            e = jnp.where(valid, jnp.exp2(s - m), 0.0)
            l = jnp.sum(e, axis=0, keepdims=True)
            p = (e * jnp.where(l > 0.0, 1.0 / l, 0.0)).astype(BF16)
            oc_ref[i] = _dot(vcT_ref[0, i, :, :nc_eff], p)
            imp_all = _dot(ovT_ref[:ns_eff, :nc_eff], p)
            acc = imp_all[:, 0:tq]
            for g in range(1, G):
                acc = acc + imp_all[:, g * tq:(g + 1) * tq]
            imp.append(acc)

        blk = lax.broadcasted_iota(jnp.int32, (ns_eff, tq), 0)
        q_blk = lax.shift_right_logical(q0 + lax.broadcasted_iota(jnp.int32, (ns_eff, tq), 1), sel_shift)
        causal = blk <= q_blk
        forced = (blk == 0) | (blk == q_blk) | (blk == q_blk - 1)
        lane = lax.broadcasted_iota(jnp.int32, (G, tq, LANES), 2)
        for i in groups:
            val = jnp.where(causal, imp[i] + jnp.where(forced, NSA_FORCE_BONUS, 0.0), -1.0)
            rank = jnp.zeros(val.shape, F32)
            for r in range(ns_eff):
                row = val[r:r + 1, :]
                rank = rank + jnp.where(blk > r, jnp.where(row >= val, 1.0, 0.0), jnp.where(row > val, 1.0, 0.0))
            bias = jnp.where(causal, jnp.where(rank < n_top, 0.0, NEG_INF), NEG_INF)
            bias_rows = [jnp.zeros((half, tq), F32), bias]
            if ns_eff < half:
                bias_rows.append(jnp.full((half - ns_eff, tq), NEG_INF, F32))
            bias_rows = jnp.concatenate(bias_rows, axis=0).T.astype(BF16)
            qsel_ref[i] = jnp.where(lane < half, q[i].reshape(G, tq, LANES), bias_rows[None]).reshape(n, LANES)

        sel_scores(0, 0)

        diff = col_qpos((span, n)) - (base + lax.broadcasted_iota(jnp.int32, (span, n), 0))
        in_band = _band_valid(diff, NSA_WINDOW)
        for i in groups:
            s = jnp.where(in_band, s_win[i], NEG_INF)
            e = jnp.exp2(s - jnp.max(s, axis=0, keepdims=True))
            vt = jnp.concatenate([vwT_ref[0, i, blk0 + c] for c in range(span // LANES)], axis=1)
            ow_ref[i] = _dot(vt, e.astype(BF16)) * (1.0 / jnp.sum(e, axis=0, keepdims=True))

    ns, nc_pad = ovT_ref.shape
    n_var = min(NSA_FRONT_VARIANTS, n_q_tiles)
    per = n_q_tiles // n_var
    for v in range(n_var):
        ns_eff = min(ns, -(-(ns * (v + 1)) // (n_var * 16)) * 16)
        nc_eff = min(nc_pad, -(-(nc_pad * (v + 1)) // (n_var * LANES)) * LANES)
        hi = n_q_tiles if v == n_var - 1 else (v + 1) * per

        @pl.when((qi >= v * per) & (qi < hi))
        def _(ns_eff=ns_eff, nc_eff=nc_eff):
            front(ns_eff, nc_eff)

    def sel_pair(j, carry):
        kb = 2 * j
        sel_scores(1, kb + 1)
        sel_absorb(0, kb, False)
        sel_scores(0, kb + 2)
        sel_absorb(1, kb + 1, False)
        return carry

    lax.fori_loop(0, lax.shift_right_logical(kb_diag, 1), sel_pair, 0)

    @pl.when((kb_diag & 1) == 1)
    def _():
        sel_scores(1, kb_diag)
        sel_absorb(0, kb_diag - 1, False)
        sel_absorb(1, kb_diag, True)

    @pl.when((kb_diag & 1) == 0)
    def _():
        sel_absorb(0, kb_diag, True)

    for i in groups:
        o_s = acc_ref[i] * (1.0 / l_ref[i])
        gates = gT_ref[0, i]
        outs = []
        for g in range(G):
            cols = slice(g * tq, (g + 1) * tq)
            outs.append(gates[3 * g:3 * g + 1, :] * oc_ref[i][:, cols]
                        + gates[3 * g + 1:3 * g + 2, :] * o_s[:, cols]
                        + gates[3 * g + 2:3 * g + 3, :] * ow_ref[i][:, cols])
        o = jnp.concatenate(outs, axis=0).T
        cols = slice(i * G * d, (i + 1) * G * d)
        o_ref[0, :, cols] = (o * sg_ref[0, :, cols]).astype(BF16)


def _nsa_overlap_T(nc_pad, ns):
    nc = nc_pad - (NSA_CMP_BLOCK // NSA_CMP_STRIDE - 1)
    cs = np.arange(nc_pad)[None, :] * NSA_CMP_STRIDE
    ss = np.arange(ns)[:, None] * NSA_SEL_BLOCK
    ov = np.clip(np.minimum(cs + NSA_CMP_BLOCK, ss + NSA_SEL_BLOCK) - np.maximum(cs, ss), 0, None)
    ov = np.where(np.arange(nc_pad)[None, :] < nc, ov, 0)
    return jnp.asarray(ov / NSA_CMP_BLOCK, dtype=BF16)


def _nsa_layer(x2, pending, B, S, norm_g, scale, shift, cos, sin, w_in, cmp_pos, w_k1, w_k2, w_v1, w_v2):
    T, D = x2.shape
    H, KV, d = NSA_HEADS, NSA_KV_HEADS, NSA_HEAD_DIM
    G = H // KV
    W = H * d
    KW = KV * d
    offs = np.cumsum([0, W] + [KW] * 6 + [3 * H, W])
    w_q, w_kc, w_vc, w_ks, w_vs, w_kw, w_vw, w_g, w_gate = (
        w_in[:, offs[i]:offs[i + 1]] for i in range(9))
    w_gT = jnp.pad(w_g.T.reshape(KV, 3 * G, D), ((0, 0), (0, 16 - 3 * G), (0, 0))).reshape(KV * 16, D)

    tm = PROJ_ROWS
    tpb = S // tm
    full = lambda shape: pl.BlockSpec(shape, lambda i: (0,) * len(shape))
    rows = lambda w: pl.BlockSpec((tm, w), lambda i: (i, 0))
    per_b = pl.BlockSpec((1, 1, D), lambda i: (i // tpb, 0, 0))
    tab = pl.BlockSpec((1, tm, LANES), lambda i: (i // tpb, i % tpb, 0))
    heads = lambda nh: pl.BlockSpec((1, nh, tm, d), lambda i: (i // tpb, 0, i % tpb, 0))
    wide = lambda nh: pl.BlockSpec((1, nh, tm, LANES), lambda i: (i // tpb, 0, i % tpb, 0))
    slab = pl.BlockSpec((1, KV, tm // LANES, d, LANES), lambda i: (i // tpb, 0, i % tpb, 0, 0))
    kv_f32 = jax.ShapeDtypeStruct((B, KV, S, d), F32)
    kv_bf16 = jax.ShapeDtypeStruct((B, KV, S, LANES), BF16)
    kv_slab = jax.ShapeDtypeStruct((B, KV, S // LANES, d, LANES), BF16)
    assert S // NSA_SEL_BLOCK <= LANES // 2, "selection-block one-hot must fit in lanes 64..127"
    (q, kc, vc, ks, vsT, kw, vwT, gT, sg), x2 = _run_proj(
        functools.partial(_nsa_proj_kernel, tiles_per_seq=tpb), "nsa_proj", S, pending,
        in_specs=[rows(D), per_b, per_b, full((1, D)), tab, tab,
                  full((D, W)), full((D, KW)), full((D, KW)), full((D, KW)), full((KW, D)),
                  full((D, KW)), full((KW, D)), full((KV * 16, D)), full((D, W))],
        args=[x2, scale[:, None, :], shift[:, None, :], norm_g[None, :], cos, sin,
              w_q.astype(BF16), w_kc.astype(BF16), w_vc.astype(BF16), w_ks.astype(BF16),
              w_vs.T.astype(BF16), w_kw.astype(BF16), w_vw.T.astype(BF16), w_gT.astype(BF16),
              w_gate.astype(BF16)],
        out_specs=[wide(H), heads(KV), heads(KV), wide(KV), slab, wide(KV), slab,
                   pl.BlockSpec((1, KV, 16, tm), lambda i: (i // tpb, 0, 0, i % tpb)),
                   rows(W)],
        out_shape=[jax.ShapeDtypeStruct((B, H, S, LANES), BF16), kv_f32, kv_f32,
                   kv_bf16, kv_slab, kv_bf16, kv_slab,
                   jax.ShapeDtypeStruct((B, KV, 16, S), F32),
                   jax.ShapeDtypeStruct((T, W), F32)])

    n_chunks = S // NSA_CMP_STRIDE
    nl = NSA_CMP_BLOCK
    token_spec = pl.BlockSpec((1, 1, S, d), lambda b, kv: (b, kv, 0, 0))
    full2 = lambda shape: pl.BlockSpec(shape, lambda b, kv: (0,) * len(shape))
    k_c, v_cT = pl.pallas_call(
        _nsa_compress_kernel,
        out_shape=(jax.ShapeDtypeStruct((B, KV, n_chunks, LANES), BF16),
                   jax.ShapeDtypeStruct((B, KV, d, n_chunks), BF16)),
        grid=(B, KV),
        in_specs=[token_spec, token_spec, full2((nl, 1, d)),
                  full2((nl, d, NSA_CMP_HIDDEN)), full2((NSA_CMP_HIDDEN, LANES)),
                  full2((nl, d, NSA_CMP_HIDDEN)), full2((d, NSA_CMP_HIDDEN))],
        out_specs=(pl.BlockSpec((1, 1, n_chunks, LANES), lambda b, kv: (b, kv, 0, 0)),
                   pl.BlockSpec((1, 1, d, n_chunks), lambda b, kv: (b, kv, 0, 0))),
        compiler_params=_params(2),
        name="nsa_compress",
    )(kc, vc, cmp_pos[:, None, :],
      w_k1.reshape(nl, d, NSA_CMP_HIDDEN).astype(BF16),
      jnp.pad(w_k2, ((0, 0), (0, LANES - d))).astype(BF16),
      w_v1.reshape(nl, d, NSA_CMP_HIDDEN).astype(BF16), w_v2.T.astype(BF16))

    tq = NSA_TQ
    ns = S // NSA_SEL_BLOCK
    ovT = _nsa_overlap_T(n_chunks, ns)
    kp = NSA_KV_PER_STEP
    whole = lambda shape: pl.BlockSpec((1, kp) + shape, lambda b, kv, i: (b, kv) + (0,) * len(shape))
    og = pl.pallas_call(
        functools.partial(_nsa_attn_kernel, n_top=min(NSA_N_SELECT, ns), n_q_tiles=S // tq),
        out_shape=jax.ShapeDtypeStruct((B, S, W), BF16),
        grid=(B, KV // kp, S // tq),
        in_specs=[pl.BlockSpec((1, kp * G, tq, LANES), lambda b, kv, i: (b, kv, i, 0)),
                  whole((n_chunks, LANES)), whole((d, n_chunks)),
                  pl.BlockSpec((ns, n_chunks), lambda b, kv, i: (0, 0)),
                  whole((S, LANES)), whole((S // LANES, d, LANES)),
                  whole((S, LANES)), whole((S // LANES, d, LANES)),
                  pl.BlockSpec((1, kp, 16, tq), lambda b, kv, i: (b, kv, 0, i)),
                  pl.BlockSpec((1, tq, kp * G * d), lambda b, kv, i: (b, i, kv))],
        out_specs=pl.BlockSpec((1, tq, kp * G * d), lambda b, kv, i: (b, i, kv)),
        scratch_shapes=[pltpu.VMEM((kp, G * tq, LANES), BF16),
                        pltpu.VMEM((2, kp, NSA_SEL_TK, G * tq), F32),
                        pltpu.VMEM((kp, 1, G * tq), F32), pltpu.VMEM((kp, 1, G * tq), F32),
                        pltpu.VMEM((kp, d, G * tq), F32), pltpu.VMEM((kp, d, G * tq), F32),
                        pltpu.VMEM((kp, d, G * tq), F32)],
        compiler_params=_params(3),
        name="nsa_attn",
    )(q, k_c, v_cT, ovT, ks, vsT, kw, vwT, gT, sg.reshape(B, S, W))
    return og.reshape(T, W), x2


def kernel(x, c, positions, norm_g, ada_w, ada_b, mla_w_in, mla_q_norm_g, mla_kv_norm_g, mla_w_q_b, mla_w_kv_b, mla_w_out, swa_w_in, swa_sinks, swa_w_out, nsa_w_in, nsa_cmp_pos, nsa_w_cmp_k1, nsa_w_cmp_k2, nsa_w_cmp_v1, nsa_w_cmp_v2, nsa_w_out, final_norm_g):
    B, S, D = x.shape
    depth = norm_g.shape[0]
    cos, sin = _rope_tables(positions)
    mod = _ada_modulation(c, ada_w, ada_b)
    x2 = x.reshape(B * S, D)
    pending = None
    for i in range(depth):
        shift, scale, gate = mod[i, :, :D], mod[i, :, D:2 * D], mod[i, :, 2 * D:]
        kind, j = i % 3, i // 3
        if kind == 0:
            og, x2 = _mla_layer(x2, pending, B, S, norm_g[i], scale, shift, cos, sin, mla_w_in[j],
                                mla_q_norm_g[j], mla_kv_norm_g[j], mla_w_q_b[j], mla_w_kv_b[j])
            w_out = mla_w_out[j]
        elif kind == 1:
            og, x2 = _swa_layer(x2, pending, B, S, norm_g[i], scale, shift, cos, sin, swa_w_in[j], swa_sinks[j])
            w_out = swa_w_out[j]
        else:
            og, x2 = _nsa_layer(x2, pending, B, S, norm_g[i], scale, shift, cos, sin, nsa_w_in[j],
                                nsa_cmp_pos[j], nsa_w_cmp_k1[j], nsa_w_cmp_k2[j], nsa_w_cmp_v1[j],
                                nsa_w_cmp_v2[j])
            w_out = nsa_w_out[j]
        pending = (og, w_out, gate)
    return _final_out_proj(*pending[:2], x2, pending[2], final_norm_g, S).reshape(B, S, D)
```

```python
import functools

import jax
import jax.numpy as jnp
import numpy as np
from jax import lax
from jax.experimental import pallas as pl
from jax.experimental.pallas import tpu as pltpu

F32 = jnp.float32
BF16 = jnp.bfloat16

ROPE_THETA = 10000.0
RMS_EPS = 1e-6
NEG_INF = -1e30
LOG2_E = 1.4426950408889634

MLA_HEADS = 8
MLA_NOPE = 128
MLA_ROPE = 64
MLA_V = 128
MLA_Q_LORA = 256
MLA_KV_LORA = 128

SWA_HEADS = 16
SWA_KV_HEADS = 2
SWA_HEAD_DIM = 64
SWA_WINDOW = 128

NSA_HEADS = 16
NSA_KV_HEADS = 4
NSA_HEAD_DIM = 64
NSA_CMP_BLOCK = 32
NSA_CMP_STRIDE = 16
NSA_CMP_HIDDEN = 128
NSA_SEL_BLOCK = 64
NSA_N_SELECT = 16
NSA_WINDOW = 512
NSA_FORCE_BONUS = 1e4

LANES = 128
ROPE_HALF = 32
VMEM_LIMIT = 56 * 1024 * 1024

PROJ_ROWS = 512
OUT_ROWS = 512
MLA_TQ = 512
MLA_HEADS_PER_STEP = 4
SWA_TQ = 128
NSA_TQ = 128
NSA_SEL_TK = 512
NSA_KV_PER_STEP = 4
NSA_FRONT_VARIANTS = 4


def _dot(a, b):
    return jnp.dot(a, b, preferred_element_type=F32)


def _dot_nt(a, b):
    return lax.dot_general(a, b, (((1,), (1,)), ((), ())), preferred_element_type=F32)


def _silu(x):
    return x / (1.0 + jnp.exp(-x))


def _sigmoid(x):
    return 1.0 / (1.0 + jnp.exp(-x))


def _rms(x, g):
    return x * lax.rsqrt(jnp.mean(x * x, axis=-1, keepdims=True) + RMS_EPS) * g


def _params(n_axes):
    return pltpu.CompilerParams(dimension_semantics=("arbitrary",) * n_axes,
                                vmem_limit_bytes=VMEM_LIMIT)


def _rope128(x, cos, sin_signed):
    lane = lax.broadcasted_iota(jnp.int32, x.shape, 1)
    lower_half = (lane & (2 * ROPE_HALF - 1)) < ROPE_HALF
    partner = jnp.where(lower_half, pltpu.roll(x, LANES - ROPE_HALF, axis=1),
                        pltpu.roll(x, ROPE_HALF, axis=1))
    return x * cos + partner * sin_signed


def _rope_table_kernel(pos_ref, freq_ref, cos_ref, sin_ref):
    ang = pos_ref[0] * freq_ref[...]
    cos_ref[0] = jnp.cos(ang)
    sin_ref[0] = jnp.sin(ang)


def _rope_tables(positions):
    B, S = positions.shape
    per_row = LANES // ROPE_HALF
    inv_freq = ROPE_THETA ** (-jnp.arange(ROPE_HALF, dtype=F32) / ROPE_HALF)
    freq = jnp.tile(inv_freq, per_row)[None, :]
    pos = jnp.repeat(positions.astype(F32).reshape(B, S // per_row, per_row), ROPE_HALF, axis=2)
    rows = S // per_row
    ts = min(rows, 512)
    spec = pl.BlockSpec((1, ts, LANES), lambda b, s: (b, s, 0))
    cos, sin = pl.pallas_call(
        _rope_table_kernel,
        out_shape=(jax.ShapeDtypeStruct((B, rows, LANES), F32),) * 2,
        grid=(B, rows // ts),
        in_specs=[spec, pl.BlockSpec((1, LANES), lambda b, s: (0, 0))],
        out_specs=(spec, spec),
        compiler_params=_params(2),
        name="rope_tables",
    )(pos, freq)
    sign = jnp.tile(jnp.concatenate([-jnp.ones(ROPE_HALF, F32), jnp.ones(ROPE_HALF, F32)]),
                    LANES // (2 * ROPE_HALF))
    spread = lambda t: jnp.tile(t.reshape(B, S, ROPE_HALF), (1, 1, per_row))
    return spread(cos), spread(sin) * sign


def _ada_kernel(c_ref, w_ref, b_ref, o_ref):
    cond = _silu(c_ref[...]).astype(BF16)
    o_ref[0] = _dot(cond, w_ref[0]) + b_ref[0]


def _ada_modulation(c, ada_w, ada_b):
    depth, D, D3 = ada_w.shape
    B = c.shape[0]
    Bp = -(-B // 16) * 16
    c_pad = jnp.pad(c, ((0, Bp - B), (0, 0)))
    tn = 1024
    out = pl.pallas_call(
        _ada_kernel,
        out_shape=jax.ShapeDtypeStruct((depth, Bp, D3), F32),
        grid=(depth, D3 // tn),
        in_specs=[pl.BlockSpec((Bp, D), lambda i, n: (0, 0)),
                  pl.BlockSpec((1, D, tn), lambda i, n: (i, 0, n)),
                  pl.BlockSpec((1, 1, tn), lambda i, n: (i, 0, n))],
        out_specs=pl.BlockSpec((1, Bp, tn), lambda i, n: (i, 0, n)),
        compiler_params=_params(2),
        name="ada_modulation",
    )(c_pad, ada_w.astype(BF16), ada_b[:, None, :])
    return out[:, :B]


def _out_proj_kernel(og_ref, w_ref, x_ref, gate_ref, fg_ref, o_ref):
    y = _dot(og_ref[...], w_ref[...])
    o_ref[...] = _rms(x_ref[...] + gate_ref[0] * y, fg_ref[...])


def _final_out_proj(og, w_out, x2, gate_c, final_g, S):
    T, D = x2.shape
    W = og.shape[1]
    tm = OUT_ROWS
    tpb = S // tm
    return pl.pallas_call(
        _out_proj_kernel,
        out_shape=jax.ShapeDtypeStruct((T, D), F32),
        grid=(T // tm,),
        in_specs=[pl.BlockSpec((tm, W), lambda i: (i, 0)),
                  pl.BlockSpec((W, D), lambda i: (0, 0)),
                  pl.BlockSpec((tm, D), lambda i: (i, 0)),
                  pl.BlockSpec((1, 1, D), lambda i: (i // tpb, 0, 0)),
                  pl.BlockSpec((1, D), lambda i: (0, 0))],
        out_specs=pl.BlockSpec((tm, D), lambda i: (i, 0)),
        compiler_params=_params(1),
        name="out_proj",
    )(og, w_out.astype(BF16), x2, gate_c[:, None, :], final_g[None, :])


def _modulated_norm(x_ref, g_ref, scale_ref, shift_ref):
    y = _rms(x_ref[...], g_ref[...])
    return (y * (1.0 + scale_ref[0]) + shift_ref[0]).astype(BF16)


def _with_pending_residual(proj_kernel, n_in, n_out):
    def kernel(x_ref, *refs, **kwargs):
        ins, (og_ref, w_ref, gate_ref) = refs[:n_in - 1], refs[n_in - 1:n_in + 2]
        outs, xo_ref = refs[n_in + 2:n_in + 2 + n_out], refs[n_in + 2 + n_out]
        xo_ref[...] = x_ref[...] + gate_ref[0] * _dot(og_ref[...], w_ref[...])
        proj_kernel(xo_ref, *ins, *outs, **kwargs)
    return kernel


def _run_proj(proj_kernel, name, S, pending, in_specs, args, out_specs, out_shape):
    x2 = args[0]
    T, D = x2.shape
    tm = PROJ_ROWS
    tpb = S // tm
    if pending is not None:
        og, w_out, gate_c = pending
        W = og.shape[1]
        proj_kernel = _with_pending_residual(proj_kernel, len(in_specs), len(out_specs))
        in_specs = in_specs + [pl.BlockSpec((tm, W), lambda i: (i, 0)),
                               pl.BlockSpec((W, D), lambda i: (0, 0)),
                               pl.BlockSpec((1, 1, D), lambda i: (i // tpb, 0, 0))]
        args = args + [og, w_out.astype(BF16), gate_c[:, None, :]]
        out_specs = out_specs + [pl.BlockSpec((tm, D), lambda i: (i, 0))]
        out_shape = out_shape + [jax.ShapeDtypeStruct((T, D), F32)]
    res = pl.pallas_call(
        proj_kernel,
        out_shape=tuple(out_shape),
        grid=(T // tm,),
        in_specs=in_specs,
        out_specs=tuple(out_specs),
        compiler_params=_params(1),
        name=name,
    )(*args)
    if pending is not None:
        return res[:-1], res[-1]
    return res, x2


def _store_heads(dst_ref, val, n_heads, width, dtype):
    for h in range(n_heads):
        dst_ref[0, h] = val[:, h * width:(h + 1) * width].astype(dtype)


def _store_heads_padded(dst_ref, val, n_heads, upper=None):
    half = LANES // 2
    lane = lax.broadcasted_iota(jnp.int32, (val.shape[0], LANES), 1)
    fill = 0.0 if upper is None else upper
    for h in range(n_heads):
        chunk = val[:, (h // 2) * LANES:(h // 2 + 1) * LANES]
        if h % 2:
            chunk = pltpu.roll(chunk, half, axis=1)
        dst_ref[0, h] = jnp.where(lane < half, chunk, fill).astype(BF16)


def _store_slabs(dst_ref, val, n_heads, d):
    for h in range(n_heads):
        for c in range(val.shape[1] // LANES):
            dst_ref[0, h, c] = val[h * d:(h + 1) * d, c * LANES:(c + 1) * LANES].astype(BF16)


def _rope_wide(val, cos, sin_signed):
    return jnp.concatenate(
        [_rope128(val[:, c * LANES:(c + 1) * LANES], cos, sin_signed)
         for c in range(val.shape[1] // LANES)], axis=1)


def _mla_proj_kernel(x_ref, scale_ref, shift_ref, g_ref, cos_ref, sin_ref,
                     wqa_ref, wkva_ref, wkpe_ref, wgate_ref, qg_ref, kvg_ref,
                     wqb_ref, wkb_ref, wvbT_ref,
                     q_ref, k_ref, vT_ref, sg_ref):
    h = _modulated_norm(x_ref, g_ref, scale_ref, shift_ref)
    cos, sin = cos_ref[0], sin_ref[0]
    head_w = MLA_NOPE + LANES
    q_scale = (MLA_NOPE + MLA_ROPE) ** -0.5 * LOG2_E

    qn = _rms(_dot(h, wqa_ref[...]), qg_ref[...]).astype(BF16)
    q = _dot(qn, wqb_ref[...]) * q_scale
    for hd in range(MLA_HEADS):
        lo = hd * head_w
        q_ref[:, lo:lo + MLA_NOPE] = q[:, lo:lo + MLA_NOPE].astype(BF16)
        q_ref[:, lo + MLA_NOPE:lo + head_w] = _rope128(
            q[:, lo + MLA_NOPE:lo + head_w], cos, sin).astype(BF16)

    kvn = _rms(_dot(h, wkva_ref[...]), kvg_ref[...]).astype(BF16)
    kn = _dot(kvn, wkb_ref[...])
    kpe = _rope128(_dot(h, wkpe_ref[...]), cos, sin).astype(BF16)
    for hd in range(MLA_HEADS):
        lo = hd * head_w
        k_ref[:, lo:lo + MLA_NOPE] = kn[:, hd * MLA_NOPE:(hd + 1) * MLA_NOPE].astype(BF16)
        k_ref[:, lo + MLA_NOPE:lo + head_w] = kpe
    _store_slabs(vT_ref, _dot_nt(wvbT_ref[...], kvn), MLA_HEADS, MLA_V)

    sg_ref[...] = _silu(_dot(h, wgate_ref[...]))


def _mla_attn_kernel(q_ref, k_ref, vT_ref, sg_ref, o_ref, s_ref, m_ref, l_ref, acc_ref):
    qi = pl.program_id(2)
    tq = tk = MLA_TQ
    slabs = tk // LANES
    head_w = MLA_NOPE + LANES
    heads = range(MLA_HEADS_PER_STEP)

    def scores(slot, kb):
        rows = pl.ds(pl.multiple_of(kb * tk, tk), tk)
        for h in heads:
            cols = slice(h * head_w, (h + 1) * head_w)
            s_ref[slot, h] = _dot_nt(k_ref[0, rows, cols], q_ref[0, :, cols])

    def absorb(slot, kb, masked):
        if masked:
            kpos = lax.broadcasted_iota(jnp.int32, (tk, tq), 0)
            qpos = lax.broadcasted_iota(jnp.int32, (tk, tq), 1)
            visible = kpos <= qpos
        for h in heads:
            s = s_ref[slot, h]
            if masked:
                s = jnp.where(visible, s, NEG_INF)
            m = m_ref[h]
            m_new = jnp.maximum(m, jnp.max(s, axis=0, keepdims=True))
            alpha = jnp.exp2(m - m_new)
            p = jnp.exp2(s - m_new)
            m_ref[h] = m_new
            l_ref[h] = alpha * l_ref[h] + jnp.sum(p, axis=0, keepdims=True)
            vt = jnp.concatenate([vT_ref[0, h, kb * slabs + c] for c in range(slabs)], axis=1)
            acc_ref[h] = alpha * acc_ref[h] + _dot(vt, p.astype(BF16))

    m_ref[...] = jnp.full(m_ref.shape, NEG_INF, F32)
    l_ref[...] = jnp.zeros(l_ref.shape, F32)
    acc_ref[...] = jnp.zeros(acc_ref.shape, F32)
    scores(0, 0)

    def pair(j, carry):
        kb = 2 * j
        scores(1, kb + 1)
        absorb(0, kb, False)
        scores(0, kb + 2)
        absorb(1, kb + 1, False)
        return carry

    lax.fori_loop(0, lax.shift_right_logical(qi, 1), pair, 0)

    @pl.when((qi & 1) == 1)
    def _():
        scores(1, qi)
        absorb(0, qi - 1, False)
        absorb(1, qi, True)

    @pl.when((qi & 1) == 0)
    def _():
        absorb(0, qi, True)

    for h in heads:
        cols = slice(h * MLA_V, (h + 1) * MLA_V)
        o = (acc_ref[h] * (1.0 / l_ref[h])).T
        o_ref[0, :, cols] = (o * sg_ref[0, :, cols]).astype(BF16)


def _mla_layer(x2, pending, B, S, norm_g, scale, shift, cos, sin, w_in, q_norm_g, kv_norm_g, w_q_b, w_kv_b):
    T, D = x2.shape
    H = MLA_HEADS
    head_w = MLA_NOPE + LANES
    o1 = MLA_Q_LORA
    o2 = o1 + MLA_KV_LORA
    o3 = o2 + MLA_ROPE
    w_qa, w_kva, w_kpe, w_gate = (w_in[:, :o1], w_in[:, o1:o2], w_in[:, o2:o3], w_in[:, o3:])
    w_kpe = jnp.pad(w_kpe, ((0, 0), (0, LANES - MLA_ROPE)))
    wq = w_q_b.reshape(MLA_Q_LORA, H, MLA_NOPE + MLA_ROPE)
    wq = jnp.pad(wq, ((0, 0), (0, 0), (0, head_w - MLA_NOPE - MLA_ROPE))).reshape(MLA_Q_LORA, H * head_w)
    wkv = w_kv_b.reshape(MLA_KV_LORA, H, MLA_NOPE + MLA_V)
    w_kb = wkv[:, :, :MLA_NOPE].reshape(MLA_KV_LORA, H * MLA_NOPE)
    w_vbT = wkv[:, :, MLA_NOPE:].reshape(MLA_KV_LORA, H * MLA_V).T
    W = H * MLA_V

    tm = PROJ_ROWS
    tpb = S // tm
    full = lambda shape: pl.BlockSpec(shape, lambda i: (0,) * len(shape))
    rows = lambda w: pl.BlockSpec((tm, w), lambda i: (i, 0))
    per_b = pl.BlockSpec((1, 1, D), lambda i: (i // tpb, 0, 0))
    tab = pl.BlockSpec((1, tm, LANES), lambda i: (i // tpb, i % tpb, 0))
    (q, k, vT, sg), x2 = _run_proj(
        _mla_proj_kernel, "mla_proj", S, pending,
        in_specs=[rows(D), per_b, per_b, full((1, D)), tab, tab,
                  full((D, o1)), full((D, MLA_KV_LORA)), full((D, LANES)), full((D, W)),
                  full((1, o1)), full((1, MLA_KV_LORA)),
                  full((o1, H * head_w)), full((MLA_KV_LORA, H * MLA_NOPE)), full((W, MLA_KV_LORA))],
        args=[x2, scale[:, None, :], shift[:, None, :], norm_g[None, :], cos, sin,
              w_qa.astype(BF16), w_kva.astype(BF16), w_kpe.astype(BF16), w_gate.astype(BF16),
              q_norm_g[None, :], kv_norm_g[None, :],
              wq.astype(BF16), w_kb.astype(BF16), w_vbT.astype(BF16)],
        out_specs=[rows(H * head_w), rows(H * head_w),
                   pl.BlockSpec((1, H, tm // LANES, MLA_V, LANES), lambda i: (i // tpb, 0, i % tpb, 0, 0)),
                   rows(W)],
        out_shape=[jax.ShapeDtypeStruct((T, H * head_w), BF16),
                   jax.ShapeDtypeStruct((T, H * head_w), BF16),
                   jax.ShapeDtypeStruct((B, H, S // LANES, MLA_V, LANES), BF16),
                   jax.ShapeDtypeStruct((T, W), F32)])

    tq = MLA_TQ
    hp = MLA_HEADS_PER_STEP
    q3 = q.reshape(B, S, H * head_w)
    k3 = k.reshape(B, S, H * head_w)
    sg3 = sg.reshape(B, S, W)
    og = pl.pallas_call(
        _mla_attn_kernel,
        out_shape=jax.ShapeDtypeStruct((B, S, W), BF16),
        grid=(B, H // hp, S // tq),
        in_specs=[pl.BlockSpec((1, tq, hp * head_w), lambda b, h, i: (b, i, h)),
                  pl.BlockSpec((1, S, hp * head_w), lambda b, h, i: (b, 0, h)),
                  pl.BlockSpec((1, hp, S // LANES, MLA_V, LANES), lambda b, h, i: (b, h, 0, 0, 0)),
                  pl.BlockSpec((1, tq, hp * MLA_V), lambda b, h, i: (b, i, h))],
        out_specs=pl.BlockSpec((1, tq, hp * MLA_V), lambda b, h, i: (b, i, h)),
        scratch_shapes=[pltpu.VMEM((2, hp, tq, tq), F32),
                        pltpu.VMEM((hp, 1, tq), F32), pltpu.VMEM((hp, 1, tq), F32),
                        pltpu.VMEM((hp, MLA_V, tq), F32)],
        compiler_params=_params(3),
        name="mla_attn",
    )(q3, k3, vT, sg3)
    return og.reshape(T, W), x2


def _swa_proj_kernel(x_ref, scale_ref, shift_ref, g_ref, cos_ref, sin_ref,
                     wq_ref, wk_ref, wvT_ref, wgate_ref,
                     q_ref, k_ref, vT_ref, sg_ref):
    h = _modulated_norm(x_ref, g_ref, scale_ref, shift_ref)
    cos, sin = cos_ref[0], sin_ref[0]
    q = _rope_wide(_dot(h, wq_ref[...]) * (SWA_HEAD_DIM ** -0.5 * LOG2_E), cos, sin)
    _store_heads(q_ref, q, SWA_HEADS, SWA_HEAD_DIM, BF16)
    k = _rope_wide(_dot(h, wk_ref[...]), cos, sin)
    _store_heads(k_ref, k, SWA_KV_HEADS, SWA_HEAD_DIM, BF16)
    _store_slabs(vT_ref, _dot_nt(wvT_ref[...], h), SWA_KV_HEADS, SWA_HEAD_DIM)
    sg_ref[...] = _silu(_dot(h, wgate_ref[...]))


def _band_valid(diff, window):
    return lax.bitcast_convert_type(diff, jnp.uint32) < jnp.uint32(window)


def _swa_attn_kernel(q_ref, k_ref, vT_ref, sink_ref, sg_ref, o_ref):
    qi = pl.program_id(1)
    tq = SWA_TQ
    G = SWA_HEADS // SWA_KV_HEADS
    d = SWA_HEAD_DIM
    n = G * tq
    groups = range(SWA_KV_HEADS)
    assert tq == SWA_WINDOW == LANES
    prev = jnp.maximum(qi - 1, 0)
    q = [q_ref[0, i * G:(i + 1) * G].reshape(n, d) for i in groups]
    s_own = [_dot_nt(k_ref[0, i, pl.ds(pl.multiple_of(qi * tq, tq), tq), :], q[i]) for i in groups]
    s_prev = [_dot_nt(k_ref[0, i, pl.ds(pl.multiple_of(prev * tq, tq), tq), :], q[i]) for i in groups]
    row = lax.broadcasted_iota(jnp.int32, (tq, n), 0)
    col = lax.broadcasted_iota(jnp.int32, (tq, n), 1) & (tq - 1)
    own = row <= col
    first = qi == 0
    for i in groups:
        s = jnp.where(own, s_own[i], jnp.where(first, NEG_INF, s_prev[i]))
        sink = sink_ref[i]
        m = jnp.maximum(jnp.max(s, axis=0, keepdims=True), sink)
        e = jnp.exp2(s - m)
        denom = jnp.sum(e, axis=0, keepdims=True) + jnp.exp2(sink - m)
        e2 = jnp.concatenate([jnp.where(own, 0.0, e), jnp.where(own, e, 0.0)], axis=0).astype(BF16)
        vt = jnp.concatenate([vT_ref[0, i, prev], vT_ref[0, i, qi]], axis=1)
        oT = _dot(vt, e2) * (1.0 / denom)
        o = jnp.concatenate([oT[:, g * tq:(g + 1) * tq] for g in range(G)], axis=0).T
        cols = slice(i * G * d, (i + 1) * G * d)
        o_ref[0, :, cols] = (o * sg_ref[0, :, cols]).astype(BF16)


def _swa_layer(x2, pending, B, S, norm_g, scale, shift, cos, sin, w_in, sinks):
    T, D = x2.shape
    H, KV, d = SWA_HEADS, SWA_KV_HEADS, SWA_HEAD_DIM
    G = H // KV
    W = H * d
    KW = KV * d
    w_q, w_k, w_v, w_gate = (w_in[:, :W], w_in[:, W:W + KW], w_in[:, W + KW:W + 2 * KW],
                             w_in[:, W + 2 * KW:])
    tm = PROJ_ROWS
    tpb = S // tm
    full = lambda shape: pl.BlockSpec(shape, lambda i: (0,) * len(shape))
    rows = lambda w: pl.BlockSpec((tm, w), lambda i: (i, 0))
    per_b = pl.BlockSpec((1, 1, D), lambda i: (i // tpb, 0, 0))
    tab = pl.BlockSpec((1, tm, LANES), lambda i: (i // tpb, i % tpb, 0))
    heads = lambda nh: pl.BlockSpec((1, nh, tm, d), lambda i: (i // tpb, 0, i % tpb, 0))
    (q, k, vT, sg), x2 = _run_proj(
        _swa_proj_kernel, "swa_proj", S, pending,
        in_specs=[rows(D), per_b, per_b, full((1, D)), tab, tab,
                  full((D, W)), full((D, KW)), full((KW, D)), full((D, W))],
        args=[x2, scale[:, None, :], shift[:, None, :], norm_g[None, :], cos, sin,
              w_q.astype(BF16), w_k.astype(BF16), w_v.T.astype(BF16), w_gate.astype(BF16)],
        out_specs=[heads(H), heads(KV),
                   pl.BlockSpec((1, KV, tm // LANES, d, LANES), lambda i: (i // tpb, 0, i % tpb, 0, 0)),
                   rows(W)],
        out_shape=[jax.ShapeDtypeStruct((B, H, S, d), BF16),
                   jax.ShapeDtypeStruct((B, KV, S, d), BF16),
                   jax.ShapeDtypeStruct((B, KV, S // LANES, d, LANES), BF16),
                   jax.ShapeDtypeStruct((T, W), F32)])

    tq = SWA_TQ
    n = G * tq
    sink_rows = jnp.repeat(sinks.astype(F32).reshape(KV, G) * LOG2_E, tq, axis=1)[:, None, :]
    og = pl.pallas_call(
        _swa_attn_kernel,
        out_shape=jax.ShapeDtypeStruct((B, S, W), BF16),
        grid=(B, S // tq),
        in_specs=[pl.BlockSpec((1, H, tq, d), lambda b, i: (b, 0, i, 0)),
                  pl.BlockSpec((1, KV, S, d), lambda b, i: (b, 0, 0, 0)),
                  pl.BlockSpec((1, KV, S // LANES, d, LANES), lambda b, i: (b, 0, 0, 0, 0)),
                  pl.BlockSpec((KV, 1, n), lambda b, i: (0, 0, 0)),
                  pl.BlockSpec((1, tq, W), lambda b, i: (b, i, 0))],
        out_specs=pl.BlockSpec((1, tq, W), lambda b, i: (b, i, 0)),
        compiler_params=_params(2),
        name="swa_attn",
    )(q, k, vT, sink_rows, sg.reshape(B, S, W))
    return og.reshape(T, W), x2


def _nsa_proj_kernel(x_ref, scale_ref, shift_ref, g_ref, cos_ref, sin_ref,
                     wq_ref, wkc_ref, wvc_ref, wks_ref, wvsT_ref, wkw_ref, wvwT_ref, wgT_ref, wgate_ref,
                     q_ref, kc_ref, vc_ref, ks_ref, vsT_ref, kw_ref, vwT_ref, gT_ref, sg_ref, *, tiles_per_seq):
    h = _modulated_norm(x_ref, g_ref, scale_ref, shift_ref)
    cos, sin = cos_ref[0], sin_ref[0]
    KV, d = NSA_KV_HEADS, NSA_HEAD_DIM
    tm = h.shape[0]
    q = _rope_wide(_dot(h, wq_ref[...]) * (d ** -0.5 * LOG2_E), cos, sin)
    _store_heads_padded(q_ref, q, NSA_HEADS)
    _store_heads(kc_ref, _dot(h, wkc_ref[...]), KV, d, F32)
    _store_heads(vc_ref, _dot(h, wvc_ref[...]), KV, d, F32)
    pos = (pl.program_id(0) % tiles_per_seq) * tm + lax.broadcasted_iota(jnp.int32, (tm, LANES), 0)
    lane = lax.broadcasted_iota(jnp.int32, (tm, LANES), 1)
    sel_shift = NSA_SEL_BLOCK.bit_length() - 1
    block_onehot = jnp.where(lane - LANES // 2 == lax.shift_right_logical(pos, sel_shift), 1.0, 0.0)
    _store_heads_padded(ks_ref, _rope_wide(_dot(h, wks_ref[...]), cos, sin), KV, upper=block_onehot)
    _store_heads_padded(kw_ref, _rope_wide(_dot(h, wkw_ref[...]), cos, sin), KV)
    _store_slabs(vsT_ref, _dot_nt(wvsT_ref[...], h), KV, d)
    _store_slabs(vwT_ref, _dot_nt(wvwT_ref[...], h), KV, d)
    gT = _sigmoid(_dot_nt(wgT_ref[...], h))
    for kv in range(KV):
        gT_ref[0, kv] = gT[kv * 16:(kv + 1) * 16, :]
    sg_ref[...] = _silu(_dot(h, wgate_ref[...]))


def _nsa_compress_kernel(kc_ref, vc_ref, pe_ref, wk1_ref, wk2_ref, wv1_ref, wv2T_ref, kout_ref, vT_ref):
    n_chunks = kout_ref.shape[2]
    stride = NSA_CMP_STRIDE

    def hidden(x_ref, w1_ref):
        top = bot = None
        for l in range(stride):
            x_l = x_ref[0, 0, pl.ds(l, n_chunks, stride=stride), :]
            t = _dot((x_l + pe_ref[l]).astype(BF16), w1_ref[l])
            b = _dot((x_l + pe_ref[stride + l]).astype(BF16), w1_ref[stride + l])
            top = t if top is None else top + t
            bot = b if bot is None else bot + b
        return _silu(top + pltpu.roll(bot, n_chunks - 1, axis=0)).astype(BF16)

    hk = hidden(kc_ref, wk1_ref)
    kc = _dot(hk, wk2_ref[...])
    row = lax.broadcasted_iota(jnp.int32, kc.shape, 0)
    kout_ref[0, 0] = jnp.where(row < n_chunks - 1, kc, 0.0).astype(BF16)
    hv = hidden(vc_ref, wv1_ref)
    vT = _dot_nt(wv2T_ref[...], hv)
    col = lax.broadcasted_iota(jnp.int32, vT.shape, 1)
    vT_ref[0, 0] = jnp.where(col < n_chunks - 1, vT, 0.0).astype(BF16)


def _nsa_attn_kernel(q_ref, kc_ref, vcT_ref, ovT_ref, ks_ref, vsT_ref, kw_ref, vwT_ref, gT_ref, sg_ref,
                     o_ref, qsel_ref, s_ref, m_ref, l_ref, acc_ref, oc_ref, ow_ref, *, n_top, n_q_tiles):
    qi = pl.program_id(2)
    tq = NSA_TQ
    G = NSA_HEADS // NSA_KV_HEADS
    d = NSA_HEAD_DIM
    n = G * tq
    groups = range(NSA_KV_PER_STEP)
    q0 = qi * tq
    q = [q_ref[0, i * G:(i + 1) * G].reshape(n, LANES) for i in groups]

    def col_qpos(shape):
        return q0 + (lax.broadcasted_iota(jnp.int32, shape, 1) & (tq - 1))

    tk = NSA_SEL_TK
    slabs = tk // LANES

    def sel_scores(slot, kb):
        rows = pl.ds(pl.multiple_of(kb * tk, tk), tk)
        for i in groups:
            s_ref[slot, i] = _dot_nt(ks_ref[0, i, rows, :], qsel_ref[i])

    def sel_absorb(slot, kb, diagonal):
        if diagonal:
            kpos = kb * tk + lax.broadcasted_iota(jnp.int32, (tk, n), 0)
            keep = kpos <= col_qpos((tk, n))
        for i in groups:
            s = s_ref[slot, i]
            if diagonal:
                s = jnp.where(keep, s, NEG_INF)
            m = m_ref[i]
            m_new = jnp.maximum(m, jnp.max(s, axis=0, keepdims=True))
            alpha = jnp.exp2(m - m_new)
            p = jnp.exp2(s - m_new)
            m_ref[i] = m_new
            l_ref[i] = alpha * l_ref[i] + jnp.sum(p, axis=0, keepdims=True)
            vt = jnp.concatenate([vsT_ref[0, i, kb * slabs + c] for c in range(slabs)], axis=1)
            acc_ref[i] = alpha * acc_ref[i] + _dot(vt, p.astype(BF16))

    m_ref[...] = jnp.full(m_ref.shape, NEG_INF, F32)
    l_ref[...] = jnp.zeros(l_ref.shape, F32)
    acc_ref[...] = jnp.zeros(acc_ref.shape, F32)
    kb_diag = lax.shift_right_logical(q0, tk.bit_length() - 1)
    span = NSA_WINDOW + tq
    blk0 = jnp.maximum(qi * (tq // LANES) - NSA_WINDOW // LANES, 0)
    base = pl.multiple_of(blk0 * LANES, LANES)
    sel_shift = NSA_SEL_BLOCK.bit_length() - 1
    half = LANES // 2

    def front(ns_eff, nc_eff, full_band):
        s_cmp = [_dot_nt(kc_ref[0, i, :nc_eff, :], q[i]) for i in groups]
        s_win = [_dot_nt(kw_ref[0, i, pl.ds(base, span), :], q[i]) for i in groups]

        cmp_end = lax.broadcasted_iota(jnp.int32, (nc_eff, n), 0) * NSA_CMP_STRIDE + (NSA_CMP_BLOCK - 1)
        valid = cmp_end <= col_qpos((nc_eff, n))
        imp = []
        for i in groups:
            s = jnp.where(valid, s_cmp[i], NEG_INF)
            m = jnp.max(s, axis=0, keepdims=True)
            e = jnp.where(valid, jnp.exp2(s - m), 0.0)
            l = jnp.sum(e, axis=0, keepdims=True)
            p = (e * jnp.where(l > 0.0, 1.0 / l, 0.0)).astype(BF16)
            oc_ref[i] = _dot(vcT_ref[0, i, :, :nc_eff], p)
            imp_all = _dot(ovT_ref[:ns_eff, :nc_eff], p)
            acc = imp_all[:, 0:tq]
            for g in range(1, G):
                acc = acc + imp_all[:, g * tq:(g + 1) * tq]
            imp.append(acc)

        blk = lax.broadcasted_iota(jnp.int32, (ns_eff, tq), 0)
        q_blk = lax.shift_right_logical(q0 + lax.broadcasted_iota(jnp.int32, (ns_eff, tq), 1), sel_shift)
        causal = blk <= q_blk
        forced = (blk == 0) | (blk == q_blk) | (blk == q_blk - 1)
        lane = lax.broadcasted_iota(jnp.int32, (G, tq, LANES), 2)
        for i in groups:
            val = jnp.where(causal, imp[i] + jnp.where(forced, NSA_FORCE_BONUS, 0.0), -1.0)
            rank = jnp.zeros(val.shape, F32)
            for r in range(ns_eff):
                row = val[r:r + 1, :]
                rank = rank + jnp.where(blk > r, jnp.where(row >= val, 1.0, 0.0), jnp.where(row > val, 1.0, 0.0))
            bias = jnp.where(causal, jnp.where(rank < n_top, 0.0, NEG_INF), NEG_INF)
            bias_rows = [jnp.zeros((half, tq), F32), bias]
            if ns_eff < half:
                bias_rows.append(jnp.full((half - ns_eff, tq), NEG_INF, F32))
            bias_rows = jnp.concatenate(bias_rows, axis=0).T.astype(BF16)
            qsel_ref[i] = jnp.where(lane < half, q[i].reshape(G, tq, LANES), bias_rows[None]).reshape(n, LANES)

        sel_scores(0, 0)

        vt = [jnp.concatenate([vwT_ref[0, i, blk0 + c] for c in range(span // LANES)], axis=1) for i in groups]
        if full_band:
            row = lax.broadcasted_iota(jnp.int32, (tq, n), 0)
            last = row <= (lax.broadcasted_iota(jnp.int32, (tq, n), 1) & (tq - 1))
            for i in groups:
                edge = jnp.where(last, s_win[i][span - tq:], s_win[i][:tq])
                mid = s_win[i][tq:span - tq]
                m = jnp.maximum(jnp.max(edge, axis=0, keepdims=True), jnp.max(mid, axis=0, keepdims=True))
                e_edge = jnp.exp2(edge - m)
                e_mid = jnp.exp2(mid - m)
                denom = jnp.sum(e_edge, axis=0, keepdims=True) + jnp.sum(e_mid, axis=0, keepdims=True)
                e = jnp.concatenate([jnp.where(last, 0.0, e_edge), e_mid, jnp.where(last, e_edge, 0.0)], axis=0)
                ow_ref[i] = _dot(vt[i], e.astype(BF16)) * (1.0 / denom)
        else:
            diff = col_qpos((span, n)) - (base + lax.broadcasted_iota(jnp.int32, (span, n), 0))
            in_band = _band_valid(diff, NSA_WINDOW)
            for i in groups:
                s = jnp.where(in_band, s_win[i], NEG_INF)
                e = jnp.exp2(s - jnp.max(s, axis=0, keepdims=True))
                ow_ref[i] = _dot(vt[i], e.astype(BF16)) * (1.0 / jnp.sum(e, axis=0, keepdims=True))

    ns, nc_pad = ovT_ref.shape
    n_var = min(NSA_FRONT_VARIANTS, n_q_tiles)
    per = n_q_tiles // n_var
    for v in range(n_var):
        ns_eff = min(ns, -(-(ns * (v + 1)) // (n_var * 16)) * 16)
        nc_eff = min(nc_pad, -(-(nc_pad * (v + 1)) // (n_var * LANES)) * LANES)
        hi = n_q_tiles if v == n_var - 1 else (v + 1) * per

        full_band = v * per * tq >= NSA_WINDOW

        @pl.when((qi >= v * per) & (qi < hi))
        def _(ns_eff=ns_eff, nc_eff=nc_eff, full_band=full_band):
            front(ns_eff, nc_eff, full_band)

    def sel_pair(j, carry):
        kb = 2 * j
        sel_scores(1, kb + 1)
        sel_absorb(0, kb, False)
        sel_scores(0, kb + 2)
        sel_absorb(1, kb + 1, False)
        return carry

    lax.fori_loop(0, lax.shift_right_logical(kb_diag, 1), sel_pair, 0)

    @pl.when((kb_diag & 1) == 1)
    def _():
        sel_scores(1, kb_diag)
        sel_absorb(0, kb_diag - 1, False)
        sel_absorb(1, kb_diag, True)

    @pl.when((kb_diag & 1) == 0)
    def _():
        sel_absorb(0, kb_diag, True)

    for i in groups:
        o_s = acc_ref[i] * (1.0 / l_ref[i])
        gates = gT_ref[0, i]
        outs = []
        for g in range(G):
            cols = slice(g * tq, (g + 1) * tq)
            outs.append(gates[3 * g:3 * g + 1, :] * oc_ref[i][:, cols]
                        + gates[3 * g + 1:3 * g + 2, :] * o_s[:, cols]
                        + gates[3 * g + 2:3 * g + 3, :] * ow_ref[i][:, cols])
        o = jnp.concatenate(outs, axis=0).T
        cols = slice(i * G * d, (i + 1) * G * d)
        o_ref[0, :, cols] = (o * sg_ref[0, :, cols]).astype(BF16)


def _nsa_overlap_T(nc_pad, ns):
    nc = nc_pad - (NSA_CMP_BLOCK // NSA_CMP_STRIDE - 1)
    cs = np.arange(nc_pad)[None, :] * NSA_CMP_STRIDE
    ss = np.arange(ns)[:, None] * NSA_SEL_BLOCK
    ov = np.clip(np.minimum(cs + NSA_CMP_BLOCK, ss + NSA_SEL_BLOCK) - np.maximum(cs, ss), 0, None)
    ov = np.where(np.arange(nc_pad)[None, :] < nc, ov, 0)
    return jnp.asarray(ov / NSA_CMP_BLOCK, dtype=BF16)


def _nsa_layer(x2, pending, B, S, norm_g, scale, shift, cos, sin, w_in, cmp_pos, w_k1, w_k2, w_v1, w_v2):
    T, D = x2.shape
    H, KV, d = NSA_HEADS, NSA_KV_HEADS, NSA_HEAD_DIM
    G = H // KV
    W = H * d
    KW = KV * d
    offs = np.cumsum([0, W] + [KW] * 6 + [3 * H, W])
    w_q, w_kc, w_vc, w_ks, w_vs, w_kw, w_vw, w_g, w_gate = (
        w_in[:, offs[i]:offs[i + 1]] for i in range(9))
    w_gT = jnp.pad(w_g.T.reshape(KV, 3 * G, D), ((0, 0), (0, 16 - 3 * G), (0, 0))).reshape(KV * 16, D)

    tm = PROJ_ROWS
    tpb = S // tm
    full = lambda shape: pl.BlockSpec(shape, lambda i: (0,) * len(shape))
    rows = lambda w: pl.BlockSpec((tm, w), lambda i: (i, 0))
    per_b = pl.BlockSpec((1, 1, D), lambda i: (i // tpb, 0, 0))
    tab = pl.BlockSpec((1, tm, LANES), lambda i: (i // tpb, i % tpb, 0))
    heads = lambda nh: pl.BlockSpec((1, nh, tm, d), lambda i: (i // tpb, 0, i % tpb, 0))
    wide = lambda nh: pl.BlockSpec((1, nh, tm, LANES), lambda i: (i // tpb, 0, i % tpb, 0))
    slab = pl.BlockSpec((1, KV, tm // LANES, d, LANES), lambda i: (i // tpb, 0, i % tpb, 0, 0))
    kv_f32 = jax.ShapeDtypeStruct((B, KV, S, d), F32)
    kv_bf16 = jax.ShapeDtypeStruct((B, KV, S, LANES), BF16)
    kv_slab = jax.ShapeDtypeStruct((B, KV, S // LANES, d, LANES), BF16)
    assert S // NSA_SEL_BLOCK <= LANES // 2, "selection-block one-hot must fit in lanes 64..127"
    (q, kc, vc, ks, vsT, kw, vwT, gT, sg), x2 = _run_proj(
        functools.partial(_nsa_proj_kernel, tiles_per_seq=tpb), "nsa_proj", S, pending,
        in_specs=[rows(D), per_b, per_b, full((1, D)), tab, tab,
                  full((D, W)), full((D, KW)), full((D, KW)), full((D, KW)), full((KW, D)),
                  full((D, KW)), full((KW, D)), full((KV * 16, D)), full((D, W))],
        args=[x2, scale[:, None, :], shift[:, None, :], norm_g[None, :], cos, sin,
              w_q.astype(BF16), w_kc.astype(BF16), w_vc.astype(BF16), w_ks.astype(BF16),
              w_vs.T.astype(BF16), w_kw.astype(BF16), w_vw.T.astype(BF16), w_gT.astype(BF16),
              w_gate.astype(BF16)],
        out_specs=[wide(H), heads(KV), heads(KV), wide(KV), slab, wide(KV), slab,
                   pl.BlockSpec((1, KV, 16, tm), lambda i: (i // tpb, 0, 0, i % tpb)),
                   rows(W)],
        out_shape=[jax.ShapeDtypeStruct((B, H, S, LANES), BF16), kv_f32, kv_f32,
                   kv_bf16, kv_slab, kv_bf16, kv_slab,
                   jax.ShapeDtypeStruct((B, KV, 16, S), F32),
                   jax.ShapeDtypeStruct((T, W), F32)])

    n_chunks = S // NSA_CMP_STRIDE
    nl = NSA_CMP_BLOCK
    token_spec = pl.BlockSpec((1, 1, S, d), lambda b, kv: (b, kv, 0, 0))
    full2 = lambda shape: pl.BlockSpec(shape, lambda b, kv: (0,) * len(shape))
    k_c, v_cT = pl.pallas_call(
        _nsa_compress_kernel,
        out_shape=(jax.ShapeDtypeStruct((B, KV, n_chunks, LANES), BF16),
                   jax.ShapeDtypeStruct((B, KV, d, n_chunks), BF16)),
        grid=(B, KV),
        in_specs=[token_spec, token_spec, full2((nl, 1, d)),
                  full2((nl, d, NSA_CMP_HIDDEN)), full2((NSA_CMP_HIDDEN, LANES)),
                  full2((nl, d, NSA_CMP_HIDDEN)), full2((d, NSA_CMP_HIDDEN))],
        out_specs=(pl.BlockSpec((1, 1, n_chunks, LANES), lambda b, kv: (b, kv, 0, 0)),
                   pl.BlockSpec((1, 1, d, n_chunks), lambda b, kv: (b, kv, 0, 0))),
        compiler_params=_params(2),
        name="nsa_compress",
    )(kc, vc, cmp_pos[:, None, :],
      w_k1.reshape(nl, d, NSA_CMP_HIDDEN).astype(BF16),
      jnp.pad(w_k2, ((0, 0), (0, LANES - d))).astype(BF16),
      w_v1.reshape(nl, d, NSA_CMP_HIDDEN).astype(BF16), w_v2.T.astype(BF16))

    tq = NSA_TQ
    ns = S // NSA_SEL_BLOCK
    ovT = _nsa_overlap_T(n_chunks, ns)
    kp = NSA_KV_PER_STEP
    whole = lambda shape: pl.BlockSpec((1, kp) + shape, lambda b, kv, i: (b, kv) + (0,) * len(shape))
    og = pl.pallas_call(
        functools.partial(_nsa_attn_kernel, n_top=min(NSA_N_SELECT, ns), n_q_tiles=S // tq),
        out_shape=jax.ShapeDtypeStruct((B, S, W), BF16),
        grid=(B, KV // kp, S // tq),
        in_specs=[pl.BlockSpec((1, kp * G, tq, LANES), lambda b, kv, i: (b, kv, i, 0)),
                  whole((n_chunks, LANES)), whole((d, n_chunks)),
                  pl.BlockSpec((ns, n_chunks), lambda b, kv, i: (0, 0)),
                  whole((S, LANES)), whole((S // LANES, d, LANES)),
                  whole((S, LANES)), whole((S // LANES, d, LANES)),
                  pl.BlockSpec((1, kp, 16, tq), lambda b, kv, i: (b, kv, 0, i)),
                  pl.BlockSpec((1, tq, kp * G * d), lambda b, kv, i: (b, i, kv))],
        out_specs=pl.BlockSpec((1, tq, kp * G * d), lambda b, kv, i: (b, i, kv)),
        scratch_shapes=[pltpu.VMEM((kp, G * tq, LANES), BF16),
                        pltpu.VMEM((2, kp, NSA_SEL_TK, G * tq), F32),
                        pltpu.VMEM((kp, 1, G * tq), F32), pltpu.VMEM((kp, 1, G * tq), F32),
                        pltpu.VMEM((kp, d, G * tq), F32), pltpu.VMEM((kp, d, G * tq), F32),
                        pltpu.VMEM((kp, d, G * tq), F32)],
        compiler_params=_params(3),
        name="nsa_attn",
    )(q, k_c, v_cT, ovT, ks, vsT, kw, vwT, gT, sg.reshape(B, S, W))
    return og.reshape(T, W), x2


def kernel(x, c, positions, norm_g, ada_w, ada_b, mla_w_in, mla_q_norm_g, mla_kv_norm_g, mla_w_q_b, mla_w_kv_b, mla_w_out, swa_w_in, swa_sinks, swa_w_out, nsa_w_in, nsa_cmp_pos, nsa_w_cmp_k1, nsa_w_cmp_k2, nsa_w_cmp_v1, nsa_w_cmp_v2, nsa_w_out, final_norm_g):
    B, S, D = x.shape
    depth = norm_g.shape[0]
    cos, sin = _rope_tables(positions)
    mod = _ada_modulation(c, ada_w, ada_b)
    x2 = x.reshape(B * S, D)
    pending = None
    for i in range(depth):
        shift, scale, gate = mod[i, :, :D], mod[i, :, D:2 * D], mod[i, :, 2 * D:]
        kind, j = i % 3, i // 3
        if kind == 0:
            og, x2 = _mla_layer(x2, pending, B, S, norm_g[i], scale, shift, cos, sin, mla_w_in[j],
                                mla_q_norm_g[j], mla_kv_norm_g[j], mla_w_q_b[j], mla_w_kv_b[j])
            w_out = mla_w_out[j]
        elif kind == 1:
            og, x2 = _swa_layer(x2, pending, B, S, norm_g[i], scale, shift, cos, sin, swa_w_in[j], swa_sinks[j])
            w_out = swa_w_out[j]
        else:
            og, x2 = _nsa_layer(x2, pending, B, S, norm_g[i], scale, shift, cos, sin, nsa_w_in[j],
                                nsa_cmp_pos[j], nsa_w_cmp_k1[j], nsa_w_cmp_k2[j], nsa_w_cmp_v1[j],
                                nsa_w_cmp_v2[j])
            w_out = nsa_w_out[j]
        pending = (og, w_out, gate)
    return _final_out_proj(*pending[:2], x2, pending[2], final_norm_g, S).reshape(B, S, D)
```

```python
import functools

import jax
import jax.numpy as jnp
import numpy as np
from jax import lax
from jax.experimental import pallas as pl
from jax.experimental.pallas import tpu as pltpu

F32 = jnp.float32
BF16 = jnp.bfloat16

ROPE_THETA = 10000.0
RMS_EPS = 1e-6
NEG_INF = -1e30
LOG2_E = 1.4426950408889634

MLA_HEADS = 8
MLA_NOPE = 128
MLA_ROPE = 64
MLA_V = 128
MLA_Q_LORA = 256
MLA_KV_LORA = 128

SWA_HEADS = 16
SWA_KV_HEADS = 2
SWA_HEAD_DIM = 64
SWA_WINDOW = 128

NSA_HEADS = 16
NSA_KV_HEADS = 4
NSA_HEAD_DIM = 64
NSA_CMP_BLOCK = 32
NSA_CMP_STRIDE = 16
NSA_CMP_HIDDEN = 128
NSA_SEL_BLOCK = 64
NSA_N_SELECT = 16
NSA_WINDOW = 512
NSA_FORCE_BONUS = 1e4

LANES = 128
ROPE_HALF = 32
VMEM_LIMIT = 56 * 1024 * 1024

PROJ_ROWS = 512
OUT_ROWS = 512
MLA_TQ = 512
MLA_HEADS_PER_STEP = 4
SWA_TQ = 128
NSA_TQ = 128
NSA_SEL_TK = 512
NSA_KV_PER_STEP = 4
NSA_FRONT_VARIANTS = 4


def _dot(a, b):
    return jnp.dot(a, b, preferred_element_type=F32)


def _dot_nt(a, b):
    return lax.dot_general(a, b, (((1,), (1,)), ((), ())), preferred_element_type=F32)


def _silu(x):
    return x / (1.0 + jnp.exp(-x))


def _sigmoid(x):
    return 1.0 / (1.0 + jnp.exp(-x))


def _rms(x, g):
    return x * lax.rsqrt(jnp.mean(x * x, axis=-1, keepdims=True) + RMS_EPS) * g


def _params(n_axes):
    return pltpu.CompilerParams(dimension_semantics=("arbitrary",) * n_axes,
                                vmem_limit_bytes=VMEM_LIMIT)


def _rope128(x, cos, sin_signed):
    lane = lax.broadcasted_iota(jnp.int32, x.shape, 1)
    lower_half = (lane & (2 * ROPE_HALF - 1)) < ROPE_HALF
    partner = jnp.where(lower_half, pltpu.roll(x, LANES - ROPE_HALF, axis=1),
                        pltpu.roll(x, ROPE_HALF, axis=1))
    return x * cos + partner * sin_signed


def _rope_table_kernel(pos_ref, freq_ref, sign_ref, cos_ref, sin_ref):
    ang = pos_ref[0] * freq_ref[...]
    cos_ref[0] = jnp.cos(ang)
    sin_ref[0] = jnp.sin(ang) * sign_ref[...]


def _rope_tables(positions):
    B, S = positions.shape
    inv_freq = ROPE_THETA ** (-jnp.arange(ROPE_HALF, dtype=F32) / ROPE_HALF)
    freq = jnp.tile(inv_freq, LANES // ROPE_HALF)[None, :]
    sign = jnp.tile(jnp.concatenate([-jnp.ones(ROPE_HALF, F32), jnp.ones(ROPE_HALF, F32)]),
                    LANES // (2 * ROPE_HALF))[None, :]
    pos = positions.astype(F32)[..., None]
    ts = min(S, 512)
    spec = pl.BlockSpec((1, ts, LANES), lambda b, s: (b, s, 0))
    return pl.pallas_call(
        _rope_table_kernel,
        out_shape=(jax.ShapeDtypeStruct((B, S, LANES), F32),) * 2,
        grid=(B, S // ts),
        in_specs=[pl.BlockSpec((1, ts, 1), lambda b, s: (b, s, 0)),
                  pl.BlockSpec((1, LANES), lambda b, s: (0, 0)),
                  pl.BlockSpec((1, LANES), lambda b, s: (0, 0))],
        out_specs=(spec, spec),
        compiler_params=_params(2),
        name="rope_tables",
    )(pos, freq, sign)


def _ada_kernel(c_ref, w_ref, b_ref, o_ref):
    cond = _silu(c_ref[...]).astype(BF16)
    o_ref[0] = _dot(cond, w_ref[0].astype(BF16)) + b_ref[0]


def _ada_modulation(c, ada_w, ada_b):
    depth, D, D3 = ada_w.shape
    B = c.shape[0]
    Bp = -(-B // 16) * 16
    c_pad = jnp.pad(c, ((0, Bp - B), (0, 0)))
    tn = 1024
    out = pl.pallas_call(
        _ada_kernel,
        out_shape=jax.ShapeDtypeStruct((depth, Bp, D3), F32),
        grid=(depth, D3 // tn),
        in_specs=[pl.BlockSpec((Bp, D), lambda i, n: (0, 0)),
                  pl.BlockSpec((1, D, tn), lambda i, n: (i, 0, n)),
                  pl.BlockSpec((1, 1, tn), lambda i, n: (i, 0, n))],
        out_specs=pl.BlockSpec((1, Bp, tn), lambda i, n: (i, 0, n)),
        compiler_params=_params(2),
        name="ada_modulation",
    )(c_pad, ada_w, ada_b[:, None, :])
    return out[:, :B]


def _out_proj_kernel(og_ref, w_ref, x_ref, gate_ref, fg_ref, o_ref):
    y = _dot(og_ref[...], w_ref[...])
    o_ref[...] = _rms(x_ref[...] + gate_ref[0] * y, fg_ref[...])


def _final_out_proj(og, w_out, x2, gate_c, final_g, S):
    T, D = x2.shape
    W = og.shape[1]
    tm = OUT_ROWS
    tpb = S // tm
    return pl.pallas_call(
        _out_proj_kernel,
        out_shape=jax.ShapeDtypeStruct((T, D), F32),
        grid=(T // tm,),
        in_specs=[pl.BlockSpec((tm, W), lambda i: (i, 0)),
                  pl.BlockSpec((W, D), lambda i: (0, 0)),
                  pl.BlockSpec((tm, D), lambda i: (i, 0)),
                  pl.BlockSpec((1, 1, D), lambda i: (i // tpb, 0, 0)),
                  pl.BlockSpec((1, D), lambda i: (0, 0))],
        out_specs=pl.BlockSpec((tm, D), lambda i: (i, 0)),
        compiler_params=_params(1),
        name="out_proj",
    )(og, w_out.astype(BF16), x2, gate_c[:, None, :], final_g[None, :])


def _modulated_norm(x_ref, g_ref, scale_ref, shift_ref):
    y = _rms(x_ref[...], g_ref[...])
    return (y * (1.0 + scale_ref[0]) + shift_ref[0]).astype(BF16)


def _with_pending_residual(proj_kernel, n_in, n_out):
    def kernel(x_ref, *refs, **kwargs):
        ins, (og_ref, w_ref, gate_ref) = refs[:n_in - 1], refs[n_in - 1:n_in + 2]
        outs, xo_ref = refs[n_in + 2:n_in + 2 + n_out], refs[n_in + 2 + n_out]
        xo_ref[...] = x_ref[...] + gate_ref[0] * _dot(og_ref[...], w_ref[...])
        proj_kernel(xo_ref, *ins, *outs, **kwargs)
    return kernel


def _run_proj(proj_kernel, name, S, pending, in_specs, args, out_specs, out_shape):
    x2 = args[0]
    T, D = x2.shape
    tm = PROJ_ROWS
    tpb = S // tm
    if pending is not None:
        og, w_out, gate_c = pending
        W = og.shape[1]
        proj_kernel = _with_pending_residual(proj_kernel, len(in_specs), len(out_specs))
        in_specs = in_specs + [pl.BlockSpec((tm, W), lambda i: (i, 0)),
                               pl.BlockSpec((W, D), lambda i: (0, 0)),
                               pl.BlockSpec((1, 1, D), lambda i: (i // tpb, 0, 0))]
        args = args + [og, w_out.astype(BF16), gate_c[:, None, :]]
        out_specs = out_specs + [pl.BlockSpec((tm, D), lambda i: (i, 0))]
        out_shape = out_shape + [jax.ShapeDtypeStruct((T, D), F32)]
    res = pl.pallas_call(
        proj_kernel,
        out_shape=tuple(out_shape),
        grid=(T // tm,),
        in_specs=in_specs,
        out_specs=tuple(out_specs),
        compiler_params=_params(1),
        name=name,
    )(*args)
    if pending is not None:
        return res[:-1], res[-1]
    return res, x2


def _store_heads(dst_ref, val, n_heads, width, dtype):
    for h in range(n_heads):
        dst_ref[0, h] = val[:, h * width:(h + 1) * width].astype(dtype)


def _store_heads_padded(dst_ref, val, n_heads, upper=None):
    half = LANES // 2
    lane = lax.broadcasted_iota(jnp.int32, (val.shape[0], LANES), 1)
    fill = 0.0 if upper is None else upper
    for h in range(n_heads):
        chunk = val[:, (h // 2) * LANES:(h // 2 + 1) * LANES]
        if h % 2:
            chunk = pltpu.roll(chunk, half, axis=1)
        dst_ref[0, h] = jnp.where(lane < half, chunk, fill).astype(BF16)


def _store_slabs(dst_ref, val, n_heads, d):
    for h in range(n_heads):
        for c in range(val.shape[1] // LANES):
            dst_ref[0, h, c] = val[h * d:(h + 1) * d, c * LANES:(c + 1) * LANES].astype(BF16)


def _rope_wide(val, cos, sin_signed):
    return jnp.concatenate(
        [_rope128(val[:, c * LANES:(c + 1) * LANES], cos, sin_signed)
         for c in range(val.shape[1] // LANES)], axis=1)


def _mla_proj_kernel(x_ref, scale_ref, shift_ref, g_ref, cos_ref, sin_ref,
                     wqa_ref, wkva_ref, wkpe_ref, wgate_ref, qg_ref, kvg_ref,
                     wqb_ref, wkb_ref, wvbT_ref,
                     q_ref, k_ref, vT_ref, sg_ref):
    h = _modulated_norm(x_ref, g_ref, scale_ref, shift_ref)
    cos, sin = cos_ref[0], sin_ref[0]
    head_w = MLA_NOPE + LANES
    q_scale = (MLA_NOPE + MLA_ROPE) ** -0.5 * LOG2_E

    qn = _rms(_dot(h, wqa_ref[...]), qg_ref[...]).astype(BF16)
    q = _dot(qn, wqb_ref[...]) * q_scale
    for hd in range(MLA_HEADS):
        lo = hd * head_w
        q_ref[:, lo:lo + MLA_NOPE] = q[:, lo:lo + MLA_NOPE].astype(BF16)
        q_ref[:, lo + MLA_NOPE:lo + head_w] = _rope128(
            q[:, lo + MLA_NOPE:lo + head_w], cos, sin).astype(BF16)

    kvn = _rms(_dot(h, wkva_ref[...]), kvg_ref[...]).astype(BF16)
    kn = _dot(kvn, wkb_ref[...])
    kpe = _rope128(_dot(h, wkpe_ref[...]), cos, sin).astype(BF16)
    for hd in range(MLA_HEADS):
        lo = hd * head_w
        k_ref[:, lo:lo + MLA_NOPE] = kn[:, hd * MLA_NOPE:(hd + 1) * MLA_NOPE].astype(BF16)
        k_ref[:, lo + MLA_NOPE:lo + head_w] = kpe
    _store_slabs(vT_ref, _dot_nt(wvbT_ref[...], kvn), MLA_HEADS, MLA_V)

    sg_ref[...] = _silu(_dot(h, wgate_ref[...]))


def _mla_attn_kernel(q_ref, k_ref, vT_ref, sg_ref, o_ref, s_ref, m_ref, l_ref, acc_ref):
    qi = pl.program_id(2)
    tq = tk = MLA_TQ
    slabs = tk // LANES
    head_w = MLA_NOPE + LANES
    heads = range(MLA_HEADS_PER_STEP)

    def scores(slot, kb):
        rows = pl.ds(pl.multiple_of(kb * tk, tk), tk)
        for h in heads:
            cols = slice(h * head_w, (h + 1) * head_w)
            s_ref[slot, h] = _dot_nt(k_ref[0, rows, cols], q_ref[0, :, cols])

    def absorb(slot, kb, masked):
        if masked:
            kpos = lax.broadcasted_iota(jnp.int32, (tk, tq), 0)
            qpos = lax.broadcasted_iota(jnp.int32, (tk, tq), 1)
            visible = kpos <= qpos
        for h in heads:
            s = s_ref[slot, h]
            if masked:
                s = jnp.where(visible, s, NEG_INF)
            m = m_ref[h]
            m_new = jnp.maximum(m, jnp.max(s, axis=0, keepdims=True))
            alpha = jnp.exp2(m - m_new)
            p = jnp.exp2(s - m_new)
            m_ref[h] = m_new
            l_ref[h] = alpha * l_ref[h] + jnp.sum(p, axis=0, keepdims=True)
            vt = jnp.concatenate([vT_ref[0, h, kb * slabs + c] for c in range(slabs)], axis=1)
            acc_ref[h] = alpha * acc_ref[h] + _dot(vt, p.astype(BF16))

    m_ref[...] = jnp.full(m_ref.shape, NEG_INF, F32)
    l_ref[...] = jnp.zeros(l_ref.shape, F32)
    acc_ref[...] = jnp.zeros(acc_ref.shape, F32)
    scores(0, 0)

    def pair(j, carry):
        kb = 2 * j
        scores(1, kb + 1)
        absorb(0, kb, False)
        scores(0, kb + 2)
        absorb(1, kb + 1, False)
        return carry

    lax.fori_loop(0, lax.shift_right_logical(qi, 1), pair, 0)

    @pl.when((qi & 1) == 1)
    def _():
        scores(1, qi)
        absorb(0, qi - 1, False)
        absorb(1, qi, True)

    @pl.when((qi & 1) == 0)
    def _():
        absorb(0, qi, True)

    for h in heads:
        cols = slice(h * MLA_V, (h + 1) * MLA_V)
        o = (acc_ref[h] * (1.0 / l_ref[h])).T
        o_ref[0, :, cols] = (o * sg_ref[0, :, cols]).astype(BF16)


def _mla_layer(x2, pending, B, S, norm_g, scale, shift, cos, sin, w_in, q_norm_g, kv_norm_g, w_q_b, w_kv_b):
    T, D = x2.shape
    H = MLA_HEADS
    head_w = MLA_NOPE + LANES
    o1 = MLA_Q_LORA
    o2 = o1 + MLA_KV_LORA
    o3 = o2 + MLA_ROPE
    w_qa, w_kva, w_kpe, w_gate = (w_in[:, :o1], w_in[:, o1:o2], w_in[:, o2:o3], w_in[:, o3:])
    w_kpe = jnp.pad(w_kpe, ((0, 0), (0, LANES - MLA_ROPE)))
    wq = w_q_b.reshape(MLA_Q_LORA, H, MLA_NOPE + MLA_ROPE)
    wq = jnp.pad(wq, ((0, 0), (0, 0), (0, head_w - MLA_NOPE - MLA_ROPE))).reshape(MLA_Q_LORA, H * head_w)
    wkv = w_kv_b.reshape(MLA_KV_LORA, H, MLA_NOPE + MLA_V)
    w_kb = wkv[:, :, :MLA_NOPE].reshape(MLA_KV_LORA, H * MLA_NOPE)
    w_vbT = wkv[:, :, MLA_NOPE:].reshape(MLA_KV_LORA, H * MLA_V).T
    W = H * MLA_V

    tm = PROJ_ROWS
    tpb = S // tm
    full = lambda shape: pl.BlockSpec(shape, lambda i: (0,) * len(shape))
    rows = lambda w: pl.BlockSpec((tm, w), lambda i: (i, 0))
    per_b = pl.BlockSpec((1, 1, D), lambda i: (i // tpb, 0, 0))
    tab = pl.BlockSpec((1, tm, LANES), lambda i: (i // tpb, i % tpb, 0))
    (q, k, vT, sg), x2 = _run_proj(
        _mla_proj_kernel, "mla_proj", S, pending,
        in_specs=[rows(D), per_b, per_b, full((1, D)), tab, tab,
                  full((D, o1)), full((D, MLA_KV_LORA)), full((D, LANES)), full((D, W)),
                  full((1, o1)), full((1, MLA_KV_LORA)),
                  full((o1, H * head_w)), full((MLA_KV_LORA, H * MLA_NOPE)), full((W, MLA_KV_LORA))],
        args=[x2, scale[:, None, :], shift[:, None, :], norm_g[None, :], cos, sin,
              w_qa.astype(BF16), w_kva.astype(BF16), w_kpe.astype(BF16), w_gate.astype(BF16),
              q_norm_g[None, :], kv_norm_g[None, :],
              wq.astype(BF16), w_kb.astype(BF16), w_vbT.astype(BF16)],
        out_specs=[rows(H * head_w), rows(H * head_w),
                   pl.BlockSpec((1, H, tm // LANES, MLA_V, LANES), lambda i: (i // tpb, 0, i % tpb, 0, 0)),
                   rows(W)],
        out_shape=[jax.ShapeDtypeStruct((T, H * head_w), BF16),
                   jax.ShapeDtypeStruct((T, H * head_w), BF16),
                   jax.ShapeDtypeStruct((B, H, S // LANES, MLA_V, LANES), BF16),
                   jax.ShapeDtypeStruct((T, W), F32)])

    tq = MLA_TQ
    hp = MLA_HEADS_PER_STEP
    q3 = q.reshape(B, S, H * head_w)
    k3 = k.reshape(B, S, H * head_w)
    sg3 = sg.reshape(B, S, W)
    og = pl.pallas_call(
        _mla_attn_kernel,
        out_shape=jax.ShapeDtypeStruct((B, S, W), BF16),
        grid=(B, H // hp, S // tq),
        in_specs=[pl.BlockSpec((1, tq, hp * head_w), lambda b, h, i: (b, i, h)),
                  pl.BlockSpec((1, S, hp * head_w), lambda b, h, i: (b, 0, h)),
                  pl.BlockSpec((1, hp, S // LANES, MLA_V, LANES), lambda b, h, i: (b, h, 0, 0, 0)),
                  pl.BlockSpec((1, tq, hp * MLA_V), lambda b, h, i: (b, i, h))],
        out_specs=pl.BlockSpec((1, tq, hp * MLA_V), lambda b, h, i: (b, i, h)),
        scratch_shapes=[pltpu.VMEM((2, hp, tq, tq), F32),
                        pltpu.VMEM((hp, 1, tq), F32), pltpu.VMEM((hp, 1, tq), F32),
                        pltpu.VMEM((hp, MLA_V, tq), F32)],
        compiler_params=_params(3),
        name="mla_attn",
    )(q3, k3, vT, sg3)
    return og.reshape(T, W), x2


def _swa_proj_kernel(x_ref, scale_ref, shift_ref, g_ref, cos_ref, sin_ref,
                     wq_ref, wk_ref, wvT_ref, wgate_ref,
                     q_ref, k_ref, vT_ref, sg_ref):
    h = _modulated_norm(x_ref, g_ref, scale_ref, shift_ref)
    cos, sin = cos_ref[0], sin_ref[0]
    q = _rope_wide(_dot(h, wq_ref[...]) * (SWA_HEAD_DIM ** -0.5 * LOG2_E), cos, sin)
    _store_heads(q_ref, q, SWA_HEADS, SWA_HEAD_DIM, BF16)
    k = _rope_wide(_dot(h, wk_ref[...]), cos, sin)
    _store_heads(k_ref, k, SWA_KV_HEADS, SWA_HEAD_DIM, BF16)
    _store_slabs(vT_ref, _dot_nt(wvT_ref[...], h), SWA_KV_HEADS, SWA_HEAD_DIM)
    sg_ref[...] = _silu(_dot(h, wgate_ref[...]))


def _band_valid(diff, window):
    return lax.bitcast_convert_type(diff, jnp.uint32) < jnp.uint32(window)


def _swa_attn_kernel(q_ref, k_ref, vT_ref, sink_ref, sg_ref, o_ref):
    qi = pl.program_id(1)
    tq = SWA_TQ
    G = SWA_HEADS // SWA_KV_HEADS
    d = SWA_HEAD_DIM
    n = G * tq
    groups = range(SWA_KV_HEADS)
    assert tq == SWA_WINDOW == LANES
    prev = jnp.maximum(qi - 1, 0)
    q = [q_ref[0, i * G:(i + 1) * G].reshape(n, d) for i in groups]
    s_own = [_dot_nt(k_ref[0, i, pl.ds(pl.multiple_of(qi * tq, tq), tq), :], q[i]) for i in groups]
    s_prev = [_dot_nt(k_ref[0, i, pl.ds(pl.multiple_of(prev * tq, tq), tq), :], q[i]) for i in groups]
    row = lax.broadcasted_iota(jnp.int32, (tq, n), 0)
    col = lax.broadcasted_iota(jnp.int32, (tq, n), 1) & (tq - 1)
    own = row <= col
    first = qi == 0
    for i in groups:
        s = jnp.where(own, s_own[i], jnp.where(first, NEG_INF, s_prev[i]))
        sink = sink_ref[i]
        m = jnp.maximum(jnp.max(s, axis=0, keepdims=True), sink)
        e = jnp.exp2(s - m)
        denom = jnp.sum(e, axis=0, keepdims=True) + jnp.exp2(sink - m)
        e2 = jnp.concatenate([jnp.where(own, 0.0, e), jnp.where(own, e, 0.0)], axis=0).astype(BF16)
        vt = jnp.concatenate([vT_ref[0, i, prev], vT_ref[0, i, qi]], axis=1)
        oT = _dot(vt, e2) * (1.0 / denom)
        o = jnp.concatenate([oT[:, g * tq:(g + 1) * tq] for g in range(G)], axis=0).T
        cols = slice(i * G * d, (i + 1) * G * d)
        o_ref[0, :, cols] = (o * sg_ref[0, :, cols]).astype(BF16)


def _swa_layer(x2, pending, B, S, norm_g, scale, shift, cos, sin, w_in, sinks):
    T, D = x2.shape
    H, KV, d = SWA_HEADS, SWA_KV_HEADS, SWA_HEAD_DIM
    G = H // KV
    W = H * d
    KW = KV * d
    w_q, w_k, w_v, w_gate = (w_in[:, :W], w_in[:, W:W + KW], w_in[:, W + KW:W + 2 * KW],
                             w_in[:, W + 2 * KW:])
    tm = PROJ_ROWS
    tpb = S // tm
    full = lambda shape: pl.BlockSpec(shape, lambda i: (0,) * len(shape))
    rows = lambda w: pl.BlockSpec((tm, w), lambda i: (i, 0))
    per_b = pl.BlockSpec((1, 1, D), lambda i: (i // tpb, 0, 0))
    tab = pl.BlockSpec((1, tm, LANES), lambda i: (i // tpb, i % tpb, 0))
    heads = lambda nh: pl.BlockSpec((1, nh, tm, d), lambda i: (i // tpb, 0, i % tpb, 0))
    (q, k, vT, sg), x2 = _run_proj(
        _swa_proj_kernel, "swa_proj", S, pending,
        in_specs=[rows(D), per_b, per_b, full((1, D)), tab, tab,
                  full((D, W)), full((D, KW)), full((KW, D)), full((D, W))],
        args=[x2, scale[:, None, :], shift[:, None, :], norm_g[None, :], cos, sin,
              w_q.astype(BF16), w_k.astype(BF16), w_v.T.astype(BF16), w_gate.astype(BF16)],
        out_specs=[heads(H), heads(KV),
                   pl.BlockSpec((1, KV, tm // LANES, d, LANES), lambda i: (i // tpb, 0, i % tpb, 0, 0)),
                   rows(W)],
        out_shape=[jax.ShapeDtypeStruct((B, H, S, d), BF16),
                   jax.ShapeDtypeStruct((B, KV, S, d), BF16),
                   jax.ShapeDtypeStruct((B, KV, S // LANES, d, LANES), BF16),
                   jax.ShapeDtypeStruct((T, W), F32)])

    tq = SWA_TQ
    n = G * tq
    sink_rows = jnp.repeat(sinks.astype(F32).reshape(KV, G) * LOG2_E, tq, axis=1)[:, None, :]
    og = pl.pallas_call(
        _swa_attn_kernel,
        out_shape=jax.ShapeDtypeStruct((B, S, W), BF16),
        grid=(B, S // tq),
        in_specs=[pl.BlockSpec((1, H, tq, d), lambda b, i: (b, 0, i, 0)),
                  pl.BlockSpec((1, KV, S, d), lambda b, i: (b, 0, 0, 0)),
                  pl.BlockSpec((1, KV, S // LANES, d, LANES), lambda b, i: (b, 0, 0, 0, 0)),
                  pl.BlockSpec((KV, 1, n), lambda b, i: (0, 0, 0)),
                  pl.BlockSpec((1, tq, W), lambda b, i: (b, i, 0))],
        out_specs=pl.BlockSpec((1, tq, W), lambda b, i: (b, i, 0)),
        compiler_params=_params(2),
        name="swa_attn",
    )(q, k, vT, sink_rows, sg.reshape(B, S, W))
    return og.reshape(T, W), x2


def _nsa_proj_kernel(x_ref, scale_ref, shift_ref, g_ref, cos_ref, sin_ref,
                     wq_ref, wkc_ref, wvc_ref, wks_ref, wvsT_ref, wkw_ref, wvwT_ref, wgT_ref, wgate_ref,
                     q_ref, kc_ref, vc_ref, ks_ref, vsT_ref, kw_ref, vwT_ref, gT_ref, sg_ref, *, tiles_per_seq):
    h = _modulated_norm(x_ref, g_ref, scale_ref, shift_ref)
    cos, sin = cos_ref[0], sin_ref[0]
    KV, d = NSA_KV_HEADS, NSA_HEAD_DIM
    tm = h.shape[0]
    q = _rope_wide(_dot(h, wq_ref[...]) * (d ** -0.5 * LOG2_E), cos, sin)
    _store_heads_padded(q_ref, q, NSA_HEADS)
    _store_heads(kc_ref, _dot(h, wkc_ref[...]), KV, d, F32)
    _store_heads(vc_ref, _dot(h, wvc_ref[...]), KV, d, F32)
    pos = (pl.program_id(0) % tiles_per_seq) * tm + lax.broadcasted_iota(jnp.int32, (tm, LANES), 0)
    lane = lax.broadcasted_iota(jnp.int32, (tm, LANES), 1)
    sel_shift = NSA_SEL_BLOCK.bit_length() - 1
    block_onehot = jnp.where(lane - LANES // 2 == lax.shift_right_logical(pos, sel_shift), 1.0, 0.0)
    _store_heads_padded(ks_ref, _rope_wide(_dot(h, wks_ref[...]), cos, sin), KV, upper=block_onehot)
    _store_heads_padded(kw_ref, _rope_wide(_dot(h, wkw_ref[...]), cos, sin), KV)
    _store_slabs(vsT_ref, _dot_nt(wvsT_ref[...], h), KV, d)
    _store_slabs(vwT_ref, _dot_nt(wvwT_ref[...], h), KV, d)
    gT = _sigmoid(_dot_nt(wgT_ref[...], h))
    for kv in range(KV):
        gT_ref[0, kv] = gT[kv * 16:(kv + 1) * 16, :]
    sg_ref[...] = _silu(_dot(h, wgate_ref[...]))


def _nsa_compress_kernel(kc_ref, vc_ref, pe_ref, wk1_ref, wk2_ref, wv1_ref, wv2T_ref, kout_ref, vT_ref):
    n_chunks = kout_ref.shape[2]
    stride = NSA_CMP_STRIDE

    def hidden(x_ref, w1_ref):
        top = bot = None
        for l in range(stride):
            x_l = x_ref[0, 0, pl.ds(l, n_chunks, stride=stride), :]
            t = _dot((x_l + pe_ref[l]).astype(BF16), w1_ref[l])
            b = _dot((x_l + pe_ref[stride + l]).astype(BF16), w1_ref[stride + l])
            top = t if top is None else top + t
            bot = b if bot is None else bot + b
        return _silu(top + pltpu.roll(bot, n_chunks - 1, axis=0)).astype(BF16)

    hk = hidden(kc_ref, wk1_ref)
    kc = _dot(hk, wk2_ref[...])
    row = lax.broadcasted_iota(jnp.int32, kc.shape, 0)
    kout_ref[0, 0] = jnp.where(row < n_chunks - 1, kc, 0.0).astype(BF16)
    hv = hidden(vc_ref, wv1_ref)
    vT = _dot_nt(wv2T_ref[...], hv)
    col = lax.broadcasted_iota(jnp.int32, vT.shape, 1)
    vT_ref[0, 0] = jnp.where(col < n_chunks - 1, vT, 0.0).astype(BF16)


def _nsa_attn_kernel(q_ref, kc_ref, vcT_ref, ovT_ref, ks_ref, vsT_ref, kw_ref, vwT_ref, gT_ref, sg_ref,
                     o_ref, qsel_ref, s_ref, m_ref, l_ref, acc_ref, oc_ref, ow_ref, *, n_top, n_q_tiles):
    qi = pl.program_id(2)
    tq = NSA_TQ
    G = NSA_HEADS // NSA_KV_HEADS
    d = NSA_HEAD_DIM
    n = G * tq
    groups = range(NSA_KV_PER_STEP)
    q0 = qi * tq
    q = [q_ref[0, i * G:(i + 1) * G].reshape(n, LANES) for i in groups]

    def col_qpos(shape):
        return q0 + (lax.broadcasted_iota(jnp.int32, shape, 1) & (tq - 1))

    tk = NSA_SEL_TK
    slabs = tk // LANES

    def sel_scores(slot, kb):
        rows = pl.ds(pl.multiple_of(kb * tk, tk), tk)
        for i in groups:
            s_ref[slot, i] = _dot_nt(ks_ref[0, i, rows, :], qsel_ref[i])

    def sel_absorb(slot, kb, diagonal):
        if diagonal:
            kpos = kb * tk + lax.broadcasted_iota(jnp.int32, (tk, n), 0)
            keep = kpos <= col_qpos((tk, n))
        for i in groups:
            s = s_ref[slot, i]
            if diagonal:
                s = jnp.where(keep, s, NEG_INF)
            m = m_ref[i]
            m_new = jnp.maximum(m, jnp.max(s, axis=0, keepdims=True))
            alpha = jnp.exp2(m - m_new)
            p = jnp.exp2(s - m_new)
            m_ref[i] = m_new
            l_ref[i] = alpha * l_ref[i] + jnp.sum(p, axis=0, keepdims=True)
            vt = jnp.concatenate([vsT_ref[0, i, kb * slabs + c] for c in range(slabs)], axis=1)
            acc_ref[i] = alpha * acc_ref[i] + _dot(vt, p.astype(BF16))

    m_ref[...] = jnp.full(m_ref.shape, NEG_INF, F32)
    l_ref[...] = jnp.zeros(l_ref.shape, F32)
    acc_ref[...] = jnp.zeros(acc_ref.shape, F32)
    kb_diag = lax.shift_right_logical(q0, tk.bit_length() - 1)
    span = NSA_WINDOW + tq
    blk0 = jnp.maximum(qi * (tq // LANES) - NSA_WINDOW // LANES, 0)
    base = pl.multiple_of(blk0 * LANES, LANES)
    sel_shift = NSA_SEL_BLOCK.bit_length() - 1
    half = LANES // 2

    def front(ns_eff, nc_eff, full_band):
        s_cmp = [_dot_nt(kc_ref[0, i, :nc_eff, :], q[i]) for i in groups]
        s_win = [_dot_nt(kw_ref[0, i, pl.ds(base, span), :], q[i]) for i in groups]

        cmp_end = lax.broadcasted_iota(jnp.int32, (nc_eff, n), 0) * NSA_CMP_STRIDE + (NSA_CMP_BLOCK - 1)
        valid = cmp_end <= col_qpos((nc_eff, n))
        imp = []
        for i in groups:
            s = jnp.where(valid, s_cmp[i], NEG_INF)
            m = jnp.max(s, axis=0, keepdims=True)
            e = jnp.where(valid, jnp.exp2(s - m), 0.0)
            l = jnp.sum(e, axis=0, keepdims=True)
            p = (e * jnp.where(l > 0.0, 1.0 / l, 0.0)).astype(BF16)
            oc_ref[i] = _dot(vcT_ref[0, i, :, :nc_eff], p)
            imp_all = _dot(ovT_ref[:ns_eff, :nc_eff], p)
            acc = imp_all[:, 0:tq]
            for g in range(1, G):
                acc = acc + imp_all[:, g * tq:(g + 1) * tq]
            imp.append(acc)

        blk = lax.broadcasted_iota(jnp.int32, (ns_eff, tq), 0)
        q_blk = lax.shift_right_logical(q0 + lax.broadcasted_iota(jnp.int32, (ns_eff, tq), 1), sel_shift)
        causal = blk <= q_blk
        forced = (blk == 0) | (blk == q_blk) | (blk == q_blk - 1)
        lane = lax.broadcasted_iota(jnp.int32, (G, tq, LANES), 2)
        for i in groups:
            val = jnp.where(causal, imp[i] + jnp.where(forced, NSA_FORCE_BONUS, 0.0), -1.0)
            rank = jnp.zeros(val.shape, F32)
            for r in range(ns_eff):
                row = val[r:r + 1, :]
                rank = rank + jnp.where(blk > r, jnp.where(row >= val, 1.0, 0.0), jnp.where(row > val, 1.0, 0.0))
            bias = jnp.where(causal, jnp.where(rank < n_top, 0.0, NEG_INF), NEG_INF)
            bias_rows = [jnp.zeros((half, tq), F32), bias]
            if ns_eff < half:
                bias_rows.append(jnp.full((half - ns_eff, tq), NEG_INF, F32))
            bias_rows = jnp.concatenate(bias_rows, axis=0).T.astype(BF16)
            qsel_ref[i] = jnp.where(lane < half, q[i].reshape(G, tq, LANES), bias_rows[None]).reshape(n, LANES)

        sel_scores(0, 0)

        vt = [jnp.concatenate([vwT_ref[0, i, blk0 + c] for c in range(span // LANES)], axis=1) for i in groups]
        if full_band:
            row = lax.broadcasted_iota(jnp.int32, (tq, n), 0)
            last = row <= (lax.broadcasted_iota(jnp.int32, (tq, n), 1) & (tq - 1))
            for i in groups:
                edge = jnp.where(last, s_win[i][span - tq:], s_win[i][:tq])
                mid = s_win[i][tq:span - tq]
                m = jnp.maximum(jnp.max(edge, axis=0, keepdims=True), jnp.max(mid, axis=0, keepdims=True))
                e_edge = jnp.exp2(edge - m)
                e_mid = jnp.exp2(mid - m)
                denom = jnp.sum(e_edge, axis=0, keepdims=True) + jnp.sum(e_mid, axis=0, keepdims=True)
                e = jnp.concatenate([jnp.where(last, 0.0, e_edge), e_mid, jnp.where(last, e_edge, 0.0)], axis=0)
                ow_ref[i] = _dot(vt[i], e.astype(BF16)) * (1.0 / denom)
        else:
            diff = col_qpos((span, n)) - (base + lax.broadcasted_iota(jnp.int32, (span, n), 0))
            in_band = _band_valid(diff, NSA_WINDOW)
            for i in groups:
                s = jnp.where(in_band, s_win[i], NEG_INF)
                e = jnp.exp2(s - jnp.max(s, axis=0, keepdims=True))
                ow_ref[i] = _dot(vt[i], e.astype(BF16)) * (1.0 / jnp.sum(e, axis=0, keepdims=True))

    ns, nc_pad = ovT_ref.shape
    n_var = min(NSA_FRONT_VARIANTS, n_q_tiles)
    per = n_q_tiles // n_var
    for v in range(n_var):
        ns_eff = min(ns, -(-(ns * (v + 1)) // (n_var * 16)) * 16)
        nc_eff = min(nc_pad, -(-(nc_pad * (v + 1)) // (n_var * LANES)) * LANES)
        hi = n_q_tiles if v == n_var - 1 else (v + 1) * per

        full_band = v * per * tq >= NSA_WINDOW

        @pl.when((qi >= v * per) & (qi < hi))
        def _(ns_eff=ns_eff, nc_eff=nc_eff, full_band=full_band):
            front(ns_eff, nc_eff, full_band)

    def sel_pair(j, carry):
        kb = 2 * j
        sel_scores(1, kb + 1)
        sel_absorb(0, kb, False)
        sel_scores(0, kb + 2)
        sel_absorb(1, kb + 1, False)
        return carry

    lax.fori_loop(0, lax.shift_right_logical(kb_diag, 1), sel_pair, 0)

    @pl.when((kb_diag & 1) == 1)
    def _():
        sel_scores(1, kb_diag)
        sel_absorb(0, kb_diag - 1, False)
        sel_absorb(1, kb_diag, True)

    @pl.when((kb_diag & 1) == 0)
    def _():
        sel_absorb(0, kb_diag, True)

    for i in groups:
        o_s = acc_ref[i] * (1.0 / l_ref[i])
        gates = gT_ref[0, i]
        outs = []
        for g in range(G):
            cols = slice(g * tq, (g + 1) * tq)
            outs.append(gates[3 * g:3 * g + 1, :] * oc_ref[i][:, cols]
                        + gates[3 * g + 1:3 * g + 2, :] * o_s[:, cols]
                        + gates[3 * g + 2:3 * g + 3, :] * ow_ref[i][:, cols])
        o = jnp.concatenate(outs, axis=0).T
        cols = slice(i * G * d, (i + 1) * G * d)
        o_ref[0, :, cols] = (o * sg_ref[0, :, cols]).astype(BF16)


def _nsa_overlap_T(nc_pad, ns):
    nc = nc_pad - (NSA_CMP_BLOCK // NSA_CMP_STRIDE - 1)
    cs = np.arange(nc_pad)[None, :] * NSA_CMP_STRIDE
    ss = np.arange(ns)[:, None] * NSA_SEL_BLOCK
    ov = np.clip(np.minimum(cs + NSA_CMP_BLOCK, ss + NSA_SEL_BLOCK) - np.maximum(cs, ss), 0, None)
    ov = np.where(np.arange(nc_pad)[None, :] < nc, ov, 0)
    return jnp.asarray(ov / NSA_CMP_BLOCK, dtype=BF16)


def _nsa_layer(x2, pending, B, S, norm_g, scale, shift, cos, sin, w_in, cmp_pos, w_k1, w_k2, w_v1, w_v2):
    T, D = x2.shape
    H, KV, d = NSA_HEADS, NSA_KV_HEADS, NSA_HEAD_DIM
    G = H // KV
    W = H * d
    KW = KV * d
    offs = np.cumsum([0, W] + [KW] * 6 + [3 * H, W])
    w_q, w_kc, w_vc, w_ks, w_vs, w_kw, w_vw, w_g, w_gate = (
        w_in[:, offs[i]:offs[i + 1]] for i in range(9))
    w_gT = jnp.pad(w_g.T.reshape(KV, 3 * G, D), ((0, 0), (0, 16 - 3 * G), (0, 0))).reshape(KV * 16, D)

    tm = PROJ_ROWS
    tpb = S // tm
    full = lambda shape: pl.BlockSpec(shape, lambda i: (0,) * len(shape))
    rows = lambda w: pl.BlockSpec((tm, w), lambda i: (i, 0))
    per_b = pl.BlockSpec((1, 1, D), lambda i: (i // tpb, 0, 0))
    tab = pl.BlockSpec((1, tm, LANES), lambda i: (i // tpb, i % tpb, 0))
    heads = lambda nh: pl.BlockSpec((1, nh, tm, d), lambda i: (i // tpb, 0, i % tpb, 0))
    wide = lambda nh: pl.BlockSpec((1, nh, tm, LANES), lambda i: (i // tpb, 0, i % tpb, 0))
    slab = pl.BlockSpec((1, KV, tm // LANES, d, LANES), lambda i: (i // tpb, 0, i % tpb, 0, 0))
    kv_f32 = jax.ShapeDtypeStruct((B, KV, S, d), F32)
    kv_bf16 = jax.ShapeDtypeStruct((B, KV, S, LANES), BF16)
    kv_slab = jax.ShapeDtypeStruct((B, KV, S // LANES, d, LANES), BF16)
    assert S // NSA_SEL_BLOCK <= LANES // 2, "selection-block one-hot must fit in lanes 64..127"
    (q, kc, vc, ks, vsT, kw, vwT, gT, sg), x2 = _run_proj(
        functools.partial(_nsa_proj_kernel, tiles_per_seq=tpb), "nsa_proj", S, pending,
        in_specs=[rows(D), per_b, per_b, full((1, D)), tab, tab,
                  full((D, W)), full((D, KW)), full((D, KW)), full((D, KW)), full((KW, D)),
                  full((D, KW)), full((KW, D)), full((KV * 16, D)), full((D, W))],
        args=[x2, scale[:, None, :], shift[:, None, :], norm_g[None, :], cos, sin,
              w_q.astype(BF16), w_kc.astype(BF16), w_vc.astype(BF16), w_ks.astype(BF16),
              w_vs.T.astype(BF16), w_kw.astype(BF16), w_vw.T.astype(BF16), w_gT.astype(BF16),
              w_gate.astype(BF16)],
        out_specs=[wide(H), heads(KV), heads(KV), wide(KV), slab, wide(KV), slab,
                   pl.BlockSpec((1, KV, 16, tm), lambda i: (i // tpb, 0, 0, i % tpb)),
                   rows(W)],
        out_shape=[jax.ShapeDtypeStruct((B, H, S, LANES), BF16), kv_f32, kv_f32,
                   kv_bf16, kv_slab, kv_bf16, kv_slab,
                   jax.ShapeDtypeStruct((B, KV, 16, S), F32),
                   jax.ShapeDtypeStruct((T, W), F32)])

    n_chunks = S // NSA_CMP_STRIDE
    nl = NSA_CMP_BLOCK
    token_spec = pl.BlockSpec((1, 1, S, d), lambda b, kv: (b, kv, 0, 0))
    full2 = lambda shape: pl.BlockSpec(shape, lambda b, kv: (0,) * len(shape))
    k_c, v_cT = pl.pallas_call(
        _nsa_compress_kernel,
        out_shape=(jax.ShapeDtypeStruct((B, KV, n_chunks, LANES), BF16),
                   jax.ShapeDtypeStruct((B, KV, d, n_chunks), BF16)),
        grid=(B, KV),
        in_specs=[token_spec, token_spec, full2((nl, 1, d)),
                  full2((nl, d, NSA_CMP_HIDDEN)), full2((NSA_CMP_HIDDEN, LANES)),
                  full2((nl, d, NSA_CMP_HIDDEN)), full2((d, NSA_CMP_HIDDEN))],
        out_specs=(pl.BlockSpec((1, 1, n_chunks, LANES), lambda b, kv: (b, kv, 0, 0)),
                   pl.BlockSpec((1, 1, d, n_chunks), lambda b, kv: (b, kv, 0, 0))),
        compiler_params=_params(2),
        name="nsa_compress",
    )(kc, vc, cmp_pos[:, None, :],
      w_k1.reshape(nl, d, NSA_CMP_HIDDEN).astype(BF16),
      jnp.pad(w_k2, ((0, 0), (0, LANES - d))).astype(BF16),
      w_v1.reshape(nl, d, NSA_CMP_HIDDEN).astype(BF16), w_v2.T.astype(BF16))

    tq = NSA_TQ
    ns = S // NSA_SEL_BLOCK
    ovT = _nsa_overlap_T(n_chunks, ns)
    kp = NSA_KV_PER_STEP
    whole = lambda shape: pl.BlockSpec((1, kp) + shape, lambda b, kv, i: (b, kv) + (0,) * len(shape))
    og = pl.pallas_call(
        functools.partial(_nsa_attn_kernel, n_top=min(NSA_N_SELECT, ns), n_q_tiles=S // tq),
        out_shape=jax.ShapeDtypeStruct((B, S, W), BF16),
        grid=(B, KV // kp, S // tq),
        in_specs=[pl.BlockSpec((1, kp * G, tq, LANES), lambda b, kv, i: (b, kv, i, 0)),
                  whole((n_chunks, LANES)), whole((d, n_chunks)),
                  pl.BlockSpec((ns, n_chunks), lambda b, kv, i: (0, 0)),
                  whole((S, LANES)), whole((S // LANES, d, LANES)),
                  whole((S, LANES)), whole((S // LANES, d, LANES)),
                  pl.BlockSpec((1, kp, 16, tq), lambda b, kv, i: (b, kv, 0, i)),
                  pl.BlockSpec((1, tq, kp * G * d), lambda b, kv, i: (b, i, kv))],
        out_specs=pl.BlockSpec((1, tq, kp * G * d), lambda b, kv, i: (b, i, kv)),
        scratch_shapes=[pltpu.VMEM((kp, G * tq, LANES), BF16),
                        pltpu.VMEM((2, kp, NSA_SEL_TK, G * tq), F32),
                        pltpu.VMEM((kp, 1, G * tq), F32), pltpu.VMEM((kp, 1, G * tq), F32),
                        pltpu.VMEM((kp, d, G * tq), F32), pltpu.VMEM((kp, d, G * tq), F32),
                        pltpu.VMEM((kp, d, G * tq), F32)],
        compiler_params=_params(3),
        name="nsa_attn",
    )(q, k_c, v_cT, ovT, ks, vsT, kw, vwT, gT, sg.reshape(B, S, W))
    return og.reshape(T, W), x2


def kernel(x, c, positions, norm_g, ada_w, ada_b, mla_w_in, mla_q_norm_g, mla_kv_norm_g, mla_w_q_b, mla_w_kv_b, mla_w_out, swa_w_in, swa_sinks, swa_w_out, nsa_w_in, nsa_cmp_pos, nsa_w_cmp_k1, nsa_w_cmp_k2, nsa_w_cmp_v1, nsa_w_cmp_v2, nsa_w_out, final_norm_g):
    B, S, D = x.shape
    depth = norm_g.shape[0]
    cos, sin = _rope_tables(positions)
    mod = _ada_modulation(c, ada_w, ada_b)
    x2 = x.reshape(B * S, D)
    pending = None
    for i in range(depth):
        shift, scale, gate = mod[i, :, :D], mod[i, :, D:2 * D], mod[i, :, 2 * D:]
        kind, j = i % 3, i // 3
        if kind == 0:
            og, x2 = _mla_layer(x2, pending, B, S, norm_g[i], scale, shift, cos, sin, mla_w_in[j],
                                mla_q_norm_g[j], mla_kv_norm_g[j], mla_w_q_b[j], mla_w_kv_b[j])
            w_out = mla_w_out[j]
        elif kind == 1:
            og, x2 = _swa_layer(x2, pending, B, S, norm_g[i], scale, shift, cos, sin, swa_w_in[j], swa_sinks[j])
            w_out = swa_w_out[j]
        else:
            og, x2 = _nsa_layer(x2, pending, B, S, norm_g[i], scale, shift, cos, sin, nsa_w_in[j],
                                nsa_cmp_pos[j], nsa_w_cmp_k1[j], nsa_w_cmp_k2[j], nsa_w_cmp_v1[j],
                                nsa_w_cmp_v2[j])
            w_out = nsa_w_out[j]
        pending = (og, w_out, gate)
    return _final_out_proj(*pending[:2], x2, pending[2], final_norm_g, S).reshape(B, S, D)
```

```python
import functools

import jax
import jax.numpy as jnp
import numpy as np
from jax import lax
from jax.experimental import pallas as pl
from jax.experimental.pallas import tpu as pltpu

F32 = jnp.float32
BF16 = jnp.bfloat16

ROPE_THETA = 10000.0
RMS_EPS = 1e-6
NEG_INF = -1e30
LOG2_E = 1.4426950408889634

MLA_HEADS = 8
MLA_NOPE = 128
MLA_ROPE = 64
MLA_V = 128
MLA_Q_LORA = 256
MLA_KV_LORA = 128

SWA_HEADS = 16
SWA_KV_HEADS = 2
SWA_HEAD_DIM = 64
SWA_WINDOW = 128

NSA_HEADS = 16
NSA_KV_HEADS = 4
NSA_HEAD_DIM = 64
NSA_CMP_BLOCK = 32
NSA_CMP_STRIDE = 16
NSA_CMP_HIDDEN = 128
NSA_SEL_BLOCK = 64
NSA_N_SELECT = 16
NSA_WINDOW = 512
NSA_FORCE_BONUS = 1e4

LANES = 128
ROPE_HALF = 32
VMEM_LIMIT = 56 * 1024 * 1024

PROJ_ROWS = 512
OUT_ROWS = 512
MLA_TQ = 512
MLA_HEADS_PER_STEP = 4
SWA_TQ = 128
NSA_TQ = 128
NSA_SEL_TK = 512
NSA_KV_PER_STEP = 4
NSA_FRONT_VARIANTS = 4


def _dot(a, b):
    return jnp.dot(a, b, preferred_element_type=F32)


def _dot_nt(a, b):
    return lax.dot_general(a, b, (((1,), (1,)), ((), ())), preferred_element_type=F32)


def _silu(x):
    return x / (1.0 + jnp.exp(-x))


def _sigmoid(x):
    return 1.0 / (1.0 + jnp.exp(-x))


def _rms(x, g):
    return x * lax.rsqrt(jnp.mean(x * x, axis=-1, keepdims=True) + RMS_EPS) * g


def _params(n_axes):
    return pltpu.CompilerParams(dimension_semantics=("arbitrary",) * n_axes,
                                vmem_limit_bytes=VMEM_LIMIT)


def _rope128(x, cos, sin_signed):
    lane = lax.broadcasted_iota(jnp.int32, x.shape, 1)
    lower_half = (lane & (2 * ROPE_HALF - 1)) < ROPE_HALF
    partner = jnp.where(lower_half, pltpu.roll(x, LANES - ROPE_HALF, axis=1),
                        pltpu.roll(x, ROPE_HALF, axis=1))
    return x * cos + partner * sin_signed


def _rope_table_kernel(pos_ref, freq_ref, sign_ref, cos_ref, sin_ref):
    ang = pos_ref[0] * freq_ref[...]
    cos_ref[0] = jnp.cos(ang)
    sin_ref[0] = jnp.sin(ang) * sign_ref[...]


def _rope_tables(positions):
    B, S = positions.shape
    inv_freq = ROPE_THETA ** (-jnp.arange(ROPE_HALF, dtype=F32) / ROPE_HALF)
    freq = jnp.tile(inv_freq, LANES // ROPE_HALF)[None, :]
    sign = jnp.tile(jnp.concatenate([-jnp.ones(ROPE_HALF, F32), jnp.ones(ROPE_HALF, F32)]),
                    LANES // (2 * ROPE_HALF))[None, :]
    pos = positions.astype(F32)[..., None]
    ts = min(S, 512)
    spec = pl.BlockSpec((1, ts, LANES), lambda b, s: (b, s, 0))
    return pl.pallas_call(
        _rope_table_kernel,
        out_shape=(jax.ShapeDtypeStruct((B, S, LANES), F32),) * 2,
        grid=(B, S // ts),
        in_specs=[pl.BlockSpec((1, ts, 1), lambda b, s: (b, s, 0)),
                  pl.BlockSpec((1, LANES), lambda b, s: (0, 0)),
                  pl.BlockSpec((1, LANES), lambda b, s: (0, 0))],
        out_specs=(spec, spec),
        compiler_params=_params(2),
        name="rope_tables",
    )(pos, freq, sign)


def _ada_kernel(c_ref, w_ref, b_ref, o_ref):
    cond = _silu(c_ref[...]).astype(BF16)
    o_ref[0] = _dot(cond, w_ref[0].astype(BF16)) + b_ref[0]


def _ada_modulation(c, ada_w, ada_b):
    depth, D, D3 = ada_w.shape
    B = c.shape[0]
    Bp = -(-B // 16) * 16
    c_pad = jnp.pad(c, ((0, Bp - B), (0, 0)))
    tn = 1024
    out = pl.pallas_call(
        _ada_kernel,
        out_shape=jax.ShapeDtypeStruct((depth, Bp, D3), F32),
        grid=(depth, D3 // tn),
        in_specs=[pl.BlockSpec((Bp, D), lambda i, n: (0, 0)),
                  pl.BlockSpec((1, D, tn), lambda i, n: (i, 0, n)),
                  pl.BlockSpec((1, 1, tn), lambda i, n: (i, 0, n))],
        out_specs=pl.BlockSpec((1, Bp, tn), lambda i, n: (i, 0, n)),
        compiler_params=_params(2),
        name="ada_modulation",
    )(c_pad, ada_w, ada_b[:, None, :])
    return out[:, :B]


def _out_proj_kernel(og_ref, w_ref, x_ref, gate_ref, fg_ref, o_ref):
    y = _dot(og_ref[...], w_ref[...])
    o_ref[...] = _rms(x_ref[...] + gate_ref[0] * y, fg_ref[...])


def _final_out_proj(og, w_out, x2, gate_c, final_g, S):
    T, D = x2.shape
    W = og.shape[1]
    tm = OUT_ROWS
    tpb = S // tm
    return pl.pallas_call(
        _out_proj_kernel,
        out_shape=jax.ShapeDtypeStruct((T, D), F32),
        grid=(T // tm,),
        in_specs=[pl.BlockSpec((tm, W), lambda i: (i, 0)),
                  pl.BlockSpec((W, D), lambda i: (0, 0)),
                  pl.BlockSpec((tm, D), lambda i: (i, 0)),
                  pl.BlockSpec((1, 1, D), lambda i: (i // tpb, 0, 0)),
                  pl.BlockSpec((1, D), lambda i: (0, 0))],
        out_specs=pl.BlockSpec((tm, D), lambda i: (i, 0)),
        compiler_params=_params(1),
        name="out_proj",
    )(og, w_out.astype(BF16), x2, gate_c[:, None, :], final_g[None, :])


def _modulated_norm(x_ref, g_ref, scale_ref, shift_ref):
    y = _rms(x_ref[...], g_ref[...])
    return (y * (1.0 + scale_ref[0]) + shift_ref[0]).astype(BF16)


def _with_pending_residual(proj_kernel, n_in, n_out):
    def kernel(x_ref, *refs, **kwargs):
        ins, (og_ref, w_ref, gate_ref) = refs[:n_in - 1], refs[n_in - 1:n_in + 2]
        outs, xo_ref = refs[n_in + 2:n_in + 2 + n_out], refs[n_in + 2 + n_out]
        xo_ref[...] = x_ref[...] + gate_ref[0] * _dot(og_ref[...], w_ref[...])
        proj_kernel(xo_ref, *ins, *outs, **kwargs)
    return kernel


def _run_proj(proj_kernel, name, S, pending, in_specs, args, out_specs, out_shape):
    x2 = args[0]
    T, D = x2.shape
    tm = PROJ_ROWS
    tpb = S // tm
    if pending is not None:
        og, w_out, gate_c = pending
        W = og.shape[1]
        proj_kernel = _with_pending_residual(proj_kernel, len(in_specs), len(out_specs))
        in_specs = in_specs + [pl.BlockSpec((tm, W), lambda i: (i, 0)),
                               pl.BlockSpec((W, D), lambda i: (0, 0)),
                               pl.BlockSpec((1, 1, D), lambda i: (i // tpb, 0, 0))]
        args = args + [og, w_out.astype(BF16), gate_c[:, None, :]]
        out_specs = out_specs + [pl.BlockSpec((tm, D), lambda i: (i, 0))]
        out_shape = out_shape + [jax.ShapeDtypeStruct((T, D), F32)]
    res = pl.pallas_call(
        proj_kernel,
        out_shape=tuple(out_shape),
        grid=(T // tm,),
        in_specs=in_specs,
        out_specs=tuple(out_specs),
        compiler_params=_params(1),
        name=name,
    )(*args)
    if pending is not None:
        return res[:-1], res[-1]
    return res, x2


def _store_heads(dst_ref, val, n_heads, width, dtype):
    for h in range(n_heads):
        dst_ref[0, h] = val[:, h * width:(h + 1) * width].astype(dtype)


def _store_heads_padded(dst_ref, val, n_heads, upper=None):
    half = LANES // 2
    lane = lax.broadcasted_iota(jnp.int32, (val.shape[0], LANES), 1)
    fill = 0.0 if upper is None else upper
    for h in range(n_heads):
        chunk = val[:, (h // 2) * LANES:(h // 2 + 1) * LANES]
        if h % 2:
            chunk = pltpu.roll(chunk, half, axis=1)
        dst_ref[0, h] = jnp.where(lane < half, chunk, fill).astype(BF16)


def _store_slabs(dst_ref, val, n_heads, d):
    for h in range(n_heads):
        for c in range(val.shape[1] // LANES):
            dst_ref[0, h, c] = val[h * d:(h + 1) * d, c * LANES:(c + 1) * LANES].astype(BF16)


def _rope_wide(val, cos, sin_signed):
    return jnp.concatenate(
        [_rope128(val[:, c * LANES:(c + 1) * LANES], cos, sin_signed)
         for c in range(val.shape[1] // LANES)], axis=1)


def _mla_proj_kernel(x_ref, scale_ref, shift_ref, g_ref, cos_ref, sin_ref,
                     wqa_ref, wkva_ref, wkpe_ref, wgate_ref, qg_ref, kvg_ref,
                     wqb_ref, wkb_ref, wvbT_ref,
                     q_ref, k_ref, vT_ref, sg_ref):
    h = _modulated_norm(x_ref, g_ref, scale_ref, shift_ref)
    cos, sin = cos_ref[0], sin_ref[0]
    head_w = MLA_NOPE + LANES
    q_scale = (MLA_NOPE + MLA_ROPE) ** -0.5 * LOG2_E

    qn = _rms(_dot(h, wqa_ref[...]), qg_ref[...]).astype(BF16)
    q = _dot(qn, wqb_ref[...]) * q_scale
    for hd in range(MLA_HEADS):
        lo = hd * head_w
        q_ref[:, lo:lo + MLA_NOPE] = q[:, lo:lo + MLA_NOPE].astype(BF16)
        q_ref[:, lo + MLA_NOPE:lo + head_w] = _rope128(
            q[:, lo + MLA_NOPE:lo + head_w], cos, sin).astype(BF16)

    kvn = _rms(_dot(h, wkva_ref[...]), kvg_ref[...]).astype(BF16)
    kn = _dot(kvn, wkb_ref[...])
    kpe = _rope128(_dot(h, wkpe_ref[...]), cos, sin).astype(BF16)
    for hd in range(MLA_HEADS):
        lo = hd * head_w
        k_ref[:, lo:lo + MLA_NOPE] = kn[:, hd * MLA_NOPE:(hd + 1) * MLA_NOPE].astype(BF16)
        k_ref[:, lo + MLA_NOPE:lo + head_w] = kpe
    _store_slabs(vT_ref, _dot_nt(wvbT_ref[...], kvn), MLA_HEADS, MLA_V)

    sg_ref[...] = _silu(_dot(h, wgate_ref[...]))


def _mla_attn_kernel(q_ref, k_ref, vT_ref, sg_ref, o_ref, s_ref, m_ref, l_ref, acc_ref):
    qi = pl.program_id(2)
    tq = tk = MLA_TQ
    slabs = tk // LANES
    head_w = MLA_NOPE + LANES
    heads = range(MLA_HEADS_PER_STEP)

    def scores(slot, kb):
        rows = pl.ds(pl.multiple_of(kb * tk, tk), tk)
        for h in heads:
            cols = slice(h * head_w, (h + 1) * head_w)
            s_ref[slot, h] = _dot_nt(k_ref[0, rows, cols], q_ref[0, :, cols])

    def absorb(slot, kb, masked):
        if masked:
            kpos = lax.broadcasted_iota(jnp.int32, (tk, tq), 0)
            qpos = lax.broadcasted_iota(jnp.int32, (tk, tq), 1)
            visible = kpos <= qpos
        for h in heads:
            s = s_ref[slot, h]
            if masked:
                s = jnp.where(visible, s, NEG_INF)
            m = m_ref[h]
            m_new = jnp.maximum(m, jnp.max(s, axis=0, keepdims=True))
            alpha = jnp.exp2(m - m_new)
            p = jnp.exp2(s - m_new)
            m_ref[h] = m_new
            l_ref[h] = alpha * l_ref[h] + jnp.sum(p, axis=0, keepdims=True)
            vt = jnp.concatenate([vT_ref[0, h, kb * slabs + c] for c in range(slabs)], axis=1)
            acc_ref[h] = alpha * acc_ref[h] + _dot(vt, p.astype(BF16))

    m_ref[...] = jnp.full(m_ref.shape, NEG_INF, F32)
    l_ref[...] = jnp.zeros(l_ref.shape, F32)
    acc_ref[...] = jnp.zeros(acc_ref.shape, F32)
    scores(0, 0)

    def pair(j, carry):
        kb = 2 * j
        scores(1, kb + 1)
        absorb(0, kb, False)
        scores(0, kb + 2)
        absorb(1, kb + 1, False)
        return carry

    lax.fori_loop(0, lax.shift_right_logical(qi, 1), pair, 0)

    @pl.when((qi & 1) == 1)
    def _():
        scores(1, qi)
        absorb(0, qi - 1, False)
        absorb(1, qi, True)

    @pl.when((qi & 1) == 0)
    def _():
        absorb(0, qi, True)

    for h in heads:
        cols = slice(h * MLA_V, (h + 1) * MLA_V)
        o = (acc_ref[h] * (1.0 / l_ref[h])).T
        o_ref[0, :, cols] = (o * sg_ref[0, :, cols]).astype(BF16)


def _mla_layer(x2, pending, B, S, norm_g, scale, shift, cos, sin, w_in, q_norm_g, kv_norm_g, w_q_b, w_kv_b):
    T, D = x2.shape
    H = MLA_HEADS
    head_w = MLA_NOPE + LANES
    o1 = MLA_Q_LORA
    o2 = o1 + MLA_KV_LORA
    o3 = o2 + MLA_ROPE
    w_qa, w_kva, w_kpe, w_gate = (w_in[:, :o1], w_in[:, o1:o2], w_in[:, o2:o3], w_in[:, o3:])
    w_kpe = jnp.pad(w_kpe, ((0, 0), (0, LANES - MLA_ROPE)))
    wq = w_q_b.reshape(MLA_Q_LORA, H, MLA_NOPE + MLA_ROPE)
    wq = jnp.pad(wq, ((0, 0), (0, 0), (0, head_w - MLA_NOPE - MLA_ROPE))).reshape(MLA_Q_LORA, H * head_w)
    wkv = w_kv_b.reshape(MLA_KV_LORA, H, MLA_NOPE + MLA_V)
    w_kb = wkv[:, :, :MLA_NOPE].reshape(MLA_KV_LORA, H * MLA_NOPE)
    w_vbT = wkv[:, :, MLA_NOPE:].reshape(MLA_KV_LORA, H * MLA_V).T
    W = H * MLA_V

    tm = PROJ_ROWS
    tpb = S // tm
    full = lambda shape: pl.BlockSpec(shape, lambda i: (0,) * len(shape))
    rows = lambda w: pl.BlockSpec((tm, w), lambda i: (i, 0))
    per_b = pl.BlockSpec((1, 1, D), lambda i: (i // tpb, 0, 0))
    tab = pl.BlockSpec((1, tm, LANES), lambda i: (i // tpb, i % tpb, 0))
    (q, k, vT, sg), x2 = _run_proj(
        _mla_proj_kernel, "mla_proj", S, pending,
        in_specs=[rows(D), per_b, per_b, full((1, D)), tab, tab,
                  full((D, o1)), full((D, MLA_KV_LORA)), full((D, LANES)), full((D, W)),
                  full((1, o1)), full((1, MLA_KV_LORA)),
                  full((o1, H * head_w)), full((MLA_KV_LORA, H * MLA_NOPE)), full((W, MLA_KV_LORA))],
        args=[x2, scale[:, None, :], shift[:, None, :], norm_g[None, :], cos, sin,
              w_qa.astype(BF16), w_kva.astype(BF16), w_kpe.astype(BF16), w_gate.astype(BF16),
              q_norm_g[None, :], kv_norm_g[None, :],
              wq.astype(BF16), w_kb.astype(BF16), w_vbT.astype(BF16)],
        out_specs=[rows(H * head_w), rows(H * head_w),
                   pl.BlockSpec((1, H, tm // LANES, MLA_V, LANES), lambda i: (i // tpb, 0, i % tpb, 0, 0)),
                   rows(W)],
        out_shape=[jax.ShapeDtypeStruct((T, H * head_w), BF16),
                   jax.ShapeDtypeStruct((T, H * head_w), BF16),
                   jax.ShapeDtypeStruct((B, H, S // LANES, MLA_V, LANES), BF16),
                   jax.ShapeDtypeStruct((T, W), F32)])

    tq = MLA_TQ
    hp = MLA_HEADS_PER_STEP
    q3 = q.reshape(B, S, H * head_w)
    k3 = k.reshape(B, S, H * head_w)
    sg3 = sg.reshape(B, S, W)
    og = pl.pallas_call(
        _mla_attn_kernel,
        out_shape=jax.ShapeDtypeStruct((B, S, W), BF16),
        grid=(B, H // hp, S // tq),
        in_specs=[pl.BlockSpec((1, tq, hp * head_w), lambda b, h, i: (b, i, h)),
                  pl.BlockSpec((1, S, hp * head_w), lambda b, h, i: (b, 0, h)),
                  pl.BlockSpec((1, hp, S // LANES, MLA_V, LANES), lambda b, h, i: (b, h, 0, 0, 0)),
                  pl.BlockSpec((1, tq, hp * MLA_V), lambda b, h, i: (b, i, h))],
        out_specs=pl.BlockSpec((1, tq, hp * MLA_V), lambda b, h, i: (b, i, h)),
        scratch_shapes=[pltpu.VMEM((2, hp, tq, tq), F32),
                        pltpu.VMEM((hp, 1, tq), F32), pltpu.VMEM((hp, 1, tq), F32),
                        pltpu.VMEM((hp, MLA_V, tq), F32)],
        compiler_params=_params(3),
        name="mla_attn",
    )(q3, k3, vT, sg3)
    return og.reshape(T, W), x2


def _swa_proj_kernel(x_ref, scale_ref, shift_ref, g_ref, cos_ref, sin_ref,
                     wq_ref, wk_ref, wvT_ref, wgate_ref,
                     q_ref, k_ref, vT_ref, sg_ref):
    h = _modulated_norm(x_ref, g_ref, scale_ref, shift_ref)
    cos, sin = cos_ref[0], sin_ref[0]
    q = _rope_wide(_dot(h, wq_ref[...]) * (SWA_HEAD_DIM ** -0.5 * LOG2_E), cos, sin)
    _store_heads(q_ref, q, SWA_HEADS, SWA_HEAD_DIM, BF16)
    k = _rope_wide(_dot(h, wk_ref[...]), cos, sin)
    _store_heads(k_ref, k, SWA_KV_HEADS, SWA_HEAD_DIM, BF16)
    _store_slabs(vT_ref, _dot_nt(wvT_ref[...], h), SWA_KV_HEADS, SWA_HEAD_DIM)
    sg_ref[...] = _silu(_dot(h, wgate_ref[...]))


def _band_valid(diff, window):
    return lax.bitcast_convert_type(diff, jnp.uint32) < jnp.uint32(window)


def _swa_attn_kernel(q_ref, k_ref, vT_ref, sink_ref, sg_ref, o_ref):
    qi = pl.program_id(1)
    tq = SWA_TQ
    G = SWA_HEADS // SWA_KV_HEADS
    d = SWA_HEAD_DIM
    n = G * tq
    groups = range(SWA_KV_HEADS)
    assert tq == SWA_WINDOW == LANES
    prev = jnp.maximum(qi - 1, 0)
    q = [q_ref[0, i * G:(i + 1) * G].reshape(n, d) for i in groups]
    s_own = [_dot_nt(k_ref[0, i, pl.ds(pl.multiple_of(qi * tq, tq), tq), :], q[i]) for i in groups]
    s_prev = [_dot_nt(k_ref[0, i, pl.ds(pl.multiple_of(prev * tq, tq), tq), :], q[i]) for i in groups]
    row = lax.broadcasted_iota(jnp.int32, (tq, n), 0)
    col = lax.broadcasted_iota(jnp.int32, (tq, n), 1) & (tq - 1)
    own = row <= col
    first = qi == 0
    for i in groups:
        s = jnp.where(own, s_own[i], jnp.where(first, NEG_INF, s_prev[i]))
        sink = sink_ref[i]
        m = jnp.maximum(jnp.max(s, axis=0, keepdims=True), sink)
        e = jnp.exp2(s - m)
        denom = jnp.sum(e, axis=0, keepdims=True) + jnp.exp2(sink - m)
        e2 = jnp.concatenate([jnp.where(own, 0.0, e), jnp.where(own, e, 0.0)], axis=0).astype(BF16)
        vt = jnp.concatenate([vT_ref[0, i, prev], vT_ref[0, i, qi]], axis=1)
        oT = _dot(vt, e2) * (1.0 / denom)
        o = jnp.concatenate([oT[:, g * tq:(g + 1) * tq] for g in range(G)], axis=0).T
        cols = slice(i * G * d, (i + 1) * G * d)
        o_ref[0, :, cols] = (o * sg_ref[0, :, cols]).astype(BF16)


def _swa_layer(x2, pending, B, S, norm_g, scale, shift, cos, sin, w_in, sinks):
    T, D = x2.shape
    H, KV, d = SWA_HEADS, SWA_KV_HEADS, SWA_HEAD_DIM
    G = H // KV
    W = H * d
    KW = KV * d
    w_q, w_k, w_v, w_gate = (w_in[:, :W], w_in[:, W:W + KW], w_in[:, W + KW:W + 2 * KW],
                             w_in[:, W + 2 * KW:])
    tm = PROJ_ROWS
    tpb = S // tm
    full = lambda shape: pl.BlockSpec(shape, lambda i: (0,) * len(shape))
    rows = lambda w: pl.BlockSpec((tm, w), lambda i: (i, 0))
    per_b = pl.BlockSpec((1, 1, D), lambda i: (i // tpb, 0, 0))
    tab = pl.BlockSpec((1, tm, LANES), lambda i: (i // tpb, i % tpb, 0))
    heads = lambda nh: pl.BlockSpec((1, nh, tm, d), lambda i: (i // tpb, 0, i % tpb, 0))
    (q, k, vT, sg), x2 = _run_proj(
        _swa_proj_kernel, "swa_proj", S, pending,
        in_specs=[rows(D), per_b, per_b, full((1, D)), tab, tab,
                  full((D, W)), full((D, KW)), full((KW, D)), full((D, W))],
        args=[x2, scale[:, None, :], shift[:, None, :], norm_g[None, :], cos, sin,
              w_q.astype(BF16), w_k.astype(BF16), w_v.T.astype(BF16), w_gate.astype(BF16)],
        out_specs=[heads(H), heads(KV),
                   pl.BlockSpec((1, KV, tm // LANES, d, LANES), lambda i: (i // tpb, 0, i % tpb, 0, 0)),
                   rows(W)],
        out_shape=[jax.ShapeDtypeStruct((B, H, S, d), BF16),
                   jax.ShapeDtypeStruct((B, KV, S, d), BF16),
                   jax.ShapeDtypeStruct((B, KV, S // LANES, d, LANES), BF16),
                   jax.ShapeDtypeStruct((T, W), F32)])

    tq = SWA_TQ
    n = G * tq
    sink_rows = jnp.repeat(sinks.astype(F32).reshape(KV, G) * LOG2_E, tq, axis=1)[:, None, :]
    og = pl.pallas_call(
        _swa_attn_kernel,
        out_shape=jax.ShapeDtypeStruct((B, S, W), BF16),
        grid=(B, S // tq),
        in_specs=[pl.BlockSpec((1, H, tq, d), lambda b, i: (b, 0, i, 0)),
                  pl.BlockSpec((1, KV, S, d), lambda b, i: (b, 0, 0, 0)),
                  pl.BlockSpec((1, KV, S // LANES, d, LANES), lambda b, i: (b, 0, 0, 0, 0)),
                  pl.BlockSpec((KV, 1, n), lambda b, i: (0, 0, 0)),
                  pl.BlockSpec((1, tq, W), lambda b, i: (b, i, 0))],
        out_specs=pl.BlockSpec((1, tq, W), lambda b, i: (b, i, 0)),
        compiler_params=_params(2),
        name="swa_attn",
    )(q, k, vT, sink_rows, sg.reshape(B, S, W))
    return og.reshape(T, W), x2


def _nsa_proj_kernel(x_ref, scale_ref, shift_ref, g_ref, cos_ref, sin_ref,
                     wq_ref, wkc_ref, wvc_ref, wks_ref, wvsT_ref, wkw_ref, wvwT_ref, wgT_ref, wgate_ref,
                     q_ref, kc_ref, vc_ref, ks_ref, vsT_ref, kw_ref, vwT_ref, gT_ref, sg_ref, *, tiles_per_seq):
    h = _modulated_norm(x_ref, g_ref, scale_ref, shift_ref)
    cos, sin = cos_ref[0], sin_ref[0]
    KV, d = NSA_KV_HEADS, NSA_HEAD_DIM
    tm = h.shape[0]
    q = _rope_wide(_dot(h, wq_ref[...]) * (d ** -0.5 * LOG2_E), cos, sin)
    _store_heads_padded(q_ref, q, NSA_HEADS)
    _store_heads(kc_ref, _dot(h, wkc_ref[...]), KV, d, F32)
    _store_heads(vc_ref, _dot(h, wvc_ref[...]), KV, d, F32)
    pos = (pl.program_id(0) % tiles_per_seq) * tm + lax.broadcasted_iota(jnp.int32, (tm, LANES), 0)
    lane = lax.broadcasted_iota(jnp.int32, (tm, LANES), 1)
    sel_shift = NSA_SEL_BLOCK.bit_length() - 1
    block_onehot = jnp.where(lane - LANES // 2 == lax.shift_right_logical(pos, sel_shift), 1.0, 0.0)
    _store_heads_padded(ks_ref, _rope_wide(_dot(h, wks_ref[...]), cos, sin), KV, upper=block_onehot)
    _store_heads_padded(kw_ref, _rope_wide(_dot(h, wkw_ref[...]), cos, sin), KV)
    _store_slabs(vsT_ref, _dot_nt(wvsT_ref[...], h), KV, d)
    _store_slabs(vwT_ref, _dot_nt(wvwT_ref[...], h), KV, d)
    gT = _sigmoid(_dot_nt(wgT_ref[...], h))
    for kv in range(KV):
        gT_ref[0, kv] = gT[kv * 16:(kv + 1) * 16, :]
    sg_ref[...] = _silu(_dot(h, wgate_ref[...]))


def _nsa_compress_kernel(kc_ref, vc_ref, pe_ref, wk1_ref, wk2_ref, wv1_ref, wv2T_ref, kout_ref, vT_ref):
    n_chunks = kout_ref.shape[2]
    stride = NSA_CMP_STRIDE

    def hidden(x_ref, w1_ref):
        top = bot = None
        for l in range(stride):
            x_l = x_ref[0, 0, pl.ds(l, n_chunks, stride=stride), :]
            t = _dot((x_l + pe_ref[l]).astype(BF16), w1_ref[l])
            b = _dot((x_l + pe_ref[stride + l]).astype(BF16), w1_ref[stride + l])
            top = t if top is None else top + t
            bot = b if bot is None else bot + b
        return _silu(top + pltpu.roll(bot, n_chunks - 1, axis=0)).astype(BF16)

    hk = hidden(kc_ref, wk1_ref)
    kc = _dot(hk, wk2_ref[...])
    row = lax.broadcasted_iota(jnp.int32, kc.shape, 0)
    kout_ref[0, 0] = jnp.where(row < n_chunks - 1, kc, 0.0).astype(BF16)
    hv = hidden(vc_ref, wv1_ref)
    vT = _dot_nt(wv2T_ref[...], hv)
    col = lax.broadcasted_iota(jnp.int32, vT.shape, 1)
    vT_ref[0, 0] = jnp.where(col < n_chunks - 1, vT, 0.0).astype(BF16)


def _nsa_attn_kernel(q_ref, kc_ref, vcT_ref, ovT_ref, ks_ref, vsT_ref, kw_ref, vwT_ref, gT_ref, sg_ref,
                     o_ref, qsel_ref, s_ref, m_ref, l_ref, acc_ref, oc_ref, ow_ref, *, n_top, n_q_tiles):
    qi = pl.program_id(2)
    tq = NSA_TQ
    G = NSA_HEADS // NSA_KV_HEADS
    d = NSA_HEAD_DIM
    n = G * tq
    groups = range(NSA_KV_PER_STEP)
    q0 = qi * tq
    q = [q_ref[0, i * G:(i + 1) * G].reshape(n, LANES) for i in groups]

    def col_qpos(shape):
        return q0 + (lax.broadcasted_iota(jnp.int32, shape, 1) & (tq - 1))

    tk = NSA_SEL_TK
    slabs = tk // LANES

    def sel_scores(slot, kb):
        rows = pl.ds(pl.multiple_of(kb * tk, tk), tk)
        for i in groups:
            s_ref[slot, i] = _dot_nt(ks_ref[0, i, rows, :], qsel_ref[i])

    def sel_absorb(slot, kb, diagonal, rows=tk):
        if diagonal:
            kpos = kb * tk + lax.broadcasted_iota(jnp.int32, (rows, n), 0)
            keep = kpos <= col_qpos((rows, n))
        for i in groups:
            s = s_ref[slot, i, :rows, :]
            if diagonal:
                s = jnp.where(keep, s, NEG_INF)
            m = m_ref[i]
            m_new = jnp.maximum(m, jnp.max(s, axis=0, keepdims=True))
            alpha = jnp.exp2(m - m_new)
            p = jnp.exp2(s - m_new)
            m_ref[i] = m_new
            l_ref[i] = alpha * l_ref[i] + jnp.sum(p, axis=0, keepdims=True)
            vt = jnp.concatenate([vsT_ref[0, i, kb * slabs + c] for c in range(rows // LANES)], axis=1)
            acc_ref[i] = alpha * acc_ref[i] + _dot(vt, p.astype(BF16))

    def sel_absorb_diagonal(slot):
        tiles_per_step = tk // tq
        for r in range(tiles_per_step):
            @pl.when((qi & (tiles_per_step - 1)) == r)
            def _(r=r):
                sel_absorb(slot, kb_diag, True, rows=(r + 1) * tq)

    m_ref[...] = jnp.full(m_ref.shape, NEG_INF, F32)
    l_ref[...] = jnp.zeros(l_ref.shape, F32)
    acc_ref[...] = jnp.zeros(acc_ref.shape, F32)
    kb_diag = lax.shift_right_logical(q0, tk.bit_length() - 1)
    span = NSA_WINDOW + tq
    blk0 = jnp.maximum(qi * (tq // LANES) - NSA_WINDOW // LANES, 0)
    base = pl.multiple_of(blk0 * LANES, LANES)
    sel_shift = NSA_SEL_BLOCK.bit_length() - 1
    half = LANES // 2

    def front(ns_eff, nc_eff, full_band):
        s_cmp = [_dot_nt(kc_ref[0, i, :nc_eff, :], q[i]) for i in groups]
        s_win = [_dot_nt(kw_ref[0, i, pl.ds(base, span), :], q[i]) for i in groups]

        cmp_end = lax.broadcasted_iota(jnp.int32, (nc_eff, n), 0) * NSA_CMP_STRIDE + (NSA_CMP_BLOCK - 1)
        valid = cmp_end <= col_qpos((nc_eff, n))
        imp = []
        for i in groups:
            s = jnp.where(valid, s_cmp[i], NEG_INF)
            m = jnp.max(s, axis=0, keepdims=True)
            e = jnp.where(valid, jnp.exp2(s - m), 0.0)
            l = jnp.sum(e, axis=0, keepdims=True)
            p = (e * jnp.where(l > 0.0, 1.0 / l, 0.0)).astype(BF16)
            oc_ref[i] = _dot(vcT_ref[0, i, :, :nc_eff], p)
            imp_all = _dot(ovT_ref[:ns_eff, :nc_eff], p)
            acc = imp_all[:, 0:tq]
            for g in range(1, G):
                acc = acc + imp_all[:, g * tq:(g + 1) * tq]
            imp.append(acc)

        blk = lax.broadcasted_iota(jnp.int32, (ns_eff, tq), 0)
        q_blk = lax.shift_right_logical(q0 + lax.broadcasted_iota(jnp.int32, (ns_eff, tq), 1), sel_shift)
        causal = blk <= q_blk
        forced = (blk == 0) | (blk == q_blk) | (blk == q_blk - 1)
        lane = lax.broadcasted_iota(jnp.int32, (G, tq, LANES), 2)
        for i in groups:
            val = jnp.where(causal, imp[i] + jnp.where(forced, NSA_FORCE_BONUS, 0.0), -1.0)
            rank = jnp.zeros(val.shape, F32)
            for r in range(ns_eff):
                row = val[r:r + 1, :]
                rank = rank + jnp.where(blk > r, jnp.where(row >= val, 1.0, 0.0), jnp.where(row > val, 1.0, 0.0))
            bias = jnp.where(causal, jnp.where(rank < n_top, 0.0, NEG_INF), NEG_INF)
            bias_rows = [jnp.zeros((half, tq), F32), bias]
            if ns_eff < half:
                bias_rows.append(jnp.full((half - ns_eff, tq), NEG_INF, F32))
            bias_rows = jnp.concatenate(bias_rows, axis=0).T.astype(BF16)
            qsel_ref[i] = jnp.where(lane < half, q[i].reshape(G, tq, LANES), bias_rows[None]).reshape(n, LANES)

        sel_scores(0, 0)

        vt = [jnp.concatenate([vwT_ref[0, i, blk0 + c] for c in range(span // LANES)], axis=1) for i in groups]
        if full_band:
            row = lax.broadcasted_iota(jnp.int32, (tq, n), 0)
            last = row <= (lax.broadcasted_iota(jnp.int32, (tq, n), 1) & (tq - 1))
            for i in groups:
                edge = jnp.where(last, s_win[i][span - tq:], s_win[i][:tq])
                mid = s_win[i][tq:span - tq]
                m = jnp.maximum(jnp.max(edge, axis=0, keepdims=True), jnp.max(mid, axis=0, keepdims=True))
                e_edge = jnp.exp2(edge - m)
                e_mid = jnp.exp2(mid - m)
                denom = jnp.sum(e_edge, axis=0, keepdims=True) + jnp.sum(e_mid, axis=0, keepdims=True)
                e = jnp.concatenate([jnp.where(last, 0.0, e_edge), e_mid, jnp.where(last, e_edge, 0.0)], axis=0)
                ow_ref[i] = _dot(vt[i], e.astype(BF16)) * (1.0 / denom)
        else:
            diff = col_qpos((span, n)) - (base + lax.broadcasted_iota(jnp.int32, (span, n), 0))
            in_band = _band_valid(diff, NSA_WINDOW)
            for i in groups:
                s = jnp.where(in_band, s_win[i], NEG_INF)
                e = jnp.exp2(s - jnp.max(s, axis=0, keepdims=True))
                ow_ref[i] = _dot(vt[i], e.astype(BF16)) * (1.0 / jnp.sum(e, axis=0, keepdims=True))

    ns, nc_pad = ovT_ref.shape
    n_var = min(NSA_FRONT_VARIANTS, n_q_tiles)
    per = n_q_tiles // n_var
    for v in range(n_var):
        ns_eff = min(ns, -(-(ns * (v + 1)) // (n_var * 16)) * 16)
        nc_eff = min(nc_pad, -(-(nc_pad * (v + 1)) // (n_var * LANES)) * LANES)
        hi = n_q_tiles if v == n_var - 1 else (v + 1) * per

        full_band = v * per * tq >= NSA_WINDOW

        @pl.when((qi >= v * per) & (qi < hi))
        def _(ns_eff=ns_eff, nc_eff=nc_eff, full_band=full_band):
            front(ns_eff, nc_eff, full_band)

    def sel_pair(j, carry):
        kb = 2 * j
        sel_scores(1, kb + 1)
        sel_absorb(0, kb, False)
        sel_scores(0, kb + 2)
        sel_absorb(1, kb + 1, False)
        return carry

    lax.fori_loop(0, lax.shift_right_logical(kb_diag, 1), sel_pair, 0)

    @pl.when((kb_diag & 1) == 1)
    def _():
        sel_scores(1, kb_diag)
        sel_absorb(0, kb_diag - 1, False)
        sel_absorb_diagonal(1)

    @pl.when((kb_diag & 1) == 0)
    def _():
        sel_absorb_diagonal(0)

    for i in groups:
        o_s = acc_ref[i] * (1.0 / l_ref[i])
        gates = gT_ref[0, i]
        outs = []
        for g in range(G):
            cols = slice(g * tq, (g + 1) * tq)
            outs.append(gates[3 * g:3 * g + 1, :] * oc_ref[i][:, cols]
                        + gates[3 * g + 1:3 * g + 2, :] * o_s[:, cols]
                        + gates[3 * g + 2:3 * g + 3, :] * ow_ref[i][:, cols])
        o = jnp.concatenate(outs, axis=0).T
        cols = slice(i * G * d, (i + 1) * G * d)
        o_ref[0, :, cols] = (o * sg_ref[0, :, cols]).astype(BF16)


def _nsa_overlap_T(nc_pad, ns):
    nc = nc_pad - (NSA_CMP_BLOCK // NSA_CMP_STRIDE - 1)
    cs = np.arange(nc_pad)[None, :] * NSA_CMP_STRIDE
    ss = np.arange(ns)[:, None] * NSA_SEL_BLOCK
    ov = np.clip(np.minimum(cs + NSA_CMP_BLOCK, ss + NSA_SEL_BLOCK) - np.maximum(cs, ss), 0, None)
    ov = np.where(np.arange(nc_pad)[None, :] < nc, ov, 0)
    return jnp.asarray(ov / NSA_CMP_BLOCK, dtype=BF16)


def _nsa_layer(x2, pending, B, S, norm_g, scale, shift, cos, sin, w_in, cmp_pos, w_k1, w_k2, w_v1, w_v2):
    T, D = x2.shape
    H, KV, d = NSA_HEADS, NSA_KV_HEADS, NSA_HEAD_DIM
    G = H // KV
    W = H * d
    KW = KV * d
    offs = np.cumsum([0, W] + [KW] * 6 + [3 * H, W])
    w_q, w_kc, w_vc, w_ks, w_vs, w_kw, w_vw, w_g, w_gate = (
        w_in[:, offs[i]:offs[i + 1]] for i in range(9))
    w_gT = jnp.pad(w_g.T.reshape(KV, 3 * G, D), ((0, 0), (0, 16 - 3 * G), (0, 0))).reshape(KV * 16, D)

    tm = PROJ_ROWS
    tpb = S // tm
    full = lambda shape: pl.BlockSpec(shape, lambda i: (0,) * len(shape))
    rows = lambda w: pl.BlockSpec((tm, w), lambda i: (i, 0))
    per_b = pl.BlockSpec((1, 1, D), lambda i: (i // tpb, 0, 0))
    tab = pl.BlockSpec((1, tm, LANES), lambda i: (i // tpb, i % tpb, 0))
    heads = lambda nh: pl.BlockSpec((1, nh, tm, d), lambda i: (i // tpb, 0, i % tpb, 0))
    wide = lambda nh: pl.BlockSpec((1, nh, tm, LANES), lambda i: (i // tpb, 0, i % tpb, 0))
    slab = pl.BlockSpec((1, KV, tm // LANES, d, LANES), lambda i: (i // tpb, 0, i % tpb, 0, 0))
    kv_f32 = jax.ShapeDtypeStruct((B, KV, S, d), F32)
    kv_bf16 = jax.ShapeDtypeStruct((B, KV, S, LANES), BF16)
    kv_slab = jax.ShapeDtypeStruct((B, KV, S // LANES, d, LANES), BF16)
    assert S // NSA_SEL_BLOCK <= LANES // 2, "selection-block one-hot must fit in lanes 64..127"
    (q, kc, vc, ks, vsT, kw, vwT, gT, sg), x2 = _run_proj(
        functools.partial(_nsa_proj_kernel, tiles_per_seq=tpb), "nsa_proj", S, pending,
        in_specs=[rows(D), per_b, per_b, full((1, D)), tab, tab,
                  full((D, W)), full((D, KW)), full((D, KW)), full((D, KW)), full((KW, D)),
                  full((D, KW)), full((KW, D)), full((KV * 16, D)), full((D, W))],
        args=[x2, scale[:, None, :], shift[:, None, :], norm_g[None, :], cos, sin,
              w_q.astype(BF16), w_kc.astype(BF16), w_vc.astype(BF16), w_ks.astype(BF16),
              w_vs.T.astype(BF16), w_kw.astype(BF16), w_vw.T.astype(BF16), w_gT.astype(BF16),
              w_gate.astype(BF16)],
        out_specs=[wide(H), heads(KV), heads(KV), wide(KV), slab, wide(KV), slab,
                   pl.BlockSpec((1, KV, 16, tm), lambda i: (i // tpb, 0, 0, i % tpb)),
                   rows(W)],
        out_shape=[jax.ShapeDtypeStruct((B, H, S, LANES), BF16), kv_f32, kv_f32,
                   kv_bf16, kv_slab, kv_bf16, kv_slab,
                   jax.ShapeDtypeStruct((B, KV, 16, S), F32),
                   jax.ShapeDtypeStruct((T, W), F32)])

    n_chunks = S // NSA_CMP_STRIDE
    nl = NSA_CMP_BLOCK
    token_spec = pl.BlockSpec((1, 1, S, d), lambda b, kv: (b, kv, 0, 0))
    full2 = lambda shape: pl.BlockSpec(shape, lambda b, kv: (0,) * len(shape))
    k_c, v_cT = pl.pallas_call(
        _nsa_compress_kernel,
        out_shape=(jax.ShapeDtypeStruct((B, KV, n_chunks, LANES), BF16),
                   jax.ShapeDtypeStruct((B, KV, d, n_chunks), BF16)),
        grid=(B, KV),
        in_specs=[token_spec, token_spec, full2((nl, 1, d)),
                  full2((nl, d, NSA_CMP_HIDDEN)), full2((NSA_CMP_HIDDEN, LANES)),
                  full2((nl, d, NSA_CMP_HIDDEN)), full2((d, NSA_CMP_HIDDEN))],
        out_specs=(pl.BlockSpec((1, 1, n_chunks, LANES), lambda b, kv: (b, kv, 0, 0)),
                   pl.BlockSpec((1, 1, d, n_chunks), lambda b, kv: (b, kv, 0, 0))),
        compiler_params=_params(2),
        name="nsa_compress",
    )(kc, vc, cmp_pos[:, None, :],
      w_k1.reshape(nl, d, NSA_CMP_HIDDEN).astype(BF16),
      jnp.pad(w_k2, ((0, 0), (0, LANES - d))).astype(BF16),
      w_v1.reshape(nl, d, NSA_CMP_HIDDEN).astype(BF16), w_v2.T.astype(BF16))

    tq = NSA_TQ
    ns = S // NSA_SEL_BLOCK
    ovT = _nsa_overlap_T(n_chunks, ns)
    kp = NSA_KV_PER_STEP
    whole = lambda shape: pl.BlockSpec((1, kp) + shape, lambda b, kv, i: (b, kv) + (0,) * len(shape))
    og = pl.pallas_call(
        functools.partial(_nsa_attn_kernel, n_top=min(NSA_N_SELECT, ns), n_q_tiles=S // tq),
        out_shape=jax.ShapeDtypeStruct((B, S, W), BF16),
        grid=(B, KV // kp, S // tq),
        in_specs=[pl.BlockSpec((1, kp * G, tq, LANES), lambda b, kv, i: (b, kv, i, 0)),
                  whole((n_chunks, LANES)), whole((d, n_chunks)),
                  pl.BlockSpec((ns, n_chunks), lambda b, kv, i: (0, 0)),
                  whole((S, LANES)), whole((S // LANES, d, LANES)),
                  whole((S, LANES)), whole((S // LANES, d, LANES)),
                  pl.BlockSpec((1, kp, 16, tq), lambda b, kv, i: (b, kv, 0, i)),
                  pl.BlockSpec((1, tq, kp * G * d), lambda b, kv, i: (b, i, kv))],
        out_specs=pl.BlockSpec((1, tq, kp * G * d), lambda b, kv, i: (b, i, kv)),
        scratch_shapes=[pltpu.VMEM((kp, G * tq, LANES), BF16),
                        pltpu.VMEM((2, kp, NSA_SEL_TK, G * tq), F32),
                        pltpu.VMEM((kp, 1, G * tq), F32), pltpu.VMEM((kp, 1, G * tq), F32),
                        pltpu.VMEM((kp, d, G * tq), F32), pltpu.VMEM((kp, d, G * tq), F32),
                        pltpu.VMEM((kp, d, G * tq), F32)],
        compiler_params=_params(3),
        name="nsa_attn",
    )(q, k_c, v_cT, ovT, ks, vsT, kw, vwT, gT, sg.reshape(B, S, W))
    return og.reshape(T, W), x2


def kernel(x, c, positions, norm_g, ada_w, ada_b, mla_w_in, mla_q_norm_g, mla_kv_norm_g, mla_w_q_b, mla_w_kv_b, mla_w_out, swa_w_in, swa_sinks, swa_w_out, nsa_w_in, nsa_cmp_pos, nsa_w_cmp_k1, nsa_w_cmp_k2, nsa_w_cmp_v1, nsa_w_cmp_v2, nsa_w_out, final_norm_g):
    B, S, D = x.shape
    depth = norm_g.shape[0]
    cos, sin = _rope_tables(positions)
    mod = _ada_modulation(c, ada_w, ada_b)
    x2 = x.reshape(B * S, D)
    pending = None
    for i in range(depth):
        shift, scale, gate = mod[i, :, :D], mod[i, :, D:2 * D], mod[i, :, 2 * D:]
        kind, j = i % 3, i // 3
        if kind == 0:
            og, x2 = _mla_layer(x2, pending, B, S, norm_g[i], scale, shift, cos, sin, mla_w_in[j],
                                mla_q_norm_g[j], mla_kv_norm_g[j], mla_w_q_b[j], mla_w_kv_b[j])
            w_out = mla_w_out[j]
        elif kind == 1:
            og, x2 = _swa_layer(x2, pending, B, S, norm_g[i], scale, shift, cos, sin, swa_w_in[j], swa_sinks[j])
            w_out = swa_w_out[j]
        else:
            og, x2 = _nsa_layer(x2, pending, B, S, norm_g[i], scale, shift, cos, sin, nsa_w_in[j],
                                nsa_cmp_pos[j], nsa_w_cmp_k1[j], nsa_w_cmp_k2[j], nsa_w_cmp_v1[j],
                                nsa_w_cmp_v2[j])
            w_out = nsa_w_out[j]
        pending = (og, w_out, gate)
    return _final_out_proj(*pending[:2], x2, pending[2], final_norm_g, S).reshape(B, S, D)
```
